```python
import jax, jax.numpy as jnp
from jax import lax
import numpy as np

D_MODEL = 1024
BATCH = 2
SEQ = 16384
DEPTH = 2
DEC_BATCH = 16
DEC_SEQ = 64
PAST_LEN = 2048

CHUNK = 64
HEAD_DIM = 64
N_HEADS_A = 8
N_HEADS_B = 8
N_KV_B = 2
GROUP_B = N_HEADS_B // N_KV_B
WIDTH_A = N_HEADS_A * HEAD_DIM
WIDTH_B = N_HEADS_B * HEAD_DIM
KV_WIDTH_B = N_KV_B * HEAD_DIM
MIX_WIDTH = WIDTH_A + WIDTH_B
IN_WIDTH = 3 * WIDTH_A + WIDTH_B + 2 * KV_WIDTH_B
PREV_CHUNKS_A = 8
A_ROWS = PREV_CHUNKS_A * CHUNK
REL_CLIP = 256
WINDOW_B = 128
PREV_CHUNKS_B = WINDOW_B // CHUNK
ROT_DIM = HEAD_DIM // 4
ROPE_THETA = 500000.0
D_FF = -(-8 * D_MODEL // (3 * 256)) * 256
D_PLE = 256
RMS_EPS = 1e-6
SCALE = HEAD_DIM ** -0.5

kernel_name = "hybrid_chunk_stream_encoder_step"


def rms_norm(x, g):
    xf = x.astype(jnp.float32)
    y = xf * lax.rsqrt(jnp.mean(xf * xf, axis=-1, keepdims=True) + RMS_EPS)
    return (y * g.astype(jnp.float32)).astype(x.dtype)


def partial_rope(x, pos):
    inv = ROPE_THETA ** (-jnp.arange(0, ROT_DIM, 2, dtype=jnp.float32) / ROT_DIM)
    ang = pos.astype(jnp.float32)[:, None] * inv[None, :]
    cos = jnp.cos(ang)[:, None, :]
    sin = jnp.sin(ang)[:, None, :]
    xr = x[..., :ROT_DIM].astype(jnp.float32)
    x1, x2 = xr[..., :ROT_DIM // 2], xr[..., ROT_DIM // 2:]
    rot = jnp.concatenate([x1 * cos - x2 * sin, x2 * cos + x1 * sin], axis=-1).astype(x.dtype)
    return jnp.concatenate([rot, x[..., ROT_DIM:]], axis=-1)


def chunk_band(x, n_prev):
    b, s = x.shape[0], x.shape[1]
    nc = s // CHUNK
    xc = x.reshape(b, nc, CHUNK, *x.shape[2:])
    xp = jnp.pad(xc, [(0, 0), (n_prev, 0)] + [(0, 0)] * (xc.ndim - 2))
    return jnp.concatenate([xp[:, j:j + nc] for j in range(n_prev + 1)], axis=2)


def band_valid(nc, n_prev):
    c = jnp.arange(nc)[:, None]
    j = jnp.arange((n_prev + 1) * CHUNK)[None, :]
    return (c - n_prev + j // CHUNK) >= 0


def rel_attention(q, k, v, dist, rel_bias, valid):
    s = jnp.einsum('...qhd,...khd->...hqk', q, k, preferred_element_type=jnp.float32) * SCALE
    s = s + rel_bias.astype(jnp.float32)[:, jnp.clip(dist, -REL_CLIP, REL_CLIP) + REL_CLIP]
    if valid is not None:
        s = jnp.where(valid, s, -jnp.inf)
    p = jax.nn.softmax(s, axis=-1).astype(v.dtype)
    return jnp.einsum('...hqk,...khd->...qhd', p, v)


def sink_gqa_attention(q, k, v, sinks, valid):
    qg = q.reshape(*q.shape[:-2], N_KV_B, GROUP_B, HEAD_DIM)
    s = jnp.einsum('...qhgd,...khd->...hgqk', qg, k, preferred_element_type=jnp.float32) * SCALE
    if valid is not None:
        s = jnp.where(valid, s, -jnp.inf)
    sk = sinks.astype(jnp.float32).reshape(N_KV_B, GROUP_B)[:, :, None, None]
    m = jnp.maximum(jnp.max(s, axis=-1, keepdims=True), sk)
    e = jnp.exp(s - m)
    p = (e / (jnp.sum(e, axis=-1, keepdims=True) + jnp.exp(sk - m))).astype(v.dtype)
    o = jnp.einsum('...hgqk,...khd->...qhgd', p, v)
    return o.reshape(*o.shape[:-3], N_HEADS_B, HEAD_DIM)


def mixer_inputs(h, g_norm, w_in):
    n = rms_norm(h, g_norm)
    z = n @ w_in
    b, s = h.shape[0], h.shape[1]
    o1 = WIDTH_A; o2 = 2 * WIDTH_A; o3 = 3 * WIDTH_A; o4 = o3 + WIDTH_B; o5 = o4 + KV_WIDTH_B
    qa = z[..., :o1].reshape(b, s, N_HEADS_A, HEAD_DIM)
    ka = z[..., o1:o2].reshape(b, s, N_HEADS_A, HEAD_DIM)
    va = z[..., o2:o3].reshape(b, s, N_HEADS_A, HEAD_DIM)
    qb = z[..., o3:o4].reshape(b, s, N_HEADS_B, HEAD_DIM)
    kb = z[..., o4:o5].reshape(b, s, N_KV_B, HEAD_DIM)
    vb = z[..., o5:].reshape(b, s, N_KV_B, HEAD_DIM)
    return qa, ka, va, qb, kb, vb


def layer_tail(h, oa, ob, p_i, g_out_a, g_out_b, w_out, g_ffn, w_gate_up, w_down, w_ple_proj, w_ple_gate):
    o = jnp.concatenate([rms_norm(oa, g_out_a), rms_norm(ob, g_out_b)], axis=-1)
    h = h + o @ w_out
    gu = rms_norm(h, g_ffn) @ w_gate_up
    h = h + (jax.nn.silu(gu[..., :D_FF]) * gu[..., D_FF:]) @ w_down
    return h + jax.nn.sigmoid(h @ w_ple_gate) * (p_i @ w_ple_proj)


def setup_inputs(seed: int = 0) -> dict:
    key = jax.random.key(seed)
    ks = jax.random.split(key, 24)
    f32 = jnp.float32
    la = min(A_ROWS, PAST_LEN)
    lb = min(WINDOW_B, PAST_LEN)
    nrm = lambda k, shape, sc: jax.random.normal(k, shape, f32) * sc
    return {
        "x_prompt": nrm(ks[0], (BATCH, SEQ, D_MODEL), 1.0),
        "x_sample": nrm(ks[1], (DEC_BATCH, DEC_SEQ, D_MODEL), 1.0),
        "p_prompt": nrm(ks[2], (DEPTH, BATCH, SEQ, D_PLE), 1.0),
        "p_sample": nrm(ks[3], (DEPTH, DEC_BATCH, DEC_SEQ, D_PLE), 1.0),
        "cache_a_k": nrm(ks[4], (DEPTH, DEC_BATCH, la, N_HEADS_A, HEAD_DIM), 1.0),
        "cache_a_v": nrm(ks[5], (DEPTH, DEC_BATCH, la, N_HEADS_A, HEAD_DIM), 1.0),
        "cache_b_k": nrm(ks[6], (DEPTH, DEC_BATCH, lb, N_KV_B, HEAD_DIM), 1.0),
        "cache_b_v": nrm(ks[7], (DEPTH, DEC_BATCH, lb, N_KV_B, HEAD_DIM), 1.0),
        "g_mix_norm": 1.0 + nrm(ks[8], (DEPTH, D_MODEL), 0.05),
        "w_in": nrm(ks[9], (DEPTH, D_MODEL, IN_WIDTH), D_MODEL ** -0.5),
        "rel_bias_a": nrm(ks[10], (DEPTH, N_HEADS_A, 2 * REL_CLIP + 1), 0.2),
        "sinks_b": nrm(ks[11], (DEPTH, N_HEADS_B), 0.5),
        "g_out_a": 1.0 + nrm(ks[12], (DEPTH, WIDTH_A), 0.05),
        "g_out_b": 1.0 + nrm(ks[13], (DEPTH, WIDTH_B), 0.05),
        "w_out": nrm(ks[14], (DEPTH, MIX_WIDTH, D_MODEL), MIX_WIDTH ** -0.5),
        "g_ffn_norm": 1.0 + nrm(ks[15], (DEPTH, D_MODEL), 0.05),
        "w_gate_up": nrm(ks[16], (DEPTH, D_MODEL, 2 * D_FF), D_MODEL ** -0.5),
        "w_down": nrm(ks[17], (DEPTH, D_FF, D_MODEL), D_FF ** -0.5),
        "w_ple_proj": nrm(ks[18], (DEPTH, D_PLE, D_MODEL), D_PLE ** -0.5),
        "w_ple_gate": nrm(ks[19], (DEPTH, D_MODEL, D_MODEL), D_MODEL ** -0.5),
        "g_final": 1.0 + nrm(ks[20], (D_MODEL,), 0.05),
    }


def reference(x_prompt, x_sample, p_prompt, p_sample, cache_a_k, cache_a_v, cache_b_k, cache_b_v,
              g_mix_norm, w_in, rel_bias_a, sinks_b, g_out_a, g_out_b, w_out, g_ffn_norm,
              w_gate_up, w_down, w_ple_proj, w_ple_gate, g_final):
    b_p, s_p = x_prompt.shape[0], x_prompt.shape[1]
    b_s, t_s = x_sample.shape[0], x_sample.shape[1]
    nc = s_p // CHUNK
    la_c = cache_a_k.shape[2]
    lb_c = cache_b_k.shape[2]
    keep_a_p = min(A_ROWS, s_p)
    keep_b_p = min(WINDOW_B, s_p)
    keep_a_s = min(A_ROWS, la_c + t_s)
    keep_b_s = min(WINDOW_B, lb_c + t_s)

    pos_p = jnp.arange(s_p)
    la_band = (PREV_CHUNKS_A + 1) * CHUNK
    dist_p = PREV_CHUNKS_A * CHUNK + jnp.arange(CHUNK)[:, None] - jnp.arange(la_band)[None, :]
    valid_a = band_valid(nc, PREV_CHUNKS_A)[:, None, None, :]
    valid_b = band_valid(nc, PREV_CHUNKS_B)[:, None, None, None, :]
    qpos_s = PAST_LEN + jnp.arange(t_s)
    kpos_s = PAST_LEN - la_c + jnp.arange(la_c + t_s)
    dist_s = qpos_s[:, None] - kpos_s[None, :]

    hp, hs = x_prompt, x_sample
    nak_p, nav_p, nbk_p, nbv_p = [], [], [], []
    nak_s, nav_s, nbk_s, nbv_s = [], [], [], []
    for i in range(DEPTH):
        tail = (g_out_a[i], g_out_b[i], w_out[i], g_ffn_norm[i], w_gate_up[i], w_down[i],
                w_ple_proj[i], w_ple_gate[i])
        qa, ka, va, qb, kb, vb = mixer_inputs(hp, g_mix_norm[i], w_in[i])
        qb = partial_rope(qb, pos_p)
        kb = partial_rope(kb, pos_p)
        oa = rel_attention(qa.reshape(b_p, nc, CHUNK, N_HEADS_A, HEAD_DIM),
                           chunk_band(ka, PREV_CHUNKS_A), chunk_band(va, PREV_CHUNKS_A),
                           dist_p, rel_bias_a[i], valid_a).reshape(b_p, s_p, WIDTH_A)
        ob = sink_gqa_attention(qb.reshape(b_p, nc, CHUNK, N_HEADS_B, HEAD_DIM),
                                chunk_band(kb, PREV_CHUNKS_B), chunk_band(vb, PREV_CHUNKS_B),
                                sinks_b[i], valid_b).reshape(b_p, s_p, WIDTH_B)
        hp = layer_tail(hp, oa, ob, p_prompt[i], *tail)
        nak_p.append(ka[:, s_p - keep_a_p:])
        nav_p.append(va[:, s_p - keep_a_p:])
        nbk_p.append(kb[:, s_p - keep_b_p:])
        nbv_p.append(vb[:, s_p - keep_b_p:])

        qa, ka, va, qb, kb, vb = mixer_inputs(hs, g_mix_norm[i], w_in[i])
        ka_all = jnp.concatenate([cache_a_k[i], ka], axis=1)
        va_all = jnp.concatenate([cache_a_v[i], va], axis=1)
        oa = rel_attention(qa, ka_all, va_all, dist_s, rel_bias_a[i], None).reshape(b_s, t_s, WIDTH_A)
        qb = partial_rope(qb, qpos_s)
        kb = partial_rope(kb, qpos_s)
        kb_all = jnp.concatenate([cache_b_k[i], kb], axis=1)
        vb_all = jnp.concatenate([cache_b_v[i], vb], axis=1)
        ob = sink_gqa_attention(qb, kb_all, vb_all, sinks_b[i], None).reshape(b_s, t_s, WIDTH_B)
        hs = layer_tail(hs, oa, ob, p_sample[i], *tail)
        nak_s.append(ka_all[:, la_c + t_s - keep_a_s:])
        nav_s.append(va_all[:, la_c + t_s - keep_a_s:])
        nbk_s.append(kb_all[:, lb_c + t_s - keep_b_s:])
        nbv_s.append(vb_all[:, lb_c + t_s - keep_b_s:])

    y_prompt = rms_norm(hp, g_final)
    y_sample = rms_norm(hs, g_final)
    return (y_prompt, y_sample,
            jnp.stack(nak_p), jnp.stack(nav_p), jnp.stack(nbk_p), jnp.stack(nbv_p),
            jnp.stack(nak_s), jnp.stack(nav_s), jnp.stack(nbk_s), jnp.stack(nbv_s))
```

```python
import functools

import numpy as np
import jax
import jax.numpy as jnp
from jax import lax
from jax.experimental import pallas as pl
from jax.experimental.pallas import tpu as pltpu

D_MODEL = 1024
CHUNK = 64
HEAD_DIM = 64
N_HEADS_A = 8
N_HEADS_B = 8
N_KV_B = 2
WIDTH_A = N_HEADS_A * HEAD_DIM
WIDTH_B = N_HEADS_B * HEAD_DIM
KV_WIDTH_B = N_KV_B * HEAD_DIM
MIX_WIDTH = WIDTH_A + WIDTH_B
IN_WIDTH = 3 * WIDTH_A + WIDTH_B + 2 * KV_WIDTH_B
PREV_CHUNKS_A = 8
A_ROWS = PREV_CHUNKS_A * CHUNK
REL_CLIP = 256
WINDOW_B = 128
PREV_CHUNKS_B = WINDOW_B // CHUNK
ROT_DIM = HEAD_DIM // 4
ROPE_THETA = 500000.0
D_FF = 2816
D_PLE = 256
PAST_LEN = 2048
RMS_EPS = 1e-6
SCALE = HEAD_DIM ** -0.5
PAIR = 2 * HEAD_DIM
NEG_INF = float("-inf")

V7X_VMEM_LIMIT_BYTES = 56 * 1024 * 1024

BF16 = jnp.bfloat16
F32 = jnp.float32


def _rms(x, g):
    return x * lax.rsqrt(jnp.mean(x * x, axis=-1, keepdims=True) + RMS_EPS) * g


def _dot(a, b):
    return jnp.dot(a, b, preferred_element_type=F32)


def _dot_nt(a, b):
    return lax.dot_general(a, b, (((1,), (1,)), ((), ())), preferred_element_type=F32)


def _resident(shape):
    nd = len(shape)
    return pl.BlockSpec(shape, lambda *_: (0,) * nd, pipeline_mode=pl.Buffered(1))


def _in_proj_kernel(x_ref, g_ref, w_ref, cos_ref, sdn_ref, sup_ref,
                    qa_ref, ka_ref, va_ref, qb_ref, kb_ref, vb_ref,
                    cak_ref, cav_ref, cbk_ref, cbv_ref, *, tiles_per_seq, rows_b):
    tm = x_ref.shape[0]
    n = _rms(x_ref[...], g_ref[...]).astype(BF16)
    z = _dot(n, w_ref[...])
    o1, o2, o3 = WIDTH_A, 2 * WIDTH_A, 3 * WIDTH_A
    o4 = o3 + WIDTH_B
    o5 = o4 + KV_WIDTH_B
    cos, sdn, sup = cos_ref[...], sdn_ref[...], sup_ref[...]

    def rope(x):
        half = ROT_DIM // 2
        return (x * cos + pltpu.roll(x, PAIR - half, axis=1) * sdn
                + pltpu.roll(x, half, axis=1) * sup)

    ka = z[:, o1:o2]
    va = z[:, o2:o3]
    kb = rope(z[:, o4:o5])
    vb = z[:, o5:]
    qa_ref[...] = (z[:, :o1] * SCALE).astype(BF16)
    ka_ref[...] = ka.astype(BF16)
    va_ref[...] = va.astype(BF16)
    for j in range(WIDTH_B // PAIR):
        sl = slice(o3 + j * PAIR, o3 + (j + 1) * PAIR)
        qb_ref[:, j * PAIR:(j + 1) * PAIR] = (rope(z[:, sl]) * SCALE).astype(BF16)
    kb_ref[:, :PAIR] = kb.astype(BF16)
    kb_ref[:, PAIR:] = pltpu.roll(kb, HEAD_DIM, axis=1).astype(BF16)
    vb_ref[:, :PAIR] = vb.astype(BF16)
    vb_ref[:, PAIR:] = pltpu.roll(vb, HEAD_DIM, axis=1).astype(BF16)

    @pl.when(pl.program_id(0) % tiles_per_seq == tiles_per_seq - 1)
    def _():
        cak_ref[...] = ka
        cav_ref[...] = va
        cbk_ref[...] = kb[tm - rows_b:, :]
        cbv_ref[...] = vb[tm - rows_b:, :]


def _in_proj(x, g, w_bf16, cos, sdn, sup, *, tm, tiles_per_seq, rows_b, pos_tiles):
    n_tok = x.shape[0]
    n_tiles = n_tok // tm
    n_seq = n_tiles // tiles_per_seq
    row = lambda i: (i, 0)
    tab = pl.BlockSpec((tm, PAIR), lambda i: (i % pos_tiles, 0))
    seq = lambda i: (i // tiles_per_seq, 0)
    kernel = functools.partial(_in_proj_kernel, tiles_per_seq=tiles_per_seq, rows_b=rows_b)
    return pl.pallas_call(
        kernel,
        grid=(n_tiles,),
        in_specs=[
            pl.BlockSpec((tm, D_MODEL), row),
            _resident((1, D_MODEL)),
            _resident((D_MODEL, IN_WIDTH)),
            tab, tab, tab,
        ],
        out_specs=[
            pl.BlockSpec((tm, WIDTH_A), row),
            pl.BlockSpec((tm, WIDTH_A), row),
            pl.BlockSpec((tm, WIDTH_A), row),
            pl.BlockSpec((tm, WIDTH_B), row),
            pl.BlockSpec((tm, 2 * KV_WIDTH_B), row),
            pl.BlockSpec((tm, 2 * KV_WIDTH_B), row),
            pl.BlockSpec((tm, WIDTH_A), seq),
            pl.BlockSpec((tm, WIDTH_A), seq),
            pl.BlockSpec((rows_b, KV_WIDTH_B), seq),
            pl.BlockSpec((rows_b, KV_WIDTH_B), seq),
        ],
        out_shape=[
            jax.ShapeDtypeStruct((n_tok, WIDTH_A), BF16),
            jax.ShapeDtypeStruct((n_tok, WIDTH_A), BF16),
            jax.ShapeDtypeStruct((n_tok, WIDTH_A), BF16),
            jax.ShapeDtypeStruct((n_tok, WIDTH_B), BF16),
            jax.ShapeDtypeStruct((n_tok, 2 * KV_WIDTH_B), BF16),
            jax.ShapeDtypeStruct((n_tok, 2 * KV_WIDTH_B), BF16),
            jax.ShapeDtypeStruct((n_seq * tm, WIDTH_A), F32),
            jax.ShapeDtypeStruct((n_seq * tm, WIDTH_A), F32),
            jax.ShapeDtypeStruct((n_seq * rows_b, KV_WIDTH_B), F32),
            jax.ShapeDtypeStruct((n_seq * rows_b, KV_WIDTH_B), F32),
        ],
        compiler_params=pltpu.CompilerParams(
            dimension_semantics=("arbitrary",), vmem_limit_bytes=V7X_VMEM_LIMIT_BYTES),
        name="in_proj",
    )(x, g, w_bf16, cos, sdn, sup)


def _lane_masks(rows):
    lane = lax.broadcasted_iota(jnp.int32, (rows, PAIR), 1)
    return lane < HEAD_DIM, lane >= HEAD_DIM


def _split_pair(x):
    lo, hi = _lane_masks(x.shape[0])
    zero = jnp.zeros_like(x)
    return jnp.where(lo, x, zero), jnp.where(hi, x, zero)


def _softmax_pv(s_blocks, v_blocks, sink=None):
    m = functools.reduce(jnp.maximum, [jnp.max(s, axis=-1, keepdims=True) for s in s_blocks])
    if sink is not None:
        m = jnp.maximum(m, sink)
    es = [jnp.exp(s - m) for s in s_blocks]
    den = functools.reduce(jnp.add, [jnp.sum(e, axis=-1, keepdims=True) for e in es])
    if sink is not None:
        den = den + jnp.exp(sink - m)
    o = functools.reduce(jnp.add, [_dot(e.astype(BF16), v) for e, v in zip(es, v_blocks)])
    return o / den


def _attention_a(q, k_blocks, v_blocks, bias_fn, valid):
    outs = []
    for p in range(WIDTH_A // PAIR):
        sl = slice(p * PAIR, (p + 1) * PAIR)
        qp = q[:, sl]
        ks = [_split_pair(k[:, sl]) for k in k_blocks]
        vs = [_split_pair(v[:, sl]) for v in v_blocks]
        o_pair = None
        for half in range(2):
            h = 2 * p + half
            s_blocks = []
            for j in range(len(k_blocks)):
                s = _dot_nt(qp, ks[j][half]) + bias_fn(h, j)
                if valid[j] is not None:
                    s = jnp.where(valid[j], s, NEG_INF)
                s_blocks.append(s)
            o = _softmax_pv(s_blocks, [v[half] for v in vs])
            o_pair = o if o_pair is None else o_pair + o
        outs.append(o_pair)
    return jnp.concatenate(outs, axis=-1)


def _attention_b(q, k2_blocks, v2_blocks, mask_fn, valid, sinks_ref):
    variants = []
    for g in range(N_KV_B):
        per_half = []
        for half in range(2):
            off = 0 if g == half else KV_WIDTH_B
            kv = []
            for k2, v2 in zip(k2_blocks, v2_blocks):
                kk = _split_pair(k2[:, off:off + PAIR])[half]
                vv = _split_pair(v2[:, off:off + PAIR])[half]
                kv.append((kk, vv))
            per_half.append(kv)
        variants.append(per_half)
    outs = []
    for p in range(WIDTH_B // PAIR):
        qp = q[:, p * PAIR:(p + 1) * PAIR]
        g = (2 * p) // (N_HEADS_B // N_KV_B)
        o_pair = None
        for half in range(2):
            h = 2 * p + half
            kv = variants[g][half]
            s_blocks = []
            for j in range(len(kv)):
                s = _dot_nt(qp, kv[j][0])
                mk = mask_fn(j)
                if mk is not None:
                    s = s + mk
                if valid[j] is not None:
                    s = jnp.where(valid[j], s, NEG_INF)
                s_blocks.append(s)
            o = _softmax_pv(s_blocks, [c[1] for c in kv], sink=sinks_ref[h])
            o_pair = o if o_pair is None else o_pair + o
        outs.append(o_pair)
    return jnp.concatenate(outs, axis=-1)


TQ = 256
NBLK_A = A_ROWS // TQ + 1


def _attn_prompt_kernel(sinks_ref, qa_ref, ka0_ref, ka1_ref, ka2_ref, va0_ref, va1_ref, va2_ref,
                        qb_ref, kbp_ref, kbc_ref, vbp_ref, vbc_ref,
                        bias_ref, maskb_ref, goa_ref, gob_ref, o_ref):
    t = pl.program_id(1)
    k_refs = (ka0_ref, ka1_ref, ka2_ref)
    v_refs = (va0_ref, va1_ref, va2_ref)
    valid_a = [t >= NBLK_A - 1 - j for j in range(NBLK_A - 1)] + [None]
    oa = _attention_a(
        qa_ref[0], [r[0] for r in k_refs], [r[0] for r in v_refs],
        lambda h, j: bias_ref[h, :, j * TQ:(j + 1) * TQ], valid_a)
    ob = _attention_b(
        qb_ref[0], [kbp_ref[0], kbc_ref[0]], [vbp_ref[0], vbc_ref[0]],
        lambda j: maskb_ref[:, :WINDOW_B] if j == 0 else maskb_ref[:, WINDOW_B:],
        [t >= 1, None], sinks_ref)
    o_ref[0, :, :WIDTH_A] = _rms(oa, goa_ref[...]).astype(BF16)
    o_ref[0, :, WIDTH_A:] = _rms(ob, gob_ref[...]).astype(BF16)


def _attn_prompt(sinks, qa, ka, va, qb, kb2, vb2, bias, maskb, goa, gob):
    b, s, _ = qa.shape
    nt = s // TQ
    cur = lambda bi, t: (bi, t, 0)
    back = lambda d: (lambda bi, t: (bi, jnp.maximum(t - d, 0), 0))
    prev_b = lambda bi, t: (bi, jnp.maximum(t * (TQ // WINDOW_B) - 1, 0), 0)
    blk_a = lambda im: pl.BlockSpec((1, TQ, WIDTH_A), im)
    return pl.pallas_call(
        _attn_prompt_kernel,
        grid=(b, nt),
        in_specs=[
            pl.BlockSpec(memory_space=pltpu.SMEM),
            blk_a(cur),
            blk_a(back(2)), blk_a(back(1)), blk_a(cur),
            blk_a(back(2)), blk_a(back(1)), blk_a(cur),
            pl.BlockSpec((1, TQ, WIDTH_B), cur),
            pl.BlockSpec((1, WINDOW_B, 2 * KV_WIDTH_B), prev_b),
            pl.BlockSpec((1, TQ, 2 * KV_WIDTH_B), cur),
            pl.BlockSpec((1, WINDOW_B, 2 * KV_WIDTH_B), prev_b),
            pl.BlockSpec((1, TQ, 2 * KV_WIDTH_B), cur),
            _resident(bias.shape),
            _resident(maskb.shape),
            _resident((1, WIDTH_A)),
            _resident((1, WIDTH_B)),
        ],
        out_specs=pl.BlockSpec((1, TQ, MIX_WIDTH), cur),
        out_shape=jax.ShapeDtypeStruct((b, s, MIX_WIDTH), BF16),
        compiler_params=pltpu.CompilerParams(
            dimension_semantics=("arbitrary", "arbitrary"),
            vmem_limit_bytes=V7X_VMEM_LIMIT_BYTES),
        name="attn_prompt",
    )(sinks, qa, ka, ka, ka, va, va, va, qb, kb2, kb2, vb2, vb2, bias, maskb, goa, gob)


def _swap_halves_f32(x):
    return pltpu.roll(x, HEAD_DIM, axis=1)


def _attn_sample_kernel(sinks_ref, qa_ref, kac_ref, kan_ref, vac_ref, van_ref,
                        qb_ref, kbc_ref, kbn_ref, vbc_ref, vbn_ref,
                        bias_ref, goa_ref, gob_ref, o_ref):
    la = kac_ref.shape[1]
    oa = _attention_a(
        qa_ref[...],
        [kac_ref[0].astype(BF16), kan_ref[...]], [vac_ref[0].astype(BF16), van_ref[...]],
        lambda h, j: bias_ref[h, :, :la] if j == 0 else bias_ref[h, :, la:],
        [None, None])
    kbc, vbc = kbc_ref[0], vbc_ref[0]
    k2c = jnp.concatenate([kbc, _swap_halves_f32(kbc)], axis=-1).astype(BF16)
    v2c = jnp.concatenate([vbc, _swap_halves_f32(vbc)], axis=-1).astype(BF16)
    ob = _attention_b(
        qb_ref[...], [k2c, kbn_ref[...]], [v2c, vbn_ref[...]],
        lambda j: None, [None, None], sinks_ref)
    o_ref[:, :WIDTH_A] = _rms(oa, goa_ref[...]).astype(BF16)
    o_ref[:, WIDTH_A:] = _rms(ob, gob_ref[...]).astype(BF16)


def _attn_sample(sinks, qa, ka_cache, ka, va_cache, va, qb, kb_cache, kb2, vb_cache, vb2,
                 bias, goa, gob, *, t_s):
    n_tok = qa.shape[0]
    n_seq = n_tok // t_s
    la, lb = ka_cache.shape[1], kb_cache.shape[1]
    row = lambda i: (i, 0)
    seq3 = lambda i: (i, 0, 0)
    return pl.pallas_call(
        _attn_sample_kernel,
        grid=(n_seq,),
        in_specs=[
            pl.BlockSpec(memory_space=pltpu.SMEM),
            pl.BlockSpec((t_s, WIDTH_A), row),
            pl.BlockSpec((1, la, WIDTH_A), seq3),
            pl.BlockSpec((t_s, WIDTH_A), row),
            pl.BlockSpec((1, la, WIDTH_A), seq3),
            pl.BlockSpec((t_s, WIDTH_A), row),
            pl.BlockSpec((t_s, WIDTH_B), row),
            pl.BlockSpec((1, lb, KV_WIDTH_B), seq3),
            pl.BlockSpec((t_s, 2 * KV_WIDTH_B), row),
            pl.BlockSpec((1, lb, KV_WIDTH_B), seq3),
            pl.BlockSpec((t_s, 2 * KV_WIDTH_B), row),
            _resident(bias.shape),
            _resident((1, WIDTH_A)),
            _resident((1, WIDTH_B)),
        ],
        out_specs=pl.BlockSpec((t_s, MIX_WIDTH), row),
        out_shape=jax.ShapeDtypeStruct((n_tok, MIX_WIDTH), BF16),
        compiler_params=pltpu.CompilerParams(
            dimension_semantics=("arbitrary",), vmem_limit_bytes=V7X_VMEM_LIMIT_BYTES),
        name="attn_sample",
    )(sinks, qa, ka_cache, ka, va_cache, va, qb, kb_cache, kb2, vb_cache, vb2, bias, goa, gob)


def _tail_kernel(h_ref, o_ref, p_ref, wout_ref, gffn_ref, wgu_ref, wdown_ref, wgate_ref,
                 wproj_ref, gfin_ref, out_ref, *, final):
    h = h_ref[...] + _dot(o_ref[...], wout_ref[...])
    gu = _dot(_rms(h, gffn_ref[...]).astype(BF16), wgu_ref[...])
    act = (jax.nn.silu(gu[:, :D_FF]) * gu[:, D_FF:]).astype(BF16)
    h = h + _dot(act, wdown_ref[...])
    gate = jax.nn.sigmoid(_dot(h.astype(BF16), wgate_ref[...]))
    h = h + gate * _dot(p_ref[...].astype(BF16), wproj_ref[...])
    if final:
        h = _rms(h, gfin_ref[...])
    out_ref[...] = h


def _tail(h, o, p, wout, gffn, wgu, wdown, wgate, wproj, gfin, *, tm, final):
    n_tok = h.shape[0]
    row = lambda i: (i, 0)
    return pl.pallas_call(
        functools.partial(_tail_kernel, final=final),
        grid=(n_tok // tm,),
        in_specs=[
            pl.BlockSpec((tm, D_MODEL), row),
            pl.BlockSpec((tm, MIX_WIDTH), row),
            pl.BlockSpec((tm, D_PLE), row),
            _resident((MIX_WIDTH, D_MODEL)),
            _resident((1, D_MODEL)),
            _resident((D_MODEL, 2 * D_FF)),
            _resident((D_FF, D_MODEL)),
            _resident((D_MODEL, D_MODEL)),
            _resident((D_PLE, D_MODEL)),
            _resident((1, D_MODEL)),
        ],
        out_specs=pl.BlockSpec((tm, D_MODEL), row),
        out_shape=jax.ShapeDtypeStruct((n_tok, D_MODEL), F32),
        compiler_params=pltpu.CompilerParams(
            dimension_semantics=("arbitrary",), vmem_limit_bytes=V7X_VMEM_LIMIT_BYTES),
        name="layer_tail",
    )(h, o, p, wout, gffn, wgu, wdown, wgate, wproj, gfin)


def _rope_tables(pos):
    half = ROT_DIM // 2
    inv = ROPE_THETA ** (-jnp.arange(0, ROT_DIM, 2, dtype=F32) / ROT_DIM)
    ang = pos.astype(F32)[:, None] * inv[None, :]
    cos, sin = jnp.cos(ang), jnp.sin(ang)
    s = pos.shape[0]
    ones = jnp.ones((s, HEAD_DIM - ROT_DIM), F32)
    zeros = jnp.zeros((s, HEAD_DIM - ROT_DIM), F32)
    zh = jnp.zeros((s, half), F32)
    cos_h = jnp.concatenate([cos, cos, ones], axis=1)
    sdn_h = jnp.concatenate([-sin, zh, zeros], axis=1)
    sup_h = jnp.concatenate([zh, sin, zeros], axis=1)
    tile2 = lambda a: jnp.concatenate([a, a], axis=1)
    return tile2(cos_h), tile2(sdn_h), tile2(sup_h)


def _band_tables(n_q_chunks, n_w_chunks, prev_chunks):
    r = np.arange(n_q_chunks * CHUNK)
    c = np.arange(n_w_chunks * CHUNK)
    rel = (c // CHUNK)[None, :] - (r // CHUNK)[:, None]
    valid = (rel >= 0) & (rel <= prev_chunks)
    dist = prev_chunks * CHUNK + (r % CHUNK)[:, None] - (rel * CHUNK + (c % CHUNK)[None, :])
    return valid, np.clip(dist, -REL_CLIP, REL_CLIP) + REL_CLIP


def _bias_a(rel_bias, n_q_chunks, n_w_chunks):
    valid, idx = _band_tables(n_q_chunks, n_w_chunks, PREV_CHUNKS_A)
    tab = rel_bias.astype(F32)[:, idx]
    return jnp.where(valid[None], tab, NEG_INF)


def _mask_b(n_q_chunks, n_w_chunks):
    valid, _ = _band_tables(n_q_chunks, n_w_chunks, PREV_CHUNKS_B)
    return jnp.asarray(np.where(valid, 0.0, NEG_INF), F32)


def kernel(x_prompt, x_sample, p_prompt, p_sample, cache_a_k, cache_a_v, cache_b_k, cache_b_v,
           g_mix_norm, w_in, rel_bias_a, sinks_b, g_out_a, g_out_b, w_out, g_ffn_norm,
           w_gate_up, w_down, w_ple_proj, w_ple_gate, g_final):
    b_p, s_p, _ = x_prompt.shape
    b_s, t_s, _ = x_sample.shape
    depth = w_in.shape[0]
    la_c, lb_c = cache_a_k.shape[2], cache_b_k.shape[2]
    keep_a_s = min(A_ROWS, la_c + t_s)
    keep_b_s = min(WINDOW_B, lb_c + t_s)
    tm_p = A_ROWS
    tm_s = 512
    tm_tail = 256
    assert s_p % tm_p == 0 and (b_s * t_s) % tm_s == 0 and tm_s % t_s == 0
    assert la_c == A_ROWS and lb_c == WINDOW_B and t_s == CHUNK and s_p >= A_ROWS

    cos_p, sdn_p, sup_p = _rope_tables(jnp.arange(s_p))
    cos_s, sdn_s, sup_s = (jnp.tile(a, (tm_s // t_s, 1))
                           for a in _rope_tables(PAST_LEN + jnp.arange(t_s)))
    mask_b_p = _mask_b(TQ // CHUNK, (WINDOW_B + TQ) // CHUNK)
    row2 = lambda a: a.reshape(1, -1).astype(F32)
    g_fin = row2(g_final)

    hp = x_prompt.reshape(b_p * s_p, D_MODEL)
    hs = x_sample.reshape(b_s * t_s, D_MODEL)
    outs = [[] for _ in range(8)]
    for i in range(depth):
        w_in_i = w_in[i].astype(BF16)
        tail_w = (w_out[i].astype(BF16), row2(g_ffn_norm[i]), w_gate_up[i].astype(BF16),
                  w_down[i].astype(BF16), w_ple_gate[i].astype(BF16), w_ple_proj[i].astype(BF16),
                  g_fin)
        g_mix = row2(g_mix_norm[i])
        goa, gob = row2(g_out_a[i]), row2(g_out_b[i])
        sinks = sinks_b[i].astype(F32)
        final = i == depth - 1

        qa, ka, va, qb, kb2, vb2, cak, cav, cbk, cbv = _in_proj(
            hp, g_mix, w_in_i, cos_p, sdn_p, sup_p,
            tm=tm_p, tiles_per_seq=s_p // tm_p, rows_b=WINDOW_B, pos_tiles=s_p // tm_p)
        r3 = lambda a: a.reshape(b_p, s_p, a.shape[-1])
        o = _attn_prompt(sinks, r3(qa), r3(ka), r3(va), r3(qb), r3(kb2), r3(vb2),
                         _bias_a(rel_bias_a[i], TQ // CHUNK, NBLK_A * TQ // CHUNK), mask_b_p,
                         goa, gob)
        hp = _tail(hp, o.reshape(b_p * s_p, MIX_WIDTH), p_prompt[i].reshape(b_p * s_p, D_PLE),
                   *tail_w, tm=tm_tail, final=final)
        outs[0].append(cak.reshape(b_p, A_ROWS, N_HEADS_A, HEAD_DIM))
        outs[1].append(cav.reshape(b_p, A_ROWS, N_HEADS_A, HEAD_DIM))
        outs[2].append(cbk.reshape(b_p, WINDOW_B, N_KV_B, HEAD_DIM))
        outs[3].append(cbv.reshape(b_p, WINDOW_B, N_KV_B, HEAD_DIM))

        qa, ka, va, qb, kb2, vb2, nak, nav, nbk, nbv = _in_proj(
            hs, g_mix, w_in_i, cos_s, sdn_s, sup_s,
            tm=tm_s, tiles_per_seq=1, rows_b=tm_s, pos_tiles=1)
        cak_i = cache_a_k[i].reshape(b_s, la_c, WIDTH_A)
        cav_i = cache_a_v[i].reshape(b_s, la_c, WIDTH_A)
        cbk_i = cache_b_k[i].reshape(b_s, lb_c, KV_WIDTH_B)
        cbv_i = cache_b_v[i].reshape(b_s, lb_c, KV_WIDTH_B)
        o = _attn_sample(sinks, qa, cak_i, ka, cav_i, va, qb, cbk_i, kb2, cbv_i, vb2,
                         _bias_a(rel_bias_a[i], 1, (la_c + t_s) // CHUNK), goa, gob, t_s=t_s)
        hs = _tail(hs, o, p_sample[i].reshape(b_s * t_s, D_PLE), *tail_w, tm=tm_tail, final=final)
        roll = lambda cache, new, keep, shp: jnp.concatenate(
            [cache, new.reshape(b_s, t_s, -1)], axis=1)[:, -keep:].reshape(b_s, keep, *shp)
        outs[4].append(roll(cak_i, nak, keep_a_s, (N_HEADS_A, HEAD_DIM)))
        outs[5].append(roll(cav_i, nav, keep_a_s, (N_HEADS_A, HEAD_DIM)))
        outs[6].append(roll(cbk_i, nbk, keep_b_s, (N_KV_B, HEAD_DIM)))
        outs[7].append(roll(cbv_i, nbv, keep_b_s, (N_KV_B, HEAD_DIM)))

    y_prompt = hp.reshape(b_p, s_p, D_MODEL)
    y_sample = hs.reshape(b_s, t_s, D_MODEL)
    return (y_prompt, y_sample) + tuple(jnp.stack(o) for o in outs)
```

```python
import functools
import math

import numpy as np
import jax
import jax.numpy as jnp
from jax import lax
from jax.experimental import pallas as pl
from jax.experimental.pallas import tpu as pltpu

D_MODEL = 1024
CHUNK = 64
HEAD_DIM = 64
N_HEADS_A = 8
N_HEADS_B = 8
N_KV_B = 2
GROUP_B = N_HEADS_B // N_KV_B
WIDTH_A = N_HEADS_A * HEAD_DIM
WIDTH_B = N_HEADS_B * HEAD_DIM
KV_WIDTH_B = N_KV_B * HEAD_DIM
MIX_WIDTH = WIDTH_A + WIDTH_B
IN_WIDTH = 3 * WIDTH_A + WIDTH_B + 2 * KV_WIDTH_B
PREV_CHUNKS_A = 8
A_ROWS = PREV_CHUNKS_A * CHUNK
REL_CLIP = 256
WINDOW_B = 128
PREV_CHUNKS_B = WINDOW_B // CHUNK
ROT_DIM = HEAD_DIM // 4
ROPE_THETA = 500000.0
D_FF = 2816
D_PLE = 256
PAST_LEN = 2048
RMS_EPS = 1e-6
LOG2E = math.log2(math.e)
QSCALE = HEAD_DIM ** -0.5 * LOG2E
PAIR = 2 * HEAD_DIM
N_PAIRS_A = WIDTH_A // PAIR
N_PAIRS_B = WIDTH_B // PAIR
NEG_INF = float("-inf")
BIAS_PERIOD = 1024

V7X_VMEM_LIMIT_BYTES = 56 * 1024 * 1024

BF16 = jnp.bfloat16
F32 = jnp.float32

O_KA, O_VA, O_QB = WIDTH_A, 2 * WIDTH_A, 3 * WIDTH_A
O_KB = O_QB + WIDTH_B
O_VB = O_KB + KV_WIDTH_B


def _rms(x, g):
    return x * lax.rsqrt(jnp.mean(x * x, axis=-1, keepdims=True) + RMS_EPS) * g


def _dot(a, b):
    return jnp.dot(a, b, preferred_element_type=F32)


def _dot_nt(a, b):
    return lax.dot_general(a, b, (((1,), (1,)), ((), ())), preferred_element_type=F32)


def _resident(shape):
    nd = len(shape)
    return pl.BlockSpec(shape, lambda *_: (0,) * nd, pipeline_mode=pl.Buffered(1))


def _rope(x, cos, sdn, sup):
    half = ROT_DIM // 2
    return (x * cos + pltpu.roll(x, PAIR - half, axis=1) * sdn
            + pltpu.roll(x, half, axis=1) * sup)


def _keep_half(x, half):
    lane = lax.broadcasted_iota(jnp.int32, x.shape, 1)
    keep = (lane < HEAD_DIM) if half == 0 else (lane >= HEAD_DIM)
    return jnp.where(keep, x, jnp.zeros_like(x))


def _kv_b_variants(x):
    swapped = pltpu.roll(x, HEAD_DIM, axis=1)
    out = []
    for g in range(N_KV_B):
        for half in range(2):
            src = x if g == half else swapped
            out.append(_keep_half(src, half).astype(BF16))
    return out


def _softmax_pv(s_blocks, v_blocks, sink=None):
    m = functools.reduce(jnp.maximum, [jnp.max(s, axis=-1, keepdims=True) for s in s_blocks])
    if sink is not None:
        m = jnp.maximum(m, sink)
    es = [jnp.exp2(s - m) for s in s_blocks]
    den = functools.reduce(jnp.add, [jnp.sum(e, axis=-1, keepdims=True) for e in es])
    if sink is not None:
        den = den + jnp.exp2(sink - m)
    o = functools.reduce(jnp.add, [_dot(e.astype(BF16), v) for e, v in zip(es, v_blocks)])
    return o / den


def _attend(q, n_pairs, kv_fn, bias_fn, sink_fn=None):
    outs = []
    for p in range(n_pairs):
        qp = q[:, p * PAIR:(p + 1) * PAIR]
        o_pair = None
        for half in range(2):
            h = 2 * p + half
            kv = kv_fn(p, half)
            s_blocks = []
            for j, (k, _) in enumerate(kv):
                s = _dot_nt(qp, k)
                bias = bias_fn(h, j)
                s_blocks.append(s if bias is None else s + bias)
            o = _softmax_pv(s_blocks, [v for _, v in kv],
                            None if sink_fn is None else sink_fn(h))
            o_pair = o if o_pair is None else o_pair + o
        outs.append(o_pair)
    return jnp.concatenate(outs, axis=-1)


def _rolled_bias_rows(rbv_ref, h, rows):
    x = jnp.broadcast_to(rbv_ref[h:h + 1, :] * LOG2E, (rows, BIAS_PERIOD))
    return pltpu.roll(x, 0, axis=1, stride=1, stride_axis=0)


def _band_valid(rows, col0, cols, prev_chunks):
    r = lax.broadcasted_iota(jnp.int32, (rows, cols), 0) // CHUNK
    c = (lax.broadcasted_iota(jnp.int32, (rows, cols), 1) + col0) // CHUNK
    return (c >= r) & (c <= r + prev_chunks)


TQ = 256
NBLK_A = A_ROWS // TQ + 1
NEG_KIND = NBLK_A


def _mix_prompt_kernel(sinks_ref, x_ref, g_ref, w_ref, cos_ref, sdn_ref, sup_ref, rbv_ref,
                       goa_ref, gob_ref,
                       o_ref, cak_ref, cav_ref, cbk_ref, cbv_ref,
                       ka_scr, va_scr, kbp_scr, vbp_scr, bias_scr, maskp_scr, maskc_scr,
                       *, n_tiles):
    b, t = pl.program_id(0), pl.program_id(1)

    @pl.when((b == 0) & (t == 0))
    def _init():
        ka_scr[...] = jnp.zeros_like(ka_scr)
        va_scr[...] = jnp.zeros_like(va_scr)
        kbp_scr[...] = jnp.zeros_like(kbp_scr)
        vbp_scr[...] = jnp.zeros_like(vbp_scr)
        for h in range(N_HEADS_A):
            rows = _rolled_bias_rows(rbv_ref, h, TQ)
            for j in range(NBLK_A):
                valid = _band_valid(TQ, j * TQ, TQ, PREV_CHUNKS_A)
                bias_scr[j, h] = jnp.where(valid, rows[:, j * TQ:(j + 1) * TQ], NEG_INF)
            bias_scr[NEG_KIND, h] = jnp.full((TQ, TQ), NEG_INF, F32)
        zero = jnp.zeros((TQ, WINDOW_B), F32)
        maskp_scr[0] = jnp.where(_band_valid(TQ, 0, WINDOW_B, PREV_CHUNKS_B), zero, NEG_INF)
        maskp_scr[1] = jnp.full((TQ, WINDOW_B), NEG_INF, F32)
        maskc_scr[...] = jnp.where(_band_valid(TQ, WINDOW_B, TQ, PREV_CHUNKS_B),
                                   jnp.zeros((TQ, TQ), F32), NEG_INF)

    n = _rms(x_ref[0], g_ref[...]).astype(BF16)
    z = _dot(n, w_ref[...])
    cos, sdn, sup = cos_ref[...], sdn_ref[...], sup_ref[...]
    ka, va = z[:, O_KA:O_VA], z[:, O_VA:O_QB]
    kb = _rope(z[:, O_KB:O_VB], cos, sdn, sup)
    vb = z[:, O_VB:]
    qa = (z[:, :O_KA] * QSCALE).astype(BF16)
    qb = jnp.concatenate(
        [(_rope(z[:, O_QB + j * PAIR:O_QB + (j + 1) * PAIR], cos, sdn, sup) * QSCALE).astype(BF16)
         for j in range(N_PAIRS_B)], axis=-1)

    @pl.when(t >= n_tiles - A_ROWS // TQ)
    def _():
        cak_ref[0] = ka
        cav_ref[0] = va

    @pl.when(t == n_tiles - 1)
    def _():
        cbk_ref[0] = kb[TQ - WINDOW_B:, :]
        cbv_ref[0] = vb[TQ - WINDOW_B:, :]

    slot = lax.rem(t, NBLK_A)
    for p in range(N_PAIRS_A):
        for half in range(2):
            ka_scr[slot, 2 * p + half] = _keep_half(ka[:, p * PAIR:(p + 1) * PAIR], half).astype(BF16)
            va_scr[slot, 2 * p + half] = _keep_half(va[:, p * PAIR:(p + 1) * PAIR], half).astype(BF16)
    slots = [lax.rem(t + j + 1, NBLK_A) for j in range(NBLK_A)]
    kinds = [jnp.where(t >= NBLK_A - 1 - j, j, NEG_KIND) for j in range(NBLK_A - 1)] + [NBLK_A - 1]
    oa = _attend(
        qa, N_PAIRS_A,
        lambda p, half: [(ka_scr[slots[j], 2 * p + half], va_scr[slots[j], 2 * p + half])
                         for j in range(NBLK_A)],
        lambda h, j: bias_scr[kinds[j], h])

    kb_cur, vb_cur = _kv_b_variants(kb), _kv_b_variants(vb)
    kind_p = jnp.where(t >= 1, 0, 1)
    ob = _attend(
        qb, N_PAIRS_B,
        lambda p, half: [(kbp_scr[2 * (2 * p // GROUP_B) + half], vbp_scr[2 * (2 * p // GROUP_B) + half]),
                         (kb_cur[2 * (2 * p // GROUP_B) + half], vb_cur[2 * (2 * p // GROUP_B) + half])],
        lambda h, j: maskp_scr[kind_p] if j == 0 else maskc_scr[...],
        lambda h: sinks_ref[h] * LOG2E)
    for i in range(2 * N_KV_B):
        kbp_scr[i] = kb_cur[i][TQ - WINDOW_B:, :]
        vbp_scr[i] = vb_cur[i][TQ - WINDOW_B:, :]

    o_ref[0, :, :WIDTH_A] = _rms(oa, goa_ref[...]).astype(BF16)
    o_ref[0, :, WIDTH_A:] = _rms(ob, gob_ref[...]).astype(BF16)


def _mix_prompt(sinks, x, g, w_bf16, cos, sdn, sup, rbv, goa, gob):
    b, s, _ = x.shape
    nt = s // TQ
    cur = lambda bi, t: (bi, t, 0)
    tab = pl.BlockSpec((TQ, PAIR), lambda bi, t: (t, 0))
    keep_a = lambda bi, t: (bi, jnp.maximum(t - (nt - A_ROWS // TQ), 0), 0)
    seq = lambda bi, t: (bi, 0, 0)
    return pl.pallas_call(
        functools.partial(_mix_prompt_kernel, n_tiles=nt),
        grid=(b, nt),
        in_specs=[
            pl.BlockSpec(memory_space=pltpu.SMEM),
            pl.BlockSpec((1, TQ, D_MODEL), cur),
            _resident((1, D_MODEL)),
            _resident((D_MODEL, IN_WIDTH)),
            tab, tab, tab,
            _resident((N_HEADS_A, BIAS_PERIOD)),
            _resident((1, WIDTH_A)),
            _resident((1, WIDTH_B)),
        ],
        out_specs=[
            pl.BlockSpec((1, TQ, MIX_WIDTH), cur),
            pl.BlockSpec((1, TQ, WIDTH_A), keep_a),
            pl.BlockSpec((1, TQ, WIDTH_A), keep_a),
            pl.BlockSpec((1, WINDOW_B, KV_WIDTH_B), seq),
            pl.BlockSpec((1, WINDOW_B, KV_WIDTH_B), seq),
        ],
        out_shape=[
            jax.ShapeDtypeStruct((b, s, MIX_WIDTH), BF16),
            jax.ShapeDtypeStruct((b, A_ROWS, WIDTH_A), F32),
            jax.ShapeDtypeStruct((b, A_ROWS, WIDTH_A), F32),
            jax.ShapeDtypeStruct((b, WINDOW_B, KV_WIDTH_B), F32),
            jax.ShapeDtypeStruct((b, WINDOW_B, KV_WIDTH_B), F32),
        ],
        scratch_shapes=[
            pltpu.VMEM((NBLK_A, N_HEADS_A, TQ, PAIR), BF16),
            pltpu.VMEM((NBLK_A, N_HEADS_A, TQ, PAIR), BF16),
            pltpu.VMEM((2 * N_KV_B, WINDOW_B, PAIR), BF16),
            pltpu.VMEM((2 * N_KV_B, WINDOW_B, PAIR), BF16),
            pltpu.VMEM((NBLK_A + 1, N_HEADS_A, TQ, TQ), F32),
            pltpu.VMEM((2, TQ, WINDOW_B), F32),
            pltpu.VMEM((TQ, TQ), F32),
        ],
        compiler_params=pltpu.CompilerParams(
            dimension_semantics=("arbitrary", "arbitrary"),
            vmem_limit_bytes=V7X_VMEM_LIMIT_BYTES),
        name="mix_prompt",
    )(sinks, x, g, w_bf16, cos, sdn, sup, rbv, goa, gob)


def _in_proj_kernel(x_ref, g_ref, w_ref, cos_ref, sdn_ref, sup_ref,
                    qa_ref, ka_ref, va_ref, qb_ref, kb_ref, vb_ref):
    n = _rms(x_ref[...], g_ref[...]).astype(BF16)
    z = _dot(n, w_ref[...])
    cos, sdn, sup = cos_ref[...], sdn_ref[...], sup_ref[...]
    qa_ref[...] = (z[:, :O_KA] * QSCALE).astype(BF16)
    ka_ref[...] = z[:, O_KA:O_VA]
    va_ref[...] = z[:, O_VA:O_QB]
    for j in range(N_PAIRS_B):
        sl = slice(O_QB + j * PAIR, O_QB + (j + 1) * PAIR)
        qb_ref[:, j * PAIR:(j + 1) * PAIR] = (_rope(z[:, sl], cos, sdn, sup) * QSCALE).astype(BF16)
    kb_ref[...] = _rope(z[:, O_KB:O_VB], cos, sdn, sup)
    vb_ref[...] = z[:, O_VB:]


def _in_proj(x, g, w_bf16, cos, sdn, sup, *, tm):
    n_tok = x.shape[0]
    row = lambda i: (i, 0)
    tab = _resident((tm, PAIR))
    widths = (WIDTH_A, WIDTH_A, WIDTH_A, WIDTH_B, KV_WIDTH_B, KV_WIDTH_B)
    dtypes = (BF16, F32, F32, BF16, F32, F32)
    return pl.pallas_call(
        _in_proj_kernel,
        grid=(n_tok // tm,),
        in_specs=[
            pl.BlockSpec((tm, D_MODEL), row),
            _resident((1, D_MODEL)),
            _resident((D_MODEL, IN_WIDTH)),
            tab, tab, tab,
        ],
        out_specs=[pl.BlockSpec((tm, w), row) for w in widths],
        out_shape=[jax.ShapeDtypeStruct((n_tok, w), d) for w, d in zip(widths, dtypes)],
        compiler_params=pltpu.CompilerParams(
            dimension_semantics=("arbitrary",), vmem_limit_bytes=V7X_VMEM_LIMIT_BYTES),
        name="in_proj",
    )(x, g, w_bf16, cos, sdn, sup)


def _attn_sample_kernel(sinks_ref, qa_ref, kac_ref, kan_ref, vac_ref, van_ref,
                        qb_ref, kbc_ref, kbn_ref, vbc_ref, vbn_ref,
                        rbv_ref, goa_ref, gob_ref, o_ref, bias_scr):
    t_s = qa_ref.shape[0]
    la = kac_ref.shape[1]

    @pl.when(pl.program_id(0) == 0)
    def _init():
        for h in range(N_HEADS_A):
            bias_scr[h] = _rolled_bias_rows(rbv_ref, h, t_s)

    ka_blocks = [kac_ref[0], kan_ref[...]]
    va_blocks = [vac_ref[0], van_ref[...]]
    cols = [(0, la), (la, la + t_s)]

    def kv_a(p, half):
        sl = slice(p * PAIR, (p + 1) * PAIR)
        return [(_keep_half(k[:, sl], half).astype(BF16), _keep_half(v[:, sl], half).astype(BF16))
                for k, v in zip(ka_blocks, va_blocks)]

    oa = _attend(qa_ref[...], N_PAIRS_A, kv_a,
                 lambda h, j: bias_scr[h, :, cols[j][0]:cols[j][1]])
    kb_var = [_kv_b_variants(kbc_ref[0]), _kv_b_variants(kbn_ref[...])]
    vb_var = [_kv_b_variants(vbc_ref[0]), _kv_b_variants(vbn_ref[...])]

    def kv_b(p, half):
        i = 2 * (2 * p // GROUP_B) + half
        return [(k[i], v[i]) for k, v in zip(kb_var, vb_var)]

    ob = _attend(qb_ref[...], N_PAIRS_B, kv_b, lambda h, j: None,
                 lambda h: sinks_ref[h] * LOG2E)
    o_ref[:, :WIDTH_A] = _rms(oa, goa_ref[...]).astype(BF16)
    o_ref[:, WIDTH_A:] = _rms(ob, gob_ref[...]).astype(BF16)


def _attn_sample(sinks, qa, ka_cache, ka, va_cache, va, qb, kb_cache, kb, vb_cache, vb,
                 rbv, goa, gob, *, t_s):
    n_tok = qa.shape[0]
    la, lb = ka_cache.shape[1], kb_cache.shape[1]
    row = lambda i: (i, 0)
    seq3 = lambda i: (i, 0, 0)
    return pl.pallas_call(
        _attn_sample_kernel,
        grid=(n_tok // t_s,),
        in_specs=[
            pl.BlockSpec(memory_space=pltpu.SMEM),
            pl.BlockSpec((t_s, WIDTH_A), row),
            pl.BlockSpec((1, la, WIDTH_A), seq3),
            pl.BlockSpec((t_s, WIDTH_A), row),
            pl.BlockSpec((1, la, WIDTH_A), seq3),
            pl.BlockSpec((t_s, WIDTH_A), row),
            pl.BlockSpec((t_s, WIDTH_B), row),
            pl.BlockSpec((1, lb, KV_WIDTH_B), seq3),
            pl.BlockSpec((t_s, KV_WIDTH_B), row),
            pl.BlockSpec((1, lb, KV_WIDTH_B), seq3),
            pl.BlockSpec((t_s, KV_WIDTH_B), row),
            _resident((N_HEADS_A, BIAS_PERIOD)),
            _resident((1, WIDTH_A)),
            _resident((1, WIDTH_B)),
        ],
        out_specs=pl.BlockSpec((t_s, MIX_WIDTH), row),
        out_shape=jax.ShapeDtypeStruct((n_tok, MIX_WIDTH), BF16),
        scratch_shapes=[pltpu.VMEM((N_HEADS_A, t_s, BIAS_PERIOD), F32)],
        compiler_params=pltpu.CompilerParams(
            dimension_semantics=("arbitrary",), vmem_limit_bytes=V7X_VMEM_LIMIT_BYTES),
        name="attn_sample",
    )(sinks, qa, ka_cache, ka, va_cache, va, qb, kb_cache, kb, vb_cache, vb, rbv, goa, gob)


def _tail_kernel(h_ref, o_ref, p_ref, wout_ref, gffn_ref, wgu_ref, wdown_ref, wgate_ref,
                 wproj_ref, gfin_ref, out_ref, *, final):
    h = h_ref[...] + _dot(o_ref[...], wout_ref[...])
    gu = _dot(_rms(h, gffn_ref[...]).astype(BF16), wgu_ref[...])
    act = (jax.nn.silu(gu[:, :D_FF]) * gu[:, D_FF:]).astype(BF16)
    h = h + _dot(act, wdown_ref[...])
    gate = jax.nn.sigmoid(_dot(h.astype(BF16), wgate_ref[...]))
    h = h + gate * _dot(p_ref[...].astype(BF16), wproj_ref[...])
    if final:
        h = _rms(h, gfin_ref[...])
    out_ref[...] = h


def _tail(h, o, p, wout, gffn, wgu, wdown, wgate, wproj, gfin, *, tm, final):
    n_tok = h.shape[0]
    row = lambda i: (i, 0)
    return pl.pallas_call(
        functools.partial(_tail_kernel, final=final),
        grid=(n_tok // tm,),
        in_specs=[
            pl.BlockSpec((tm, D_MODEL), row),
            pl.BlockSpec((tm, MIX_WIDTH), row),
            pl.BlockSpec((tm, D_PLE), row),
            _resident((MIX_WIDTH, D_MODEL)),
            _resident((1, D_MODEL)),
            _resident((D_MODEL, 2 * D_FF)),
            _resident((D_FF, D_MODEL)),
            _resident((D_MODEL, D_MODEL)),
            _resident((D_PLE, D_MODEL)),
            _resident((1, D_MODEL)),
        ],
        out_specs=pl.BlockSpec((tm, D_MODEL), row),
        out_shape=jax.ShapeDtypeStruct((n_tok, D_MODEL), F32),
        compiler_params=pltpu.CompilerParams(
            dimension_semantics=("arbitrary",), vmem_limit_bytes=V7X_VMEM_LIMIT_BYTES),
        name="layer_tail",
    )(h, o, p, wout, gffn, wgu, wdown, wgate, wproj, gfin)


def _rope_tables(pos):
    half = ROT_DIM // 2
    inv = ROPE_THETA ** (-jnp.arange(0, ROT_DIM, 2, dtype=F32) / ROT_DIM)
    ang = pos.astype(F32)[:, None] * inv[None, :]
    cos, sin = jnp.cos(ang), jnp.sin(ang)
    s = pos.shape[0]
    ones = jnp.ones((s, HEAD_DIM - ROT_DIM), F32)
    zeros = jnp.zeros((s, HEAD_DIM - ROT_DIM), F32)
    zh = jnp.zeros((s, half), F32)
    cos_h = jnp.concatenate([cos, cos, ones], axis=1)
    sdn_h = jnp.concatenate([-sin, zh, zeros], axis=1)
    sup_h = jnp.concatenate([zh, sin, zeros], axis=1)
    tile2 = lambda a: jnp.concatenate([a, a], axis=1)
    return tile2(cos_h), tile2(sdn_h), tile2(sup_h)


def _rel_bias_row(rel_bias):
    u = np.arange(BIAS_PERIOD)
    diff = np.where(u < BIAS_PERIOD // 2 + A_ROWS // 2, u, u - BIAS_PERIOD)
    idx = np.clip(A_ROWS - diff, -REL_CLIP, REL_CLIP) + REL_CLIP
    return rel_bias.astype(F32)[:, idx]


def kernel(x_prompt, x_sample, p_prompt, p_sample, cache_a_k, cache_a_v, cache_b_k, cache_b_v,
           g_mix_norm, w_in, rel_bias_a, sinks_b, g_out_a, g_out_b, w_out, g_ffn_norm,
           w_gate_up, w_down, w_ple_proj, w_ple_gate, g_final):
    b_p, s_p, _ = x_prompt.shape
    b_s, t_s, _ = x_sample.shape
    depth = w_in.shape[0]
    la_c, lb_c = cache_a_k.shape[2], cache_b_k.shape[2]
    keep_a_s = min(A_ROWS, la_c + t_s)
    keep_b_s = min(WINDOW_B, lb_c + t_s)
    tm_s = 512
    tm_tail = 256
    assert s_p % TQ == 0 and s_p >= A_ROWS and (b_s * t_s) % tm_s == 0 and tm_s % t_s == 0
    assert la_c == A_ROWS and lb_c == WINDOW_B and t_s == CHUNK

    cos_p, sdn_p, sup_p = _rope_tables(jnp.arange(s_p))
    cos_s, sdn_s, sup_s = (jnp.tile(a, (tm_s // t_s, 1))
                           for a in _rope_tables(PAST_LEN + jnp.arange(t_s)))
    row2 = lambda a: a.reshape(1, -1).astype(F32)
    g_fin = row2(g_final)

    hp = x_prompt
    hs = x_sample.reshape(b_s * t_s, D_MODEL)
    outs = [[] for _ in range(8)]
    for i in range(depth):
        w_in_i = w_in[i].astype(BF16)
        tail_w = (w_out[i].astype(BF16), row2(g_ffn_norm[i]), w_gate_up[i].astype(BF16),
                  w_down[i].astype(BF16), w_ple_gate[i].astype(BF16), w_ple_proj[i].astype(BF16),
                  g_fin)
        g_mix = row2(g_mix_norm[i])
        goa, gob = row2(g_out_a[i]), row2(g_out_b[i])
        sinks = sinks_b[i].astype(F32)
        rbv = _rel_bias_row(rel_bias_a[i])
        final = i == depth - 1

        o, cak, cav, cbk, cbv = _mix_prompt(sinks, hp, g_mix, w_in_i, cos_p, sdn_p, sup_p, rbv,
                                            goa, gob)
        hp = _tail(hp.reshape(b_p * s_p, D_MODEL), o.reshape(b_p * s_p, MIX_WIDTH),
                   p_prompt[i].reshape(b_p * s_p, D_PLE), *tail_w, tm=tm_tail, final=final)
        hp = hp.reshape(b_p, s_p, D_MODEL)
        outs[0].append(cak.reshape(b_p, A_ROWS, N_HEADS_A, HEAD_DIM))
        outs[1].append(cav.reshape(b_p, A_ROWS, N_HEADS_A, HEAD_DIM))
        outs[2].append(cbk.reshape(b_p, WINDOW_B, N_KV_B, HEAD_DIM))
        outs[3].append(cbv.reshape(b_p, WINDOW_B, N_KV_B, HEAD_DIM))

        qa, ka, va, qb, kb, vb = _in_proj(hs, g_mix, w_in_i, cos_s, sdn_s, sup_s, tm=tm_s)
        cak_i = cache_a_k[i].reshape(b_s, la_c, WIDTH_A)
        cav_i = cache_a_v[i].reshape(b_s, la_c, WIDTH_A)
        cbk_i = cache_b_k[i].reshape(b_s, lb_c, KV_WIDTH_B)
        cbv_i = cache_b_v[i].reshape(b_s, lb_c, KV_WIDTH_B)
        o = _attn_sample(sinks, qa, cak_i, ka, cav_i, va, qb, cbk_i, kb, cbv_i, vb,
                         rbv, goa, gob, t_s=t_s)
        hs = _tail(hs, o, p_sample[i].reshape(b_s * t_s, D_PLE), *tail_w, tm=tm_tail, final=final)
        roll = lambda cache, new, keep, shp: jnp.concatenate(
            [cache, new.reshape(b_s, t_s, -1)], axis=1)[:, -keep:].reshape(b_s, keep, *shp)
        outs[4].append(roll(cak_i, ka, keep_a_s, (N_HEADS_A, HEAD_DIM)))
        outs[5].append(roll(cav_i, va, keep_a_s, (N_HEADS_A, HEAD_DIM)))
        outs[6].append(roll(cbk_i, kb, keep_b_s, (N_KV_B, HEAD_DIM)))
        outs[7].append(roll(cbv_i, vb, keep_b_s, (N_KV_B, HEAD_DIM)))

    y_sample = hs.reshape(b_s, t_s, D_MODEL)
    return (hp, y_sample) + tuple(jnp.stack(o) for o in outs)
```

```python
import functools
import math

import numpy as np
import jax
import jax.numpy as jnp
from jax import lax
from jax.experimental import pallas as pl
from jax.experimental.pallas import tpu as pltpu

D_MODEL = 1024
CHUNK = 64
HEAD_DIM = 64
N_HEADS_A = 8
N_HEADS_B = 8
N_KV_B = 2
GROUP_B = N_HEADS_B // N_KV_B
WIDTH_A = N_HEADS_A * HEAD_DIM
WIDTH_B = N_HEADS_B * HEAD_DIM
KV_WIDTH_B = N_KV_B * HEAD_DIM
MIX_WIDTH = WIDTH_A + WIDTH_B
IN_WIDTH = 3 * WIDTH_A + WIDTH_B + 2 * KV_WIDTH_B
PREV_CHUNKS_A = 8
A_ROWS = PREV_CHUNKS_A * CHUNK
REL_CLIP = 256
WINDOW_B = 128
PREV_CHUNKS_B = WINDOW_B // CHUNK
ROT_DIM = HEAD_DIM // 4
ROPE_THETA = 500000.0
D_FF = 2816
D_PLE = 256
PAST_LEN = 2048
RMS_EPS = 1e-6
LOG2E = math.log2(math.e)
QSCALE = HEAD_DIM ** -0.5 * LOG2E
PAIR = 2 * HEAD_DIM
N_PAIRS_A = WIDTH_A // PAIR
N_PAIRS_B = WIDTH_B // PAIR
PAIRS_PER_KV = GROUP_B // 2
NEG_INF = float("-inf")
BIAS_PERIOD = 1024

V7X_VMEM_LIMIT_BYTES = 56 * 1024 * 1024

BF16 = jnp.bfloat16
F32 = jnp.float32

O_KA, O_VA, O_QB = WIDTH_A, 2 * WIDTH_A, 3 * WIDTH_A
O_KB = O_QB + WIDTH_B
O_VB = O_KB + KV_WIDTH_B


def _rms(x, g):
    return x * lax.rsqrt(jnp.mean(x * x, axis=-1, keepdims=True) + RMS_EPS) * g


def _dot(a, b):
    return jnp.dot(a, b, preferred_element_type=F32)


def _dot_nt(a, b):
    return lax.dot_general(a, b, (((1,), (1,)), ((), ())), preferred_element_type=F32)


def _resident(shape):
    nd = len(shape)
    return pl.BlockSpec(shape, lambda *_: (0,) * nd, pipeline_mode=pl.Buffered(1))


def _rope(x, cos, sdn, sup):
    half = ROT_DIM // 2
    return (x * cos + pltpu.roll(x, PAIR - half, axis=1) * sdn
            + pltpu.roll(x, half, axis=1) * sup)


def _keep_half(x, half):
    lane = lax.broadcasted_iota(jnp.int32, x.shape, 1)
    keep = (lane < HEAD_DIM) if half == 0 else (lane >= HEAD_DIM)
    return jnp.where(keep, x, jnp.zeros_like(x))


def _kv_b_variant(x, swapped, g, half):
    return _keep_half(x if g == half else swapped, half).astype(BF16)


def _softmax_pv(s_blocks, v, sink=None):
    m = functools.reduce(jnp.maximum, [jnp.max(s, axis=-1, keepdims=True) for s in s_blocks])
    if sink is not None:
        m = jnp.maximum(m, sink)
    es = [jnp.exp2(s - m) for s in s_blocks]
    den = functools.reduce(jnp.add, [jnp.sum(e, axis=-1, keepdims=True) for e in es])
    if sink is not None:
        den = den + jnp.exp2(sink - m)
    p = jnp.concatenate([e.astype(BF16) for e in es], axis=-1)
    return _dot(p, v) / den


def _rolled_bias_rows(rbv_ref, h, rows):
    x = jnp.broadcast_to(rbv_ref[h:h + 1, :] * LOG2E, (rows, BIAS_PERIOD))
    return pltpu.roll(x, 0, axis=1, stride=1, stride_axis=0)


def _band_valid(rows, col0, cols, prev_chunks):
    r = lax.broadcasted_iota(jnp.int32, (rows, cols), 0) // CHUNK
    c = (lax.broadcasted_iota(jnp.int32, (rows, cols), 1) + col0) // CHUNK
    return (c >= r) & (c <= r + prev_chunks)


TQ = 256
NBLK_A = A_ROWS // TQ + 1
WIN_A = NBLK_A * TQ
WIN_B = WINDOW_B + TQ
NEG_KIND = NBLK_A


def _mix_prompt_kernel(sinks_ref, x_ref, g_ref, w_ref, rope_ref, ct_ref, st_ref, rbv_ref,
                       goa_ref, gob_ref,
                       o_ref, cak_ref, cav_ref, cbk_ref, cbv_ref,
                       ka_scr, va_scr, kb_scr, vb_scr, bias_scr, maskb_scr, *, n_tiles):
    b, t = pl.program_id(0), pl.program_id(1)

    @pl.when((b == 0) & (t == 0))
    def _init():
        ka_scr[...] = jnp.zeros_like(ka_scr)
        va_scr[...] = jnp.zeros_like(va_scr)
        kb_scr[...] = jnp.zeros_like(kb_scr)
        vb_scr[...] = jnp.zeros_like(vb_scr)
        for h in range(N_HEADS_A):
            rows = _rolled_bias_rows(rbv_ref, h, TQ)
            for j in range(NBLK_A):
                valid = _band_valid(TQ, j * TQ, TQ, PREV_CHUNKS_A)
                bias_scr[j, h] = jnp.where(valid, rows[:, j * TQ:(j + 1) * TQ], NEG_INF)
            bias_scr[NEG_KIND, h] = jnp.full((TQ, TQ), NEG_INF, F32)
        band = jnp.where(_band_valid(TQ, 0, WIN_B, PREV_CHUNKS_B), jnp.zeros((TQ, WIN_B), F32), NEG_INF)
        col = lax.broadcasted_iota(jnp.int32, (TQ, WIN_B), 1)
        maskb_scr[0] = band
        maskb_scr[1] = jnp.where(col >= WINDOW_B, band, NEG_INF)

    n = _rms(x_ref[0], g_ref[...]).astype(BF16)
    z = _dot(n, w_ref[...])
    ct, st = ct_ref[pl.ds(t, 1), :], st_ref[pl.ds(t, 1), :]
    cos = ct * rope_ref[0] - st * rope_ref[1]
    sdn = st * rope_ref[2] + ct * rope_ref[3]
    sup = st * rope_ref[4] + ct * rope_ref[5]
    ka, va = z[:, O_KA:O_VA], z[:, O_VA:O_QB]
    kb = _rope(z[:, O_KB:O_VB], cos, sdn, sup)
    vb = z[:, O_VB:]
    qa = (z[:, :O_KA] * QSCALE).astype(BF16)
    qb = [(_rope(z[:, O_QB + j * PAIR:O_QB + (j + 1) * PAIR], cos, sdn, sup) * QSCALE).astype(BF16)
          for j in range(N_PAIRS_B)]

    @pl.when(t >= n_tiles - A_ROWS // TQ)
    def _():
        cak_ref[0] = ka
        cav_ref[0] = va

    @pl.when(t == n_tiles - 1)
    def _():
        cbk_ref[0] = kb[TQ - WINDOW_B:, :]
        cbv_ref[0] = vb[TQ - WINDOW_B:, :]

    slot = lax.rem(t, NBLK_A)
    row0 = pl.multiple_of(slot * TQ, TQ)
    for p in range(N_PAIRS_A):
        for half in range(2):
            sl = slice(p * PAIR, (p + 1) * PAIR)
            ka_scr[p, pl.ds(half * WIN_A + row0, TQ), :] = _keep_half(ka[:, sl], half).astype(BF16)
            va_scr[p, half, pl.ds(row0, TQ), :] = _keep_half(va[:, sl], half).astype(BF16)
    kinds = []
    for ps in range(NBLK_A):
        d = lax.rem(t - ps + NBLK_A, NBLK_A)
        kinds.append(jnp.where(t >= d, NBLK_A - 1 - d, NEG_KIND))
    oa = []
    for p in range(N_PAIRS_A):
        s = _dot_nt(qa[:, p * PAIR:(p + 1) * PAIR], ka_scr[p])
        o_pair = None
        for half in range(2):
            s_blocks = [s[:, half * WIN_A + ps * TQ:half * WIN_A + (ps + 1) * TQ]
                        + bias_scr[kinds[ps], 2 * p + half] for ps in range(NBLK_A)]
            o = _softmax_pv(s_blocks, va_scr[p, half])
            o_pair = o if o_pair is None else o_pair + o
        oa.append(o_pair)
    oa = jnp.concatenate(oa, axis=-1)

    kb_sw, vb_sw = pltpu.roll(kb, HEAD_DIM, axis=1), pltpu.roll(vb, HEAD_DIM, axis=1)
    for g in range(N_KV_B):
        for half in range(2):
            kb_scr[g, pl.ds(half * WIN_B + WINDOW_B, TQ), :] = _kv_b_variant(kb, kb_sw, g, half)
            vb_scr[g, half, pl.ds(WINDOW_B, TQ), :] = _kv_b_variant(vb, vb_sw, g, half)
    maskb = maskb_scr[jnp.where(t >= 1, 0, 1)]
    ob = [None] * N_PAIRS_B
    for g in range(N_KV_B):
        pairs = range(g * PAIRS_PER_KV, (g + 1) * PAIRS_PER_KV)
        s = _dot_nt(jnp.concatenate([qb[p] for p in pairs], axis=0), kb_scr[g])
        for i, p in enumerate(pairs):
            o_pair = None
            for half in range(2):
                sb = s[i * TQ:(i + 1) * TQ, half * WIN_B:(half + 1) * WIN_B] + maskb
                o = _softmax_pv([sb], vb_scr[g, half], sinks_ref[2 * p + half] * LOG2E)
                o_pair = o if o_pair is None else o_pair + o
            ob[p] = o_pair
    ob = jnp.concatenate(ob, axis=-1)
    for g in range(N_KV_B):
        for half in range(2):
            kb_scr[g, pl.ds(half * WIN_B, WINDOW_B), :] = kb_scr[g, pl.ds(half * WIN_B + TQ, WINDOW_B), :]
            vb_scr[g, half, pl.ds(0, WINDOW_B), :] = vb_scr[g, half, pl.ds(TQ, WINDOW_B), :]

    o_ref[0, :, :WIDTH_A] = _rms(oa, goa_ref[...]).astype(BF16)
    o_ref[0, :, WIDTH_A:] = _rms(ob, gob_ref[...]).astype(BF16)


def _mix_prompt(sinks, x, g, w_bf16, rope_rows, ct, st, rbv, goa, gob):
    b, s, _ = x.shape
    nt = s // TQ
    cur = lambda bi, t: (bi, t, 0)
    keep_a = lambda bi, t: (bi, jnp.maximum(t - (nt - A_ROWS // TQ), 0), 0)
    seq = lambda bi, t: (bi, 0, 0)
    return pl.pallas_call(
        functools.partial(_mix_prompt_kernel, n_tiles=nt),
        grid=(b, nt),
        in_specs=[
            pl.BlockSpec(memory_space=pltpu.SMEM),
            pl.BlockSpec((1, TQ, D_MODEL), cur),
            _resident((1, D_MODEL)),
            _resident((D_MODEL, IN_WIDTH)),
            _resident(rope_rows.shape),
            _resident(ct.shape),
            _resident(st.shape),
            _resident((N_HEADS_A, BIAS_PERIOD)),
            _resident((1, WIDTH_A)),
            _resident((1, WIDTH_B)),
        ],
        out_specs=[
            pl.BlockSpec((1, TQ, MIX_WIDTH), cur),
            pl.BlockSpec((1, TQ, WIDTH_A), keep_a),
            pl.BlockSpec((1, TQ, WIDTH_A), keep_a),
            pl.BlockSpec((1, WINDOW_B, KV_WIDTH_B), seq),
            pl.BlockSpec((1, WINDOW_B, KV_WIDTH_B), seq),
        ],
        out_shape=[
            jax.ShapeDtypeStruct((b, s, MIX_WIDTH), BF16),
            jax.ShapeDtypeStruct((b, A_ROWS, WIDTH_A), F32),
            jax.ShapeDtypeStruct((b, A_ROWS, WIDTH_A), F32),
            jax.ShapeDtypeStruct((b, WINDOW_B, KV_WIDTH_B), F32),
            jax.ShapeDtypeStruct((b, WINDOW_B, KV_WIDTH_B), F32),
        ],
        scratch_shapes=[
            pltpu.VMEM((N_PAIRS_A, 2 * WIN_A, PAIR), BF16),
            pltpu.VMEM((N_PAIRS_A, 2, WIN_A, PAIR), BF16),
            pltpu.VMEM((N_KV_B, 2 * WIN_B, PAIR), BF16),
            pltpu.VMEM((N_KV_B, 2, WIN_B, PAIR), BF16),
            pltpu.VMEM((NBLK_A + 1, N_HEADS_A, TQ, TQ), F32),
            pltpu.VMEM((2, TQ, WIN_B), F32),
        ],
        compiler_params=pltpu.CompilerParams(
            dimension_semantics=("arbitrary", "arbitrary"),
            vmem_limit_bytes=V7X_VMEM_LIMIT_BYTES),
        name="mix_prompt",
    )(sinks, x, g, w_bf16, rope_rows, ct, st, rbv, goa, gob)


def _in_proj_kernel(x_ref, g_ref, w_ref, rope_ref, qa_ref, ka_ref, va_ref, qb_ref, kb_ref, vb_ref):
    n = _rms(x_ref[...], g_ref[...]).astype(BF16)
    z = _dot(n, w_ref[...])
    cos, sdn, sup = rope_ref[0], rope_ref[1], rope_ref[2]
    qa_ref[...] = (z[:, :O_KA] * QSCALE).astype(BF16)
    ka_ref[...] = z[:, O_KA:O_VA]
    va_ref[...] = z[:, O_VA:O_QB]
    for j in range(N_PAIRS_B):
        sl = slice(O_QB + j * PAIR, O_QB + (j + 1) * PAIR)
        qb_ref[:, j * PAIR:(j + 1) * PAIR] = (_rope(z[:, sl], cos, sdn, sup) * QSCALE).astype(BF16)
    kb_ref[...] = _rope(z[:, O_KB:O_VB], cos, sdn, sup)
    vb_ref[...] = z[:, O_VB:]


def _in_proj(x, g, w_bf16, rope_tabs, *, tm):
    n_tok = x.shape[0]
    row = lambda i: (i, 0)
    widths = (WIDTH_A, WIDTH_A, WIDTH_A, WIDTH_B, KV_WIDTH_B, KV_WIDTH_B)
    dtypes = (BF16, F32, F32, BF16, F32, F32)
    return pl.pallas_call(
        _in_proj_kernel,
        grid=(n_tok // tm,),
        in_specs=[
            pl.BlockSpec((tm, D_MODEL), row),
            _resident((1, D_MODEL)),
            _resident((D_MODEL, IN_WIDTH)),
            _resident(rope_tabs.shape),
        ],
        out_specs=[pl.BlockSpec((tm, w), row) for w in widths],
        out_shape=[jax.ShapeDtypeStruct((n_tok, w), d) for w, d in zip(widths, dtypes)],
        compiler_params=pltpu.CompilerParams(
            dimension_semantics=("arbitrary",), vmem_limit_bytes=V7X_VMEM_LIMIT_BYTES),
        name="in_proj",
    )(x, g, w_bf16, rope_tabs)


def _attn_sample_kernel(sinks_ref, qa_ref, kac_ref, kan_ref, vac_ref, van_ref,
                        qb_ref, kbc_ref, kbn_ref, vbc_ref, vbn_ref,
                        rbv_ref, goa_ref, gob_ref, o_ref, bias_scr):
    t_s = qa_ref.shape[0]
    la = kac_ref.shape[1]

    @pl.when(pl.program_id(0) == 0)
    def _init():
        for h in range(N_HEADS_A):
            bias_scr[h] = _rolled_bias_rows(rbv_ref, h, t_s)

    oa = []
    for p in range(N_PAIRS_A):
        sl = slice(p * PAIR, (p + 1) * PAIR)
        qp = qa_ref[:, sl]
        kc, kn, vc, vn = kac_ref[0, :, sl], kan_ref[:, sl], vac_ref[0, :, sl], van_ref[:, sl]
        o_pair = None
        for half in range(2):
            h = 2 * p + half
            s_blocks = [_dot_nt(qp, _keep_half(kc, half).astype(BF16)) + bias_scr[h, :, :la],
                        _dot_nt(qp, _keep_half(kn, half).astype(BF16)) + bias_scr[h, :, la:la + t_s]]
            v = jnp.concatenate([_keep_half(vc, half), _keep_half(vn, half)], axis=0).astype(BF16)
            o = _softmax_pv(s_blocks, v)
            o_pair = o if o_pair is None else o_pair + o
        oa.append(o_pair)
    oa = jnp.concatenate(oa, axis=-1)

    kb = jnp.concatenate([kbc_ref[0], kbn_ref[...]], axis=0)
    vb = jnp.concatenate([vbc_ref[0], vbn_ref[...]], axis=0)
    kb_sw, vb_sw = pltpu.roll(kb, HEAD_DIM, axis=1), pltpu.roll(vb, HEAD_DIM, axis=1)
    ob = []
    for p in range(N_PAIRS_B):
        g = p // PAIRS_PER_KV
        qp = qb_ref[:, p * PAIR:(p + 1) * PAIR]
        o_pair = None
        for half in range(2):
            s = _dot_nt(qp, _kv_b_variant(kb, kb_sw, g, half))
            o = _softmax_pv([s], _kv_b_variant(vb, vb_sw, g, half), sinks_ref[2 * p + half] * LOG2E)
            o_pair = o if o_pair is None else o_pair + o
        ob.append(o_pair)
    ob = jnp.concatenate(ob, axis=-1)
    o_ref[:, :WIDTH_A] = _rms(oa, goa_ref[...]).astype(BF16)
    o_ref[:, WIDTH_A:] = _rms(ob, gob_ref[...]).astype(BF16)


def _attn_sample(sinks, qa, ka_cache, ka, va_cache, va, qb, kb_cache, kb, vb_cache, vb,
                 rbv, goa, gob, *, t_s):
    n_tok = qa.shape[0]
    la, lb = ka_cache.shape[1], kb_cache.shape[1]
    row = lambda i: (i, 0)
    seq3 = lambda i: (i, 0, 0)
    return pl.pallas_call(
        _attn_sample_kernel,
        grid=(n_tok // t_s,),
        in_specs=[
            pl.BlockSpec(memory_space=pltpu.SMEM),
            pl.BlockSpec((t_s, WIDTH_A), row),
            pl.BlockSpec((1, la, WIDTH_A), seq3),
            pl.BlockSpec((t_s, WIDTH_A), row),
            pl.BlockSpec((1, la, WIDTH_A), seq3),
            pl.BlockSpec((t_s, WIDTH_A), row),
            pl.BlockSpec((t_s, WIDTH_B), row),
            pl.BlockSpec((1, lb, KV_WIDTH_B), seq3),
            pl.BlockSpec((t_s, KV_WIDTH_B), row),
            pl.BlockSpec((1, lb, KV_WIDTH_B), seq3),
            pl.BlockSpec((t_s, KV_WIDTH_B), row),
            _resident((N_HEADS_A, BIAS_PERIOD)),
            _resident((1, WIDTH_A)),
            _resident((1, WIDTH_B)),
        ],
        out_specs=pl.BlockSpec((t_s, MIX_WIDTH), row),
        out_shape=jax.ShapeDtypeStruct((n_tok, MIX_WIDTH), BF16),
        scratch_shapes=[pltpu.VMEM((N_HEADS_A, t_s, BIAS_PERIOD), F32)],
        compiler_params=pltpu.CompilerParams(
            dimension_semantics=("arbitrary",), vmem_limit_bytes=V7X_VMEM_LIMIT_BYTES),
        name="attn_sample",
    )(sinks, qa, ka_cache, ka, va_cache, va, qb, kb_cache, kb, vb_cache, vb, rbv, goa, gob)


def _tail_kernel(h_ref, o_ref, p_ref, wout_ref, gffn_ref, wgu_ref, wdown_ref, wgate_ref,
                 wproj_ref, gfin_ref, out_ref, *, final):
    h = h_ref[...] + _dot(o_ref[...], wout_ref[...])
    gu = _dot(_rms(h, gffn_ref[...]).astype(BF16), wgu_ref[...])
    act = (jax.nn.silu(gu[:, :D_FF]) * gu[:, D_FF:]).astype(BF16)
    h = h + _dot(act, wdown_ref[...])
    gate = jax.nn.sigmoid(_dot(h.astype(BF16), wgate_ref[...]))
    h = h + gate * _dot(p_ref[...].astype(BF16), wproj_ref[...])
    if final:
        h = _rms(h, gfin_ref[...])
    out_ref[...] = h


def _tail(h, o, p, wout, gffn, wgu, wdown, wgate, wproj, gfin, *, tm, final):
    n_tok = h.shape[0]
    row = lambda i: (i, 0)
    return pl.pallas_call(
        functools.partial(_tail_kernel, final=final),
        grid=(n_tok // tm,),
        in_specs=[
            pl.BlockSpec((tm, D_MODEL), row),
            pl.BlockSpec((tm, MIX_WIDTH), row),
            pl.BlockSpec((tm, D_PLE), row),
            _resident((MIX_WIDTH, D_MODEL)),
            _resident((1, D_MODEL)),
            _resident((D_MODEL, 2 * D_FF)),
            _resident((D_FF, D_MODEL)),
            _resident((D_MODEL, D_MODEL)),
            _resident((D_PLE, D_MODEL)),
            _resident((1, D_MODEL)),
        ],
        out_specs=pl.BlockSpec((tm, D_MODEL), row),
        out_shape=jax.ShapeDtypeStruct((n_tok, D_MODEL), F32),
        compiler_params=pltpu.CompilerParams(
            dimension_semantics=("arbitrary",), vmem_limit_bytes=V7X_VMEM_LIMIT_BYTES),
        name="layer_tail",
    )(h, o, p, wout, gffn, wgu, wdown, wgate, wproj, gfin)


def _rope_lane_tables(pos):
    half = ROT_DIM // 2
    d = np.arange(PAIR) % HEAD_DIM
    inv = ROPE_THETA ** (-(2.0 * (d % half)) / ROT_DIM)
    ang = np.where(d < ROT_DIM, np.asarray(pos, np.float64)[:, None] * inv[None, :], 0.0)
    lower = (d < half).astype(np.float64)[None, :]
    upper = ((d >= half) & (d < ROT_DIM)).astype(np.float64)[None, :]
    return np.cos(ang), np.sin(ang), lower, upper


def _rope_tables_direct(pos):
    cos, sin, lower, upper = _rope_lane_tables(pos)
    return jnp.asarray(np.stack([cos, -sin * lower, sin * upper]), F32)


def _rope_tables_split(n_tiles, tile):
    cr, sr, lower, upper = _rope_lane_tables(np.arange(tile))
    ct, st, _, _ = _rope_lane_tables(np.arange(n_tiles) * tile)
    rows = np.stack([cr, sr, -cr * lower, -sr * lower, cr * upper, sr * upper])
    return jnp.asarray(rows, F32), jnp.asarray(ct, F32), jnp.asarray(st, F32)


def _rel_bias_row(rel_bias):
    u = np.arange(BIAS_PERIOD)
    diff = np.where(u < BIAS_PERIOD // 2 + A_ROWS // 2, u, u - BIAS_PERIOD)
    idx = np.clip(A_ROWS - diff, -REL_CLIP, REL_CLIP) + REL_CLIP
    return rel_bias.astype(F32)[:, idx]


def kernel(x_prompt, x_sample, p_prompt, p_sample, cache_a_k, cache_a_v, cache_b_k, cache_b_v,
           g_mix_norm, w_in, rel_bias_a, sinks_b, g_out_a, g_out_b, w_out, g_ffn_norm,
           w_gate_up, w_down, w_ple_proj, w_ple_gate, g_final):
    b_p, s_p, _ = x_prompt.shape
    b_s, t_s, _ = x_sample.shape
    depth = w_in.shape[0]
    la_c, lb_c = cache_a_k.shape[2], cache_b_k.shape[2]
    keep_a_s = min(A_ROWS, la_c + t_s)
    keep_b_s = min(WINDOW_B, lb_c + t_s)
    tm_s = 512
    tm_tail = 256
    assert s_p % TQ == 0 and s_p >= A_ROWS and (b_s * t_s) % tm_s == 0 and tm_s % t_s == 0
    assert la_c == A_ROWS and lb_c == WINDOW_B and t_s == CHUNK

    rope_rows, rope_ct, rope_st = _rope_tables_split(s_p // TQ, TQ)
    rope_s = _rope_tables_direct(PAST_LEN + np.arange(tm_s) % t_s)
    row2 = lambda a: a.reshape(1, -1).astype(F32)
    g_fin = row2(g_final)

    hp = x_prompt
    hs = x_sample.reshape(b_s * t_s, D_MODEL)
    outs = [[] for _ in range(8)]
    for i in range(depth):
        w_in_i = w_in[i].astype(BF16)
        tail_w = (w_out[i].astype(BF16), row2(g_ffn_norm[i]), w_gate_up[i].astype(BF16),
                  w_down[i].astype(BF16), w_ple_gate[i].astype(BF16), w_ple_proj[i].astype(BF16),
                  g_fin)
        g_mix = row2(g_mix_norm[i])
        goa, gob = row2(g_out_a[i]), row2(g_out_b[i])
        sinks = sinks_b[i].astype(F32)
        rbv = _rel_bias_row(rel_bias_a[i])
        final = i == depth - 1

        o, cak, cav, cbk, cbv = _mix_prompt(sinks, hp, g_mix, w_in_i, rope_rows, rope_ct, rope_st,
                                            rbv, goa, gob)
        hp = _tail(hp.reshape(b_p * s_p, D_MODEL), o.reshape(b_p * s_p, MIX_WIDTH),
                   p_prompt[i].reshape(b_p * s_p, D_PLE), *tail_w, tm=tm_tail, final=final)
        hp = hp.reshape(b_p, s_p, D_MODEL)
        outs[0].append(cak.reshape(b_p, A_ROWS, N_HEADS_A, HEAD_DIM))
        outs[1].append(cav.reshape(b_p, A_ROWS, N_HEADS_A, HEAD_DIM))
        outs[2].append(cbk.reshape(b_p, WINDOW_B, N_KV_B, HEAD_DIM))
        outs[3].append(cbv.reshape(b_p, WINDOW_B, N_KV_B, HEAD_DIM))

        qa, ka, va, qb, kb, vb = _in_proj(hs, g_mix, w_in_i, rope_s, tm=tm_s)
        cak_i = cache_a_k[i].reshape(b_s, la_c, WIDTH_A)
        cav_i = cache_a_v[i].reshape(b_s, la_c, WIDTH_A)
        cbk_i = cache_b_k[i].reshape(b_s, lb_c, KV_WIDTH_B)
        cbv_i = cache_b_v[i].reshape(b_s, lb_c, KV_WIDTH_B)
        o = _attn_sample(sinks, qa, cak_i, ka, cav_i, va, qb, cbk_i, kb, cbv_i, vb,
                         rbv, goa, gob, t_s=t_s)
        hs = _tail(hs, o, p_sample[i].reshape(b_s * t_s, D_PLE), *tail_w, tm=tm_tail, final=final)
        roll = lambda cache, new, keep, shp: jnp.concatenate(
            [cache, new.reshape(b_s, t_s, -1)], axis=1)[:, -keep:].reshape(b_s, keep, *shp)
        outs[4].append(roll(cak_i, ka, keep_a_s, (N_HEADS_A, HEAD_DIM)))
        outs[5].append(roll(cav_i, va, keep_a_s, (N_HEADS_A, HEAD_DIM)))
        outs[6].append(roll(cbk_i, kb, keep_b_s, (N_KV_B, HEAD_DIM)))
        outs[7].append(roll(cbv_i, vb, keep_b_s, (N_KV_B, HEAD_DIM)))

    y_sample = hs.reshape(b_s, t_s, D_MODEL)
    return (hp, y_sample) + tuple(jnp.stack(o) for o in outs)
```

```python
import functools
import math

import numpy as np
import jax
import jax.numpy as jnp
from jax import lax
from jax.experimental import pallas as pl
from jax.experimental.pallas import tpu as pltpu

D_MODEL = 1024
CHUNK = 64
HEAD_DIM = 64
N_HEADS_A = 8
N_HEADS_B = 8
N_KV_B = 2
GROUP_B = N_HEADS_B // N_KV_B
WIDTH_A = N_HEADS_A * HEAD_DIM
WIDTH_B = N_HEADS_B * HEAD_DIM
KV_WIDTH_B = N_KV_B * HEAD_DIM
MIX_WIDTH = WIDTH_A + WIDTH_B
IN_WIDTH = 3 * WIDTH_A + WIDTH_B + 2 * KV_WIDTH_B
PREV_CHUNKS_A = 8
A_ROWS = PREV_CHUNKS_A * CHUNK
REL_CLIP = 256
WINDOW_B = 128
PREV_CHUNKS_B = WINDOW_B // CHUNK
ROT_DIM = HEAD_DIM // 4
ROPE_THETA = 500000.0
D_FF = 2816
D_PLE = 256
PAST_LEN = 2048
RMS_EPS = 1e-6
LOG2E = math.log2(math.e)
QSCALE = HEAD_DIM ** -0.5 * LOG2E
PAIR = 2 * HEAD_DIM
N_PAIRS_A = WIDTH_A // PAIR
N_PAIRS_B = WIDTH_B // PAIR
PAIRS_PER_KV = GROUP_B // 2
NEG_INF = float("-inf")
BIAS_PERIOD = 1024

V7X_VMEM_LIMIT_BYTES = 56 * 1024 * 1024

BF16 = jnp.bfloat16
F32 = jnp.float32

O_KA, O_VA, O_QB = WIDTH_A, 2 * WIDTH_A, 3 * WIDTH_A
O_KB = O_QB + WIDTH_B
O_VB = O_KB + KV_WIDTH_B


def _rms(x, g):
    return x * lax.rsqrt(jnp.mean(x * x, axis=-1, keepdims=True) + RMS_EPS) * g


def _dot(a, b):
    return jnp.dot(a, b, preferred_element_type=F32)


def _dot_nt(a, b):
    return lax.dot_general(a, b, (((1,), (1,)), ((), ())), preferred_element_type=F32)


def _resident(shape):
    nd = len(shape)
    return pl.BlockSpec(shape, lambda *_: (0,) * nd, pipeline_mode=pl.Buffered(1))


def _rope(x, cos, sdn, sup):
    half = ROT_DIM // 2
    return (x * cos + pltpu.roll(x, PAIR - half, axis=1) * sdn
            + pltpu.roll(x, half, axis=1) * sup)


def _low_lanes(shape):
    return lax.broadcasted_iota(jnp.int32, shape, 1) < HEAD_DIM


def _keep_half(x, half, fill):
    keep = _low_lanes(x.shape) if half == 0 else ~_low_lanes(x.shape)
    return jnp.where(keep, x, jnp.full_like(x, fill))


def _kv_b_variant(x, swapped, g, half, fill):
    return _keep_half(x if g == half else swapped, half, fill).astype(BF16)


def _exp_scores(s_blocks, sink=None):
    m = functools.reduce(jnp.maximum, [jnp.max(s, axis=-1, keepdims=True) for s in s_blocks])
    if sink is not None:
        m = jnp.maximum(m, sink)
    p = jnp.concatenate([jnp.exp2(s - m).astype(BF16) for s in s_blocks], axis=-1)
    return p, (None if sink is None else jnp.exp2(sink - m))


def _pv_pair(ps, vs, extras=(None, None)):
    outs = []
    for p, v, extra in zip(ps, vs, extras):
        x = _dot(p, v)
        den = pltpu.roll(x, HEAD_DIM, axis=1)
        outs.append(x / (den if extra is None else den + extra))
    return jnp.where(_low_lanes(outs[0].shape), outs[0], outs[1])


def _rolled_bias_rows(rbv_ref, h, rows):
    x = jnp.broadcast_to(rbv_ref[h:h + 1, :] * LOG2E, (rows, BIAS_PERIOD))
    return pltpu.roll(x, 0, axis=1, stride=1, stride_axis=0)


def _band_valid(rows, col0, cols, prev_chunks):
    r = lax.broadcasted_iota(jnp.int32, (rows, cols), 0) // CHUNK
    c = (lax.broadcasted_iota(jnp.int32, (rows, cols), 1) + col0) // CHUNK
    return (c >= r) & (c <= r + prev_chunks)


TQ = 256
NBLK_A = A_ROWS // TQ + 1
WIN_A = NBLK_A * TQ
WIN_B = WINDOW_B + TQ
NEG_KIND = NBLK_A


def _mix_prompt_kernel(sinks_ref, x_ref, g_ref, w_ref, rope_ref, ct_ref, st_ref, rbv_ref,
                       goa_ref, gob_ref,
                       o_ref, cak_ref, cav_ref, cbk_ref, cbv_ref,
                       ka_scr, va_scr, kb_scr, vb_scr, bias_scr, maskb_scr, *, n_tiles):
    b, t = pl.program_id(0), pl.program_id(1)

    @pl.when((b == 0) & (t == 0))
    def _init():
        ka_scr[...] = jnp.zeros_like(ka_scr)
        va_scr[...] = jnp.zeros_like(va_scr)
        kb_scr[...] = jnp.zeros_like(kb_scr)
        vb_scr[...] = jnp.zeros_like(vb_scr)
        for h in range(N_HEADS_A):
            rows = _rolled_bias_rows(rbv_ref, h, TQ)
            for j in range(NBLK_A):
                valid = _band_valid(TQ, j * TQ, TQ, PREV_CHUNKS_A)
                bias_scr[j, h] = jnp.where(valid, rows[:, j * TQ:(j + 1) * TQ], NEG_INF)
            bias_scr[NEG_KIND, h] = jnp.full((TQ, TQ), NEG_INF, F32)
        band = jnp.where(_band_valid(TQ, 0, WIN_B, PREV_CHUNKS_B), jnp.zeros((TQ, WIN_B), F32), NEG_INF)
        col = lax.broadcasted_iota(jnp.int32, (TQ, WIN_B), 1)
        maskb_scr[0] = band
        maskb_scr[1] = jnp.where(col >= WINDOW_B, band, NEG_INF)

    n = _rms(x_ref[0], g_ref[...]).astype(BF16)
    z = _dot(n, w_ref[...])
    ct, st = ct_ref[pl.ds(t, 1), :], st_ref[pl.ds(t, 1), :]
    cos = ct * rope_ref[0] - st * rope_ref[1]
    sdn = st * rope_ref[2] + ct * rope_ref[3]
    sup = st * rope_ref[4] + ct * rope_ref[5]
    ka, va = z[:, O_KA:O_VA], z[:, O_VA:O_QB]
    kb = _rope(z[:, O_KB:O_VB], cos, sdn, sup)
    vb = z[:, O_VB:]
    qa = (z[:, :O_KA] * QSCALE).astype(BF16)
    qb = [(_rope(z[:, O_QB + j * PAIR:O_QB + (j + 1) * PAIR], cos, sdn, sup) * QSCALE).astype(BF16)
          for j in range(N_PAIRS_B)]

    @pl.when(t >= n_tiles - A_ROWS // TQ)
    def _():
        cak_ref[0] = ka
        cav_ref[0] = va

    @pl.when(t == n_tiles - 1)
    def _():
        cbk_ref[0] = kb[TQ - WINDOW_B:, :]
        cbv_ref[0] = vb[TQ - WINDOW_B:, :]

    slot = lax.rem(t, NBLK_A)
    row0 = pl.multiple_of(slot * TQ, TQ)
    for p in range(N_PAIRS_A):
        for half in range(2):
            sl = slice(p * PAIR, (p + 1) * PAIR)
            ka_scr[p, pl.ds(half * WIN_A + row0, TQ), :] = _keep_half(ka[:, sl], half, 0.0).astype(BF16)
            va_scr[p, half, pl.ds(row0, TQ), :] = _keep_half(va[:, sl], half, 1.0).astype(BF16)
    kinds = []
    for ps in range(NBLK_A):
        d = lax.rem(t - ps + NBLK_A, NBLK_A)
        kinds.append(jnp.where(t >= d, NBLK_A - 1 - d, NEG_KIND))
    oa = []
    for p in range(N_PAIRS_A):
        s = _dot_nt(qa[:, p * PAIR:(p + 1) * PAIR], ka_scr[p])
        probs = []
        for half in range(2):
            s_blocks = [s[:, half * WIN_A + ps * TQ:half * WIN_A + (ps + 1) * TQ]
                        + bias_scr[kinds[ps], 2 * p + half] for ps in range(NBLK_A)]
            probs.append(_exp_scores(s_blocks)[0])
        oa.append(_pv_pair(probs, [va_scr[p, 0], va_scr[p, 1]]))
    oa = jnp.concatenate(oa, axis=-1)

    kb_sw, vb_sw = pltpu.roll(kb, HEAD_DIM, axis=1), pltpu.roll(vb, HEAD_DIM, axis=1)
    for g in range(N_KV_B):
        for half in range(2):
            kb_scr[g, pl.ds(half * WIN_B + WINDOW_B, TQ), :] = _kv_b_variant(kb, kb_sw, g, half, 0.0)
            vb_scr[g, half, pl.ds(WINDOW_B, TQ), :] = _kv_b_variant(vb, vb_sw, g, half, 1.0)
    maskb = maskb_scr[jnp.where(t >= 1, 0, 1)]
    ob = [None] * N_PAIRS_B
    for g in range(N_KV_B):
        pairs = range(g * PAIRS_PER_KV, (g + 1) * PAIRS_PER_KV)
        s = _dot_nt(jnp.concatenate([qb[p] for p in pairs], axis=0), kb_scr[g])
        for i, p in enumerate(pairs):
            probs, extras = [], []
            for half in range(2):
                sb = s[i * TQ:(i + 1) * TQ, half * WIN_B:(half + 1) * WIN_B] + maskb
                pr, ex = _exp_scores([sb], sinks_ref[2 * p + half] * LOG2E)
                probs.append(pr)
                extras.append(ex)
            ob[p] = _pv_pair(probs, [vb_scr[g, 0], vb_scr[g, 1]], extras)
    ob = jnp.concatenate(ob, axis=-1)
    for g in range(N_KV_B):
        for half in range(2):
            kb_scr[g, pl.ds(half * WIN_B, WINDOW_B), :] = kb_scr[g, pl.ds(half * WIN_B + TQ, WINDOW_B), :]
            vb_scr[g, half, pl.ds(0, WINDOW_B), :] = vb_scr[g, half, pl.ds(TQ, WINDOW_B), :]

    o_ref[0, :, :WIDTH_A] = _rms(oa, goa_ref[...]).astype(BF16)
    o_ref[0, :, WIDTH_A:] = _rms(ob, gob_ref[...]).astype(BF16)


def _mix_prompt(sinks, x, g, w_bf16, rope_rows, ct, st, rbv, goa, gob):
    b, s, _ = x.shape
    nt = s // TQ
    cur = lambda bi, t: (bi, t, 0)
    keep_a = lambda bi, t: (bi, jnp.maximum(t - (nt - A_ROWS // TQ), 0), 0)
    seq = lambda bi, t: (bi, 0, 0)
    return pl.pallas_call(
        functools.partial(_mix_prompt_kernel, n_tiles=nt),
        grid=(b, nt),
        in_specs=[
            pl.BlockSpec(memory_space=pltpu.SMEM),
            pl.BlockSpec((1, TQ, D_MODEL), cur),
            _resident((1, D_MODEL)),
            _resident((D_MODEL, IN_WIDTH)),
            _resident(rope_rows.shape),
            _resident(ct.shape),
            _resident(st.shape),
            _resident((N_HEADS_A, BIAS_PERIOD)),
            _resident((1, WIDTH_A)),
            _resident((1, WIDTH_B)),
        ],
        out_specs=[
            pl.BlockSpec((1, TQ, MIX_WIDTH), cur),
            pl.BlockSpec((1, TQ, WIDTH_A), keep_a),
            pl.BlockSpec((1, TQ, WIDTH_A), keep_a),
            pl.BlockSpec((1, WINDOW_B, KV_WIDTH_B), seq),
            pl.BlockSpec((1, WINDOW_B, KV_WIDTH_B), seq),
        ],
        out_shape=[
            jax.ShapeDtypeStruct((b, s, MIX_WIDTH), BF16),
            jax.ShapeDtypeStruct((b, A_ROWS, WIDTH_A), F32),
            jax.ShapeDtypeStruct((b, A_ROWS, WIDTH_A), F32),
            jax.ShapeDtypeStruct((b, WINDOW_B, KV_WIDTH_B), F32),
            jax.ShapeDtypeStruct((b, WINDOW_B, KV_WIDTH_B), F32),
        ],
        scratch_shapes=[
            pltpu.VMEM((N_PAIRS_A, 2 * WIN_A, PAIR), BF16),
            pltpu.VMEM((N_PAIRS_A, 2, WIN_A, PAIR), BF16),
            pltpu.VMEM((N_KV_B, 2 * WIN_B, PAIR), BF16),
            pltpu.VMEM((N_KV_B, 2, WIN_B, PAIR), BF16),
            pltpu.VMEM((NBLK_A + 1, N_HEADS_A, TQ, TQ), F32),
            pltpu.VMEM((2, TQ, WIN_B), F32),
        ],
        compiler_params=pltpu.CompilerParams(
            dimension_semantics=("arbitrary", "arbitrary"),
            vmem_limit_bytes=V7X_VMEM_LIMIT_BYTES),
        name="mix_prompt",
    )(sinks, x, g, w_bf16, rope_rows, ct, st, rbv, goa, gob)


def _in_proj_kernel(x_ref, g_ref, w_ref, rope_ref, qa_ref, ka_ref, va_ref, qb_ref, kb_ref, vb_ref):
    n = _rms(x_ref[...], g_ref[...]).astype(BF16)
    z = _dot(n, w_ref[...])
    cos, sdn, sup = rope_ref[0], rope_ref[1], rope_ref[2]
    qa_ref[...] = (z[:, :O_KA] * QSCALE).astype(BF16)
    ka_ref[...] = z[:, O_KA:O_VA]
    va_ref[...] = z[:, O_VA:O_QB]
    for j in range(N_PAIRS_B):
        sl = slice(O_QB + j * PAIR, O_QB + (j + 1) * PAIR)
        qb_ref[:, j * PAIR:(j + 1) * PAIR] = (_rope(z[:, sl], cos, sdn, sup) * QSCALE).astype(BF16)
    kb_ref[...] = _rope(z[:, O_KB:O_VB], cos, sdn, sup)
    vb_ref[...] = z[:, O_VB:]


def _in_proj(x, g, w_bf16, rope_tabs, *, tm):
    n_tok = x.shape[0]
    row = lambda i: (i, 0)
    widths = (WIDTH_A, WIDTH_A, WIDTH_A, WIDTH_B, KV_WIDTH_B, KV_WIDTH_B)
    dtypes = (BF16, F32, F32, BF16, F32, F32)
    return pl.pallas_call(
        _in_proj_kernel,
        grid=(n_tok // tm,),
        in_specs=[
            pl.BlockSpec((tm, D_MODEL), row),
            _resident((1, D_MODEL)),
            _resident((D_MODEL, IN_WIDTH)),
            _resident(rope_tabs.shape),
        ],
        out_specs=[pl.BlockSpec((tm, w), row) for w in widths],
        out_shape=[jax.ShapeDtypeStruct((n_tok, w), d) for w, d in zip(widths, dtypes)],
        compiler_params=pltpu.CompilerParams(
            dimension_semantics=("arbitrary",), vmem_limit_bytes=V7X_VMEM_LIMIT_BYTES),
        name="in_proj",
    )(x, g, w_bf16, rope_tabs)


def _attn_sample_kernel(sinks_ref, qa_ref, kac_ref, kan_ref, vac_ref, van_ref,
                        qb_ref, kbc_ref, kbn_ref, vbc_ref, vbn_ref,
                        rbv_ref, goa_ref, gob_ref, o_ref, bias_scr):
    t_s = qa_ref.shape[0]
    la = kac_ref.shape[1]

    @pl.when(pl.program_id(0) == 0)
    def _init():
        for h in range(N_HEADS_A):
            bias_scr[h] = _rolled_bias_rows(rbv_ref, h, t_s)

    oa = []
    for p in range(N_PAIRS_A):
        sl = slice(p * PAIR, (p + 1) * PAIR)
        qp = qa_ref[:, sl]
        kc, kn, vc, vn = kac_ref[0, :, sl], kan_ref[:, sl], vac_ref[0, :, sl], van_ref[:, sl]
        probs, vals = [], []
        for half in range(2):
            h = 2 * p + half
            s_blocks = [_dot_nt(qp, _keep_half(kc, half, 0.0).astype(BF16)) + bias_scr[h, :, :la],
                        _dot_nt(qp, _keep_half(kn, half, 0.0).astype(BF16)) + bias_scr[h, :, la:la + t_s]]
            probs.append(_exp_scores(s_blocks)[0])
            vals.append(jnp.concatenate([_keep_half(vc, half, 1.0), _keep_half(vn, half, 1.0)],
                                        axis=0).astype(BF16))
        oa.append(_pv_pair(probs, vals))
    oa = jnp.concatenate(oa, axis=-1)

    kb = jnp.concatenate([kbc_ref[0], kbn_ref[...]], axis=0)
    vb = jnp.concatenate([vbc_ref[0], vbn_ref[...]], axis=0)
    kb_sw, vb_sw = pltpu.roll(kb, HEAD_DIM, axis=1), pltpu.roll(vb, HEAD_DIM, axis=1)
    ob = []
    for p in range(N_PAIRS_B):
        g = p // PAIRS_PER_KV
        qp = qb_ref[:, p * PAIR:(p + 1) * PAIR]
        probs, extras, vals = [], [], []
        for half in range(2):
            s = _dot_nt(qp, _kv_b_variant(kb, kb_sw, g, half, 0.0))
            pr, ex = _exp_scores([s], sinks_ref[2 * p + half] * LOG2E)
            probs.append(pr)
            extras.append(ex)
            vals.append(_kv_b_variant(vb, vb_sw, g, half, 1.0))
        ob.append(_pv_pair(probs, vals, extras))
    ob = jnp.concatenate(ob, axis=-1)
    o_ref[:, :WIDTH_A] = _rms(oa, goa_ref[...]).astype(BF16)
    o_ref[:, WIDTH_A:] = _rms(ob, gob_ref[...]).astype(BF16)


def _attn_sample(sinks, qa, ka_cache, ka, va_cache, va, qb, kb_cache, kb, vb_cache, vb,
                 rbv, goa, gob, *, t_s):
    n_tok = qa.shape[0]
    la, lb = ka_cache.shape[1], kb_cache.shape[1]
    row = lambda i: (i, 0)
    seq3 = lambda i: (i, 0, 0)
    return pl.pallas_call(
        _attn_sample_kernel,
        grid=(n_tok // t_s,),
        in_specs=[
            pl.BlockSpec(memory_space=pltpu.SMEM),
            pl.BlockSpec((t_s, WIDTH_A), row),
            pl.BlockSpec((1, la, WIDTH_A), seq3),
            pl.BlockSpec((t_s, WIDTH_A), row),
            pl.BlockSpec((1, la, WIDTH_A), seq3),
            pl.BlockSpec((t_s, WIDTH_A), row),
            pl.BlockSpec((t_s, WIDTH_B), row),
            pl.BlockSpec((1, lb, KV_WIDTH_B), seq3),
            pl.BlockSpec((t_s, KV_WIDTH_B), row),
            pl.BlockSpec((1, lb, KV_WIDTH_B), seq3),
            pl.BlockSpec((t_s, KV_WIDTH_B), row),
            _resident((N_HEADS_A, BIAS_PERIOD)),
            _resident((1, WIDTH_A)),
            _resident((1, WIDTH_B)),
        ],
        out_specs=pl.BlockSpec((t_s, MIX_WIDTH), row),
        out_shape=jax.ShapeDtypeStruct((n_tok, MIX_WIDTH), BF16),
        scratch_shapes=[pltpu.VMEM((N_HEADS_A, t_s, BIAS_PERIOD), F32)],
        compiler_params=pltpu.CompilerParams(
            dimension_semantics=("arbitrary",), vmem_limit_bytes=V7X_VMEM_LIMIT_BYTES),
        name="attn_sample",
    )(sinks, qa, ka_cache, ka, va_cache, va, qb, kb_cache, kb, vb_cache, vb, rbv, goa, gob)


def _tail_kernel(h_ref, o_ref, p_ref, wout_ref, gffn_ref, wgu_ref, wdown_ref, wgate_ref,
                 wproj_ref, gfin_ref, out_ref, *, final):
    h = h_ref[...] + _dot(o_ref[...], wout_ref[...])
    gu = _dot(_rms(h, gffn_ref[...]).astype(BF16), wgu_ref[...])
    act = (jax.nn.silu(gu[:, :D_FF]) * gu[:, D_FF:]).astype(BF16)
    h = h + _dot(act, wdown_ref[...])
    gate = jax.nn.sigmoid(_dot(h.astype(BF16), wgate_ref[...]))
    h = h + gate * _dot(p_ref[...].astype(BF16), wproj_ref[...])
    if final:
        h = _rms(h, gfin_ref[...])
    out_ref[...] = h


def _tail(h, o, p, wout, gffn, wgu, wdown, wgate, wproj, gfin, *, tm, final):
    n_tok = h.shape[0]
    row = lambda i: (i, 0)
    return pl.pallas_call(
        functools.partial(_tail_kernel, final=final),
        grid=(n_tok // tm,),
        in_specs=[
            pl.BlockSpec((tm, D_MODEL), row),
            pl.BlockSpec((tm, MIX_WIDTH), row),
            pl.BlockSpec((tm, D_PLE), row),
            _resident((MIX_WIDTH, D_MODEL)),
            _resident((1, D_MODEL)),
            _resident((D_MODEL, 2 * D_FF)),
            _resident((D_FF, D_MODEL)),
            _resident((D_MODEL, D_MODEL)),
            _resident((D_PLE, D_MODEL)),
            _resident((1, D_MODEL)),
        ],
        out_specs=pl.BlockSpec((tm, D_MODEL), row),
        out_shape=jax.ShapeDtypeStruct((n_tok, D_MODEL), F32),
        compiler_params=pltpu.CompilerParams(
            dimension_semantics=("arbitrary",), vmem_limit_bytes=V7X_VMEM_LIMIT_BYTES),
        name="layer_tail",
    )(h, o, p, wout, gffn, wgu, wdown, wgate, wproj, gfin)


def _rope_lane_tables(pos):
    half = ROT_DIM // 2
    d = np.arange(PAIR) % HEAD_DIM
    inv = ROPE_THETA ** (-(2.0 * (d % half)) / ROT_DIM)
    ang = np.where(d < ROT_DIM, np.asarray(pos, np.float64)[:, None] * inv[None, :], 0.0)
    lower = (d < half).astype(np.float64)[None, :]
    upper = ((d >= half) & (d < ROT_DIM)).astype(np.float64)[None, :]
    return np.cos(ang), np.sin(ang), lower, upper


def _rope_tables_direct(pos):
    cos, sin, lower, upper = _rope_lane_tables(pos)
    return jnp.asarray(np.stack([cos, -sin * lower, sin * upper]), F32)


def _rope_tables_split(n_tiles, tile):
    cr, sr, lower, upper = _rope_lane_tables(np.arange(tile))
    ct, st, _, _ = _rope_lane_tables(np.arange(n_tiles) * tile)
    rows = np.stack([cr, sr, -cr * lower, -sr * lower, cr * upper, sr * upper])
    return jnp.asarray(rows, F32), jnp.asarray(ct, F32), jnp.asarray(st, F32)


def _rel_bias_row(rel_bias):
    u = np.arange(BIAS_PERIOD)
    diff = np.where(u < BIAS_PERIOD // 2 + A_ROWS // 2, u, u - BIAS_PERIOD)
    idx = np.clip(A_ROWS - diff, -REL_CLIP, REL_CLIP) + REL_CLIP
    return rel_bias.astype(F32)[:, idx]


def kernel(x_prompt, x_sample, p_prompt, p_sample, cache_a_k, cache_a_v, cache_b_k, cache_b_v,
           g_mix_norm, w_in, rel_bias_a, sinks_b, g_out_a, g_out_b, w_out, g_ffn_norm,
           w_gate_up, w_down, w_ple_proj, w_ple_gate, g_final):
    b_p, s_p, _ = x_prompt.shape
    b_s, t_s, _ = x_sample.shape
    depth = w_in.shape[0]
    la_c, lb_c = cache_a_k.shape[2], cache_b_k.shape[2]
    keep_a_s = min(A_ROWS, la_c + t_s)
    keep_b_s = min(WINDOW_B, lb_c + t_s)
    tm_s = 512
    tm_tail = 512
    assert s_p % TQ == 0 and s_p >= A_ROWS and (b_s * t_s) % tm_s == 0 and tm_s % t_s == 0
    assert la_c == A_ROWS and lb_c == WINDOW_B and t_s == CHUNK

    rope_rows, rope_ct, rope_st = _rope_tables_split(s_p // TQ, TQ)
    rope_s = _rope_tables_direct(PAST_LEN + np.arange(tm_s) % t_s)
    row2 = lambda a: a.reshape(1, -1).astype(F32)
    g_fin = row2(g_final)

    hp = x_prompt
    hs = x_sample.reshape(b_s * t_s, D_MODEL)
    outs = [[] for _ in range(8)]
    for i in range(depth):
        w_in_i = w_in[i].astype(BF16)
        tail_w = (w_out[i].astype(BF16), row2(g_ffn_norm[i]), w_gate_up[i].astype(BF16),
                  w_down[i].astype(BF16), w_ple_gate[i].astype(BF16), w_ple_proj[i].astype(BF16),
                  g_fin)
        g_mix = row2(g_mix_norm[i])
        goa, gob = row2(g_out_a[i]), row2(g_out_b[i])
        sinks = sinks_b[i].astype(F32)
        rbv = _rel_bias_row(rel_bias_a[i])
        final = i == depth - 1

        o, cak, cav, cbk, cbv = _mix_prompt(sinks, hp, g_mix, w_in_i, rope_rows, rope_ct, rope_st,
                                            rbv, goa, gob)
        hp = _tail(hp.reshape(b_p * s_p, D_MODEL), o.reshape(b_p * s_p, MIX_WIDTH),
                   p_prompt[i].reshape(b_p * s_p, D_PLE), *tail_w, tm=tm_tail, final=final)
        hp = hp.reshape(b_p, s_p, D_MODEL)
        outs[0].append(cak.reshape(b_p, A_ROWS, N_HEADS_A, HEAD_DIM))
        outs[1].append(cav.reshape(b_p, A_ROWS, N_HEADS_A, HEAD_DIM))
        outs[2].append(cbk.reshape(b_p, WINDOW_B, N_KV_B, HEAD_DIM))
        outs[3].append(cbv.reshape(b_p, WINDOW_B, N_KV_B, HEAD_DIM))

        qa, ka, va, qb, kb, vb = _in_proj(hs, g_mix, w_in_i, rope_s, tm=tm_s)
        cak_i = cache_a_k[i].reshape(b_s, la_c, WIDTH_A)
        cav_i = cache_a_v[i].reshape(b_s, la_c, WIDTH_A)
        cbk_i = cache_b_k[i].reshape(b_s, lb_c, KV_WIDTH_B)
        cbv_i = cache_b_v[i].reshape(b_s, lb_c, KV_WIDTH_B)
        o = _attn_sample(sinks, qa, cak_i, ka, cav_i, va, qb, cbk_i, kb, cbv_i, vb,
                         rbv, goa, gob, t_s=t_s)
        hs = _tail(hs, o, p_sample[i].reshape(b_s * t_s, D_PLE), *tail_w, tm=tm_tail, final=final)
        roll = lambda cache, new, keep, shp: jnp.concatenate(
            [cache, new.reshape(b_s, t_s, -1)], axis=1)[:, -keep:].reshape(b_s, keep, *shp)
        outs[4].append(roll(cak_i, ka, keep_a_s, (N_HEADS_A, HEAD_DIM)))
        outs[5].append(roll(cav_i, va, keep_a_s, (N_HEADS_A, HEAD_DIM)))
        outs[6].append(roll(cbk_i, kb, keep_b_s, (N_KV_B, HEAD_DIM)))
        outs[7].append(roll(cbv_i, vb, keep_b_s, (N_KV_B, HEAD_DIM)))

    y_sample = hs.reshape(b_s, t_s, D_MODEL)
    return (hp, y_sample) + tuple(jnp.stack(o) for o in outs)
```

```python
import functools
import math

import numpy as np
import jax
import jax.numpy as jnp
from jax import lax
from jax.experimental import pallas as pl
from jax.experimental.pallas import tpu as pltpu

D_MODEL = 1024
CHUNK = 64
HEAD_DIM = 64
N_HEADS_A = 8
N_HEADS_B = 8
N_KV_B = 2
GROUP_B = N_HEADS_B // N_KV_B
WIDTH_A = N_HEADS_A * HEAD_DIM
WIDTH_B = N_HEADS_B * HEAD_DIM
KV_WIDTH_B = N_KV_B * HEAD_DIM
MIX_WIDTH = WIDTH_A + WIDTH_B
IN_WIDTH = 3 * WIDTH_A + WIDTH_B + 2 * KV_WIDTH_B
PREV_CHUNKS_A = 8
A_ROWS = PREV_CHUNKS_A * CHUNK
REL_CLIP = 256
WINDOW_B = 128
PREV_CHUNKS_B = WINDOW_B // CHUNK
ROT_DIM = HEAD_DIM // 4
ROPE_THETA = 500000.0
D_FF = 2816
D_PLE = 256
PAST_LEN = 2048
RMS_EPS = 1e-6
LOG2E = math.log2(math.e)
QSCALE = HEAD_DIM ** -0.5 * LOG2E
PAIR = 2 * HEAD_DIM
N_PAIRS_A = WIDTH_A // PAIR
N_PAIRS_B = WIDTH_B // PAIR
PAIRS_PER_KV = GROUP_B // 2
NEG_INF = float("-inf")
BIAS_PERIOD = 1024

V7X_VMEM_LIMIT_BYTES = 56 * 1024 * 1024

BF16 = jnp.bfloat16
F32 = jnp.float32

O_KA, O_VA, O_QB = WIDTH_A, 2 * WIDTH_A, 3 * WIDTH_A
O_KB = O_QB + WIDTH_B
O_VB = O_KB + KV_WIDTH_B


def _rms(x, g):
    return x * lax.rsqrt(jnp.mean(x * x, axis=-1, keepdims=True) + RMS_EPS) * g


def _dot(a, b):
    return jnp.dot(a, b, preferred_element_type=F32)


def _dot_nt(a, b):
    return lax.dot_general(a, b, (((1,), (1,)), ((), ())), preferred_element_type=F32)


def _resident(shape):
    nd = len(shape)
    return pl.BlockSpec(shape, lambda *_: (0,) * nd, pipeline_mode=pl.Buffered(1))


def _rope(x, cos, sdn, sup):
    half = ROT_DIM // 2
    return (x * cos + pltpu.roll(x, PAIR - half, axis=1) * sdn
            + pltpu.roll(x, half, axis=1) * sup)


def _low_lanes(shape):
    return lax.broadcasted_iota(jnp.int32, shape, 1) < HEAD_DIM


def _keep_half(x, half, fill):
    keep = _low_lanes(x.shape) if half == 0 else ~_low_lanes(x.shape)
    return jnp.where(keep, x, jnp.full_like(x, fill))


def _keep_rows(x_t, half):
    row = lax.broadcasted_iota(jnp.int32, x_t.shape, 0)
    keep = (row < HEAD_DIM) if half == 0 else (row >= HEAD_DIM)
    return jnp.where(keep, x_t, jnp.zeros_like(x_t))


def _kv_b_variant(x, swapped, g, half, fill):
    return _keep_half(x if g == half else swapped, half, fill).astype(BF16)


def _exp_scores(s_blocks, sink=None):
    if len({s.shape for s in s_blocks}) == 1:
        m = jnp.max(functools.reduce(jnp.maximum, s_blocks), axis=-1, keepdims=True)
    else:
        m = functools.reduce(jnp.maximum, [jnp.max(s, axis=-1, keepdims=True) for s in s_blocks])
    if sink is not None:
        m = jnp.maximum(m, sink)
    p = jnp.concatenate([jnp.exp2(s - m).astype(BF16) for s in s_blocks], axis=-1)
    return p, (None if sink is None else jnp.exp2(sink - m))


def _pv_pair(ps, vs, extras=(None, None)):
    x = [_dot(p, v) for p, v in zip(ps, vs)]
    low = _low_lanes(x[0].shape)
    out = jnp.where(low, x[0], x[1])
    den = pltpu.roll(jnp.where(low, x[1], x[0]), HEAD_DIM, axis=1)
    if extras[0] is not None:
        den = den + jnp.where(low, extras[0], extras[1])
    return out / den


def _rolled_bias_rows(rbv_ref, h, rows):
    x = jnp.broadcast_to(rbv_ref[h:h + 1, :] * LOG2E, (rows, BIAS_PERIOD))
    return pltpu.roll(x, 0, axis=1, stride=1, stride_axis=0)


def _band_valid(rows, col0, cols, prev_chunks):
    r = lax.broadcasted_iota(jnp.int32, (rows, cols), 0) // CHUNK
    c = (lax.broadcasted_iota(jnp.int32, (rows, cols), 1) + col0) // CHUNK
    return (c >= r) & (c <= r + prev_chunks)


TQ = 256
NBLK_A = A_ROWS // TQ + 1
WIN_A = NBLK_A * TQ
WIN_B = WINDOW_B + TQ
NEG_KIND = NBLK_A


def _mix_prompt_kernel(sinks_ref, x_ref, g_ref, w_ref, rope_ref, ct_ref, st_ref, rbv_ref,
                       goa_ref, gob_ref,
                       o_ref, cak_ref, cav_ref, cbk_ref, cbv_ref,
                       kta_scr, va_scr, ktb_scr, vb_scr, bias_scr, maskb_scr, *, n_tiles):
    b, t = pl.program_id(0), pl.program_id(1)

    @pl.when((b == 0) & (t == 0))
    def _init():
        kta_scr[...] = jnp.zeros_like(kta_scr)
        va_scr[...] = jnp.zeros_like(va_scr)
        ktb_scr[...] = jnp.zeros_like(ktb_scr)
        vb_scr[...] = jnp.zeros_like(vb_scr)
        for h in range(N_HEADS_A):
            rows = _rolled_bias_rows(rbv_ref, h, TQ)
            for j in range(NBLK_A):
                valid = _band_valid(TQ, j * TQ, TQ, PREV_CHUNKS_A)
                bias_scr[j, h] = jnp.where(valid, rows[:, j * TQ:(j + 1) * TQ], NEG_INF)
            bias_scr[NEG_KIND, h] = jnp.full((TQ, TQ), NEG_INF, F32)
        band = jnp.where(_band_valid(TQ, 0, WIN_B, PREV_CHUNKS_B), jnp.zeros((TQ, WIN_B), F32), NEG_INF)
        col = lax.broadcasted_iota(jnp.int32, (TQ, WIN_B), 1)
        maskb_scr[0] = band
        maskb_scr[1] = jnp.where(col >= WINDOW_B, band, NEG_INF)

    n = _rms(x_ref[0], g_ref[...]).astype(BF16)
    z = _dot(n, w_ref[...])
    ct, st = ct_ref[pl.ds(t, 1), :], st_ref[pl.ds(t, 1), :]
    cos = ct * rope_ref[0] - st * rope_ref[1]
    sdn = st * rope_ref[2] + ct * rope_ref[3]
    sup = st * rope_ref[4] + ct * rope_ref[5]
    ka, va = z[:, O_KA:O_VA], z[:, O_VA:O_QB]
    kb = _rope(z[:, O_KB:O_VB], cos, sdn, sup)
    vb = z[:, O_VB:]
    qa = (z[:, :O_KA] * QSCALE).astype(BF16)
    qb = [(_rope(z[:, O_QB + j * PAIR:O_QB + (j + 1) * PAIR], cos, sdn, sup) * QSCALE).astype(BF16)
          for j in range(N_PAIRS_B)]

    @pl.when(t >= n_tiles - A_ROWS // TQ)
    def _():
        cak_ref[0] = ka
        cav_ref[0] = va

    @pl.when(t == n_tiles - 1)
    def _():
        cbk_ref[0] = kb[TQ - WINDOW_B:, :]
        cbv_ref[0] = vb[TQ - WINDOW_B:, :]

    old = (NBLK_A - 1) * TQ
    for p in range(N_PAIRS_A):
        sl = slice(p * PAIR, (p + 1) * PAIR)
        ka_t = ka[:, sl].T
        for half in range(2):
            c0 = half * WIN_A
            kta_scr[p, :, c0:c0 + old] = kta_scr[p, :, c0 + TQ:c0 + WIN_A]
            kta_scr[p, :, c0 + old:c0 + WIN_A] = _keep_rows(ka_t, half).astype(BF16)
            va_scr[p, half, :old, :] = va_scr[p, half, TQ:, :]
            va_scr[p, half, old:, :] = _keep_half(va[:, sl], half, 1.0).astype(BF16)
    kinds = [jnp.where(t >= NBLK_A - 1 - j, j, NEG_KIND) for j in range(NBLK_A - 1)] + [NBLK_A - 1]
    oa = []
    for p in range(N_PAIRS_A):
        s = _dot(qa[:, p * PAIR:(p + 1) * PAIR], kta_scr[p])
        probs = []
        for half in range(2):
            s_blocks = [s[:, half * WIN_A + j * TQ:half * WIN_A + (j + 1) * TQ]
                        + bias_scr[kinds[j], 2 * p + half] for j in range(NBLK_A)]
            probs.append(_exp_scores(s_blocks)[0])
        oa.append(_pv_pair(probs, [va_scr[p, 0], va_scr[p, 1]]))
    oa = jnp.concatenate(oa, axis=-1)

    kb_t = kb.T
    kb_t_sw = jnp.concatenate([kb_t[HEAD_DIM:], kb_t[:HEAD_DIM]], axis=0)
    vb_sw = pltpu.roll(vb, HEAD_DIM, axis=1)
    for g in range(N_KV_B):
        for half in range(2):
            c0 = half * WIN_B
            ktb_scr[g, :, c0:c0 + WINDOW_B] = ktb_scr[g, :, c0 + TQ:c0 + WIN_B]
            ktb_scr[g, :, c0 + WINDOW_B:c0 + WIN_B] = _keep_rows(
                kb_t if g == half else kb_t_sw, half).astype(BF16)
            vb_scr[g, half, :WINDOW_B, :] = vb_scr[g, half, TQ:, :]
            vb_scr[g, half, WINDOW_B:, :] = _kv_b_variant(vb, vb_sw, g, half, 1.0)
    maskb = maskb_scr[jnp.where(t >= 1, 0, 1)]
    ob = [None] * N_PAIRS_B
    for g in range(N_KV_B):
        pairs = range(g * PAIRS_PER_KV, (g + 1) * PAIRS_PER_KV)
        s = _dot(jnp.concatenate([qb[p] for p in pairs], axis=0), ktb_scr[g])
        for i, p in enumerate(pairs):
            probs, extras = [], []
            for half in range(2):
                sb = s[i * TQ:(i + 1) * TQ, half * WIN_B:(half + 1) * WIN_B] + maskb
                pr, ex = _exp_scores([sb], sinks_ref[2 * p + half] * LOG2E)
                probs.append(pr)
                extras.append(ex)
            ob[p] = _pv_pair(probs, [vb_scr[g, 0], vb_scr[g, 1]], extras)
    ob = jnp.concatenate(ob, axis=-1)

    o_ref[0, :, :WIDTH_A] = _rms(oa, goa_ref[...]).astype(BF16)
    o_ref[0, :, WIDTH_A:] = _rms(ob, gob_ref[...]).astype(BF16)


def _mix_prompt(sinks, x, g, w_bf16, rope_rows, ct, st, rbv, goa, gob):
    b, s, _ = x.shape
    nt = s // TQ
    cur = lambda bi, t: (bi, t, 0)
    keep_a = lambda bi, t: (bi, jnp.maximum(t - (nt - A_ROWS // TQ), 0), 0)
    seq = lambda bi, t: (bi, 0, 0)
    return pl.pallas_call(
        functools.partial(_mix_prompt_kernel, n_tiles=nt),
        grid=(b, nt),
        in_specs=[
            pl.BlockSpec(memory_space=pltpu.SMEM),
            pl.BlockSpec((1, TQ, D_MODEL), cur),
            _resident((1, D_MODEL)),
            _resident((D_MODEL, IN_WIDTH)),
            _resident(rope_rows.shape),
            _resident(ct.shape),
            _resident(st.shape),
            _resident((N_HEADS_A, BIAS_PERIOD)),
            _resident((1, WIDTH_A)),
            _resident((1, WIDTH_B)),
        ],
        out_specs=[
            pl.BlockSpec((1, TQ, MIX_WIDTH), cur),
            pl.BlockSpec((1, TQ, WIDTH_A), keep_a),
            pl.BlockSpec((1, TQ, WIDTH_A), keep_a),
            pl.BlockSpec((1, WINDOW_B, KV_WIDTH_B), seq),
            pl.BlockSpec((1, WINDOW_B, KV_WIDTH_B), seq),
        ],
        out_shape=[
            jax.ShapeDtypeStruct((b, s, MIX_WIDTH), BF16),
            jax.ShapeDtypeStruct((b, A_ROWS, WIDTH_A), F32),
            jax.ShapeDtypeStruct((b, A_ROWS, WIDTH_A), F32),
            jax.ShapeDtypeStruct((b, WINDOW_B, KV_WIDTH_B), F32),
            jax.ShapeDtypeStruct((b, WINDOW_B, KV_WIDTH_B), F32),
        ],
        scratch_shapes=[
            pltpu.VMEM((N_PAIRS_A, PAIR, 2 * WIN_A), BF16),
            pltpu.VMEM((N_PAIRS_A, 2, WIN_A, PAIR), BF16),
            pltpu.VMEM((N_KV_B, PAIR, 2 * WIN_B), BF16),
            pltpu.VMEM((N_KV_B, 2, WIN_B, PAIR), BF16),
            pltpu.VMEM((NBLK_A + 1, N_HEADS_A, TQ, TQ), F32),
            pltpu.VMEM((2, TQ, WIN_B), F32),
        ],
        compiler_params=pltpu.CompilerParams(
            dimension_semantics=("arbitrary", "arbitrary"),
            vmem_limit_bytes=V7X_VMEM_LIMIT_BYTES),
        name="mix_prompt",
    )(sinks, x, g, w_bf16, rope_rows, ct, st, rbv, goa, gob)


def _in_proj_kernel(x_ref, g_ref, w_ref, rope_ref, qa_ref, ka_ref, va_ref, qb_ref, kb_ref, vb_ref):
    n = _rms(x_ref[...], g_ref[...]).astype(BF16)
    z = _dot(n, w_ref[...])
    cos, sdn, sup = rope_ref[0], rope_ref[1], rope_ref[2]
    qa_ref[...] = (z[:, :O_KA] * QSCALE).astype(BF16)
    ka_ref[...] = z[:, O_KA:O_VA]
    va_ref[...] = z[:, O_VA:O_QB]
    for j in range(N_PAIRS_B):
        sl = slice(O_QB + j * PAIR, O_QB + (j + 1) * PAIR)
        qb_ref[:, j * PAIR:(j + 1) * PAIR] = (_rope(z[:, sl], cos, sdn, sup) * QSCALE).astype(BF16)
    kb_ref[...] = _rope(z[:, O_KB:O_VB], cos, sdn, sup)
    vb_ref[...] = z[:, O_VB:]


def _in_proj(x, g, w_bf16, rope_tabs, *, tm):
    n_tok = x.shape[0]
    row = lambda i: (i, 0)
    widths = (WIDTH_A, WIDTH_A, WIDTH_A, WIDTH_B, KV_WIDTH_B, KV_WIDTH_B)
    dtypes = (BF16, F32, F32, BF16, F32, F32)
    return pl.pallas_call(
        _in_proj_kernel,
        grid=(n_tok // tm,),
        in_specs=[
            pl.BlockSpec((tm, D_MODEL), row),
            _resident((1, D_MODEL)),
            _resident((D_MODEL, IN_WIDTH)),
            _resident(rope_tabs.shape),
        ],
        out_specs=[pl.BlockSpec((tm, w), row) for w in widths],
        out_shape=[jax.ShapeDtypeStruct((n_tok, w), d) for w, d in zip(widths, dtypes)],
        compiler_params=pltpu.CompilerParams(
            dimension_semantics=("arbitrary",), vmem_limit_bytes=V7X_VMEM_LIMIT_BYTES),
        name="in_proj",
    )(x, g, w_bf16, rope_tabs)


def _attn_sample_kernel(sinks_ref, qa_ref, kac_ref, kan_ref, vac_ref, van_ref,
                        qb_ref, kbc_ref, kbn_ref, vbc_ref, vbn_ref,
                        rbv_ref, goa_ref, gob_ref, o_ref, bias_scr):
    t_s = qa_ref.shape[0]
    la = kac_ref.shape[1]

    @pl.when(pl.program_id(0) == 0)
    def _init():
        for h in range(N_HEADS_A):
            bias_scr[h] = _rolled_bias_rows(rbv_ref, h, t_s)

    oa = []
    for p in range(N_PAIRS_A):
        sl = slice(p * PAIR, (p + 1) * PAIR)
        qp = qa_ref[:, sl]
        kc, kn, vc, vn = kac_ref[0, :, sl], kan_ref[:, sl], vac_ref[0, :, sl], van_ref[:, sl]
        probs, vals = [], []
        for half in range(2):
            h = 2 * p + half
            s_blocks = [_dot_nt(qp, _keep_half(kc, half, 0.0).astype(BF16)) + bias_scr[h, :, :la],
                        _dot_nt(qp, _keep_half(kn, half, 0.0).astype(BF16)) + bias_scr[h, :, la:la + t_s]]
            probs.append(_exp_scores(s_blocks)[0])
            vals.append(jnp.concatenate([_keep_half(vc, half, 1.0), _keep_half(vn, half, 1.0)],
                                        axis=0).astype(BF16))
        oa.append(_pv_pair(probs, vals))
    oa = jnp.concatenate(oa, axis=-1)

    kb = jnp.concatenate([kbc_ref[0], kbn_ref[...]], axis=0)
    vb = jnp.concatenate([vbc_ref[0], vbn_ref[...]], axis=0)
    kb_sw, vb_sw = pltpu.roll(kb, HEAD_DIM, axis=1), pltpu.roll(vb, HEAD_DIM, axis=1)
    ob = []
    for p in range(N_PAIRS_B):
        g = p // PAIRS_PER_KV
        qp = qb_ref[:, p * PAIR:(p + 1) * PAIR]
        probs, extras, vals = [], [], []
        for half in range(2):
            s = _dot_nt(qp, _kv_b_variant(kb, kb_sw, g, half, 0.0))
            pr, ex = _exp_scores([s], sinks_ref[2 * p + half] * LOG2E)
            probs.append(pr)
            extras.append(ex)
            vals.append(_kv_b_variant(vb, vb_sw, g, half, 1.0))
        ob.append(_pv_pair(probs, vals, extras))
    ob = jnp.concatenate(ob, axis=-1)
    o_ref[:, :WIDTH_A] = _rms(oa, goa_ref[...]).astype(BF16)
    o_ref[:, WIDTH_A:] = _rms(ob, gob_ref[...]).astype(BF16)


def _attn_sample(sinks, qa, ka_cache, ka, va_cache, va, qb, kb_cache, kb, vb_cache, vb,
                 rbv, goa, gob, *, t_s):
    n_tok = qa.shape[0]
    la, lb = ka_cache.shape[1], kb_cache.shape[1]
    row = lambda i: (i, 0)
    seq3 = lambda i: (i, 0, 0)
    return pl.pallas_call(
        _attn_sample_kernel,
        grid=(n_tok // t_s,),
        in_specs=[
            pl.BlockSpec(memory_space=pltpu.SMEM),
            pl.BlockSpec((t_s, WIDTH_A), row),
            pl.BlockSpec((1, la, WIDTH_A), seq3),
            pl.BlockSpec((t_s, WIDTH_A), row),
            pl.BlockSpec((1, la, WIDTH_A), seq3),
            pl.BlockSpec((t_s, WIDTH_A), row),
            pl.BlockSpec((t_s, WIDTH_B), row),
            pl.BlockSpec((1, lb, KV_WIDTH_B), seq3),
            pl.BlockSpec((t_s, KV_WIDTH_B), row),
            pl.BlockSpec((1, lb, KV_WIDTH_B), seq3),
            pl.BlockSpec((t_s, KV_WIDTH_B), row),
            _resident((N_HEADS_A, BIAS_PERIOD)),
            _resident((1, WIDTH_A)),
            _resident((1, WIDTH_B)),
        ],
        out_specs=pl.BlockSpec((t_s, MIX_WIDTH), row),
        out_shape=jax.ShapeDtypeStruct((n_tok, MIX_WIDTH), BF16),
        scratch_shapes=[pltpu.VMEM((N_HEADS_A, t_s, BIAS_PERIOD), F32)],
        compiler_params=pltpu.CompilerParams(
            dimension_semantics=("arbitrary",), vmem_limit_bytes=V7X_VMEM_LIMIT_BYTES),
        name="attn_sample",
    )(sinks, qa, ka_cache, ka, va_cache, va, qb, kb_cache, kb, vb_cache, vb, rbv, goa, gob)


def _tail_kernel(h_ref, o_ref, p_ref, wout_ref, gffn_ref, wgu_ref, wdown_ref, wgate_ref,
                 wproj_ref, gfin_ref, out_ref, *, final):
    h = h_ref[...] + _dot(o_ref[...], wout_ref[...])
    gu = _dot(_rms(h, gffn_ref[...]).astype(BF16), wgu_ref[...])
    act = (jax.nn.silu(gu[:, :D_FF]) * gu[:, D_FF:]).astype(BF16)
    h = h + _dot(act, wdown_ref[...])
    gate = jax.nn.sigmoid(_dot(h.astype(BF16), wgate_ref[...]))
    h = h + gate * _dot(p_ref[...].astype(BF16), wproj_ref[...])
    if final:
        h = _rms(h, gfin_ref[...])
    out_ref[...] = h


def _tail(h, o, p, wout, gffn, wgu, wdown, wgate, wproj, gfin, *, tm, final):
    n_tok = h.shape[0]
    row = lambda i: (i, 0)
    return pl.pallas_call(
        functools.partial(_tail_kernel, final=final),
        grid=(n_tok // tm,),
        in_specs=[
            pl.BlockSpec((tm, D_MODEL), row),
            pl.BlockSpec((tm, MIX_WIDTH), row),
            pl.BlockSpec((tm, D_PLE), row),
            _resident((MIX_WIDTH, D_MODEL)),
            _resident((1, D_MODEL)),
            _resident((D_MODEL, 2 * D_FF)),
            _resident((D_FF, D_MODEL)),
            _resident((D_MODEL, D_MODEL)),
            _resident((D_PLE, D_MODEL)),
            _resident((1, D_MODEL)),
        ],
        out_specs=pl.BlockSpec((tm, D_MODEL), row),
        out_shape=jax.ShapeDtypeStruct((n_tok, D_MODEL), F32),
        compiler_params=pltpu.CompilerParams(
            dimension_semantics=("arbitrary",), vmem_limit_bytes=V7X_VMEM_LIMIT_BYTES),
        name="layer_tail",
    )(h, o, p, wout, gffn, wgu, wdown, wgate, wproj, gfin)


def _rope_lane_tables(pos):
    half = ROT_DIM // 2
    d = np.arange(PAIR) % HEAD_DIM
    inv = ROPE_THETA ** (-(2.0 * (d % half)) / ROT_DIM)
    ang = np.where(d < ROT_DIM, np.asarray(pos, np.float64)[:, None] * inv[None, :], 0.0)
    lower = (d < half).astype(np.float64)[None, :]
    upper = ((d >= half) & (d < ROT_DIM)).astype(np.float64)[None, :]
    return np.cos(ang), np.sin(ang), lower, upper


def _rope_tables_direct(pos):
    cos, sin, lower, upper = _rope_lane_tables(pos)
    return jnp.asarray(np.stack([cos, -sin * lower, sin * upper]), F32)


def _rope_tables_split(n_tiles, tile):
    cr, sr, lower, upper = _rope_lane_tables(np.arange(tile))
    ct, st, _, _ = _rope_lane_tables(np.arange(n_tiles) * tile)
    rows = np.stack([cr, sr, -cr * lower, -sr * lower, cr * upper, sr * upper])
    return jnp.asarray(rows, F32), jnp.asarray(ct, F32), jnp.asarray(st, F32)


def _rel_bias_row(rel_bias):
    u = np.arange(BIAS_PERIOD)
    diff = np.where(u < BIAS_PERIOD // 2 + A_ROWS // 2, u, u - BIAS_PERIOD)
    idx = np.clip(A_ROWS - diff, -REL_CLIP, REL_CLIP) + REL_CLIP
    return rel_bias.astype(F32)[:, idx]


def kernel(x_prompt, x_sample, p_prompt, p_sample, cache_a_k, cache_a_v, cache_b_k, cache_b_v,
           g_mix_norm, w_in, rel_bias_a, sinks_b, g_out_a, g_out_b, w_out, g_ffn_norm,
           w_gate_up, w_down, w_ple_proj, w_ple_gate, g_final):
    b_p, s_p, _ = x_prompt.shape
    b_s, t_s, _ = x_sample.shape
    depth = w_in.shape[0]
    la_c, lb_c = cache_a_k.shape[2], cache_b_k.shape[2]
    keep_a_s = min(A_ROWS, la_c + t_s)
    keep_b_s = min(WINDOW_B, lb_c + t_s)
    tm_s = 512
    tm_tail = 512
    assert s_p % TQ == 0 and s_p >= A_ROWS and (b_s * t_s) % tm_s == 0 and tm_s % t_s == 0
    assert la_c == A_ROWS and lb_c == WINDOW_B and t_s == CHUNK

    rope_rows, rope_ct, rope_st = _rope_tables_split(s_p // TQ, TQ)
    rope_s = _rope_tables_direct(PAST_LEN + np.arange(tm_s) % t_s)
    row2 = lambda a: a.reshape(1, -1).astype(F32)
    g_fin = row2(g_final)

    hp = x_prompt
    hs = x_sample.reshape(b_s * t_s, D_MODEL)
    outs = [[] for _ in range(8)]
    for i in range(depth):
        w_in_i = w_in[i].astype(BF16)
        tail_w = (w_out[i].astype(BF16), row2(g_ffn_norm[i]), w_gate_up[i].astype(BF16),
                  w_down[i].astype(BF16), w_ple_gate[i].astype(BF16), w_ple_proj[i].astype(BF16),
                  g_fin)
        g_mix = row2(g_mix_norm[i])
        goa, gob = row2(g_out_a[i]), row2(g_out_b[i])
        sinks = sinks_b[i].astype(F32)
        rbv = _rel_bias_row(rel_bias_a[i])
        final = i == depth - 1

        o, cak, cav, cbk, cbv = _mix_prompt(sinks, hp, g_mix, w_in_i, rope_rows, rope_ct, rope_st,
                                            rbv, goa, gob)
        hp = _tail(hp.reshape(b_p * s_p, D_MODEL), o.reshape(b_p * s_p, MIX_WIDTH),
                   p_prompt[i].reshape(b_p * s_p, D_PLE), *tail_w, tm=tm_tail, final=final)
        hp = hp.reshape(b_p, s_p, D_MODEL)
        outs[0].append(cak.reshape(b_p, A_ROWS, N_HEADS_A, HEAD_DIM))
        outs[1].append(cav.reshape(b_p, A_ROWS, N_HEADS_A, HEAD_DIM))
        outs[2].append(cbk.reshape(b_p, WINDOW_B, N_KV_B, HEAD_DIM))
        outs[3].append(cbv.reshape(b_p, WINDOW_B, N_KV_B, HEAD_DIM))

        qa, ka, va, qb, kb, vb = _in_proj(hs, g_mix, w_in_i, rope_s, tm=tm_s)
        cak_i = cache_a_k[i].reshape(b_s, la_c, WIDTH_A)
        cav_i = cache_a_v[i].reshape(b_s, la_c, WIDTH_A)
        cbk_i = cache_b_k[i].reshape(b_s, lb_c, KV_WIDTH_B)
        cbv_i = cache_b_v[i].reshape(b_s, lb_c, KV_WIDTH_B)
        o = _attn_sample(sinks, qa, cak_i, ka, cav_i, va, qb, cbk_i, kb, cbv_i, vb,
                         rbv, goa, gob, t_s=t_s)
        hs = _tail(hs, o, p_sample[i].reshape(b_s * t_s, D_PLE), *tail_w, tm=tm_tail, final=final)
        roll = lambda cache, new, keep, shp: jnp.concatenate(
            [cache, new.reshape(b_s, t_s, -1)], axis=1)[:, -keep:].reshape(b_s, keep, *shp)
        outs[4].append(roll(cak_i, ka, keep_a_s, (N_HEADS_A, HEAD_DIM)))
        outs[5].append(roll(cav_i, va, keep_a_s, (N_HEADS_A, HEAD_DIM)))
        outs[6].append(roll(cbk_i, kb, keep_b_s, (N_KV_B, HEAD_DIM)))
        outs[7].append(roll(cbv_i, vb, keep_b_s, (N_KV_B, HEAD_DIM)))

    y_sample = hs.reshape(b_s, t_s, D_MODEL)
    return (hp, y_sample) + tuple(jnp.stack(o) for o in outs)
```

```python
import functools
import math

import numpy as np
import jax
import jax.numpy as jnp
from jax import lax
from jax.experimental import pallas as pl
from jax.experimental.pallas import tpu as pltpu

D_MODEL = 1024
CHUNK = 64
HEAD_DIM = 64
N_HEADS_A = 8
N_HEADS_B = 8
N_KV_B = 2
GROUP_B = N_HEADS_B // N_KV_B
WIDTH_A = N_HEADS_A * HEAD_DIM
WIDTH_B = N_HEADS_B * HEAD_DIM
KV_WIDTH_B = N_KV_B * HEAD_DIM
MIX_WIDTH = WIDTH_A + WIDTH_B
IN_WIDTH = 3 * WIDTH_A + WIDTH_B + 2 * KV_WIDTH_B
PREV_CHUNKS_A = 8
A_ROWS = PREV_CHUNKS_A * CHUNK
REL_CLIP = 256
WINDOW_B = 128
PREV_CHUNKS_B = WINDOW_B // CHUNK
ROT_DIM = HEAD_DIM // 4
ROPE_THETA = 500000.0
D_FF = 2816
D_PLE = 256
PAST_LEN = 2048
RMS_EPS = 1e-6
LOG2E = math.log2(math.e)
QSCALE = HEAD_DIM ** -0.5 * LOG2E
PAIR = 2 * HEAD_DIM
N_PAIRS_A = WIDTH_A // PAIR
N_PAIRS_B = WIDTH_B // PAIR
PAIRS_PER_KV = GROUP_B // 2
NEG_INF = float("-inf")
BIAS_PERIOD = 1024

V7X_VMEM_LIMIT_BYTES = 56 * 1024 * 1024

BF16 = jnp.bfloat16
F32 = jnp.float32

O_KA, O_VA, O_QB = WIDTH_A, 2 * WIDTH_A, 3 * WIDTH_A
O_KB = O_QB + WIDTH_B
O_VB = O_KB + KV_WIDTH_B


def _rms(x, g):
    return x * lax.rsqrt(jnp.mean(x * x, axis=-1, keepdims=True) + RMS_EPS) * g


def _dot(a, b):
    return jnp.dot(a, b, preferred_element_type=F32)


def _dot_nt(a, b):
    return lax.dot_general(a, b, (((1,), (1,)), ((), ())), preferred_element_type=F32)


def _resident(shape, layer=None):
    nd = len(shape)
    if layer is None:
        return pl.BlockSpec(shape, lambda *_: (0,) * nd, pipeline_mode=pl.Buffered(1))
    return pl.BlockSpec((None,) + tuple(shape), lambda *_: (layer,) + (0,) * nd,
                        pipeline_mode=pl.Buffered(1))


def _rope(x, cos, sdn, sup):
    half = ROT_DIM // 2
    return (x * cos + pltpu.roll(x, PAIR - half, axis=1) * sdn
            + pltpu.roll(x, half, axis=1) * sup)


def _low_lanes(shape):
    return lax.broadcasted_iota(jnp.int32, shape, 1) < HEAD_DIM


def _keep_half(x, half, fill):
    keep = _low_lanes(x.shape) if half == 0 else ~_low_lanes(x.shape)
    return jnp.where(keep, x, jnp.full_like(x, fill))


def _kv_b_variant(x, swapped, g, half, fill):
    return _keep_half(x if g == half else swapped, half, fill).astype(BF16)


def _exp_scores(s_blocks, sink=None):
    if len({s.shape for s in s_blocks}) == 1:
        m = jnp.max(functools.reduce(jnp.maximum, s_blocks), axis=-1, keepdims=True)
    else:
        m = functools.reduce(jnp.maximum, [jnp.max(s, axis=-1, keepdims=True) for s in s_blocks])
    if sink is not None:
        m = jnp.maximum(m, sink)
    p = jnp.concatenate([jnp.exp2(s - m).astype(BF16) for s in s_blocks], axis=-1)
    return p, (None if sink is None else jnp.exp2(sink - m))


def _pv_pair(ps, vs, extras=(None, None)):
    x = [_dot(p, v) for p, v in zip(ps, vs)]
    low = _low_lanes(x[0].shape)
    out = jnp.where(low, x[0], x[1])
    den = pltpu.roll(jnp.where(low, x[1], x[0]), HEAD_DIM, axis=1)
    if extras[0] is not None:
        den = den + jnp.where(low, extras[0], extras[1])
    return out / den


def _rolled_bias_rows(rbv_ref, h, rows):
    x = jnp.broadcast_to(rbv_ref[h:h + 1, :] * LOG2E, (rows, BIAS_PERIOD))
    return pltpu.roll(x, 0, axis=1, stride=1, stride_axis=0)


def _band_valid(rows, col0, cols, prev_chunks):
    r = lax.broadcasted_iota(jnp.int32, (rows, cols), 0) // CHUNK
    c = (lax.broadcasted_iota(jnp.int32, (rows, cols), 1) + col0) // CHUNK
    return (c >= r) & (c <= r + prev_chunks)


TQ = 256
NBLK_A = A_ROWS // TQ + 1
WIN_A = NBLK_A * TQ
WIN_B = WINDOW_B + TQ
NEG_KIND = NBLK_A


def _mix_prompt_kernel(sinks_ref, x_ref, g_ref, w_ref, rope_ref, ct_ref, st_ref, rbv_ref,
                       goa_ref, gob_ref,
                       o_ref, cak_ref, cav_ref, cbk_ref, cbv_ref,
                       ka_scr, va_scr, kb_scr, vb_scr, bias_scr, maskb_scr, *, n_tiles):
    b, t = pl.program_id(0), pl.program_id(1)

    @pl.when((b == 0) & (t == 0))
    def _init():
        ka_scr[...] = jnp.zeros_like(ka_scr)
        va_scr[...] = jnp.zeros_like(va_scr)
        kb_scr[...] = jnp.zeros_like(kb_scr)
        vb_scr[...] = jnp.zeros_like(vb_scr)
        for h in range(N_HEADS_A):
            rows = _rolled_bias_rows(rbv_ref, h, TQ)
            for j in range(NBLK_A):
                valid = _band_valid(TQ, j * TQ, TQ, PREV_CHUNKS_A)
                bias_scr[j, h] = jnp.where(valid, rows[:, j * TQ:(j + 1) * TQ], NEG_INF)
            bias_scr[NEG_KIND, h] = jnp.full((TQ, TQ), NEG_INF, F32)
        band = jnp.where(_band_valid(TQ, 0, WIN_B, PREV_CHUNKS_B), jnp.zeros((TQ, WIN_B), F32), NEG_INF)
        col = lax.broadcasted_iota(jnp.int32, (TQ, WIN_B), 1)
        maskb_scr[0] = band
        maskb_scr[1] = jnp.where(col >= WINDOW_B, band, NEG_INF)

    n = _rms(x_ref[0], g_ref[...]).astype(BF16)
    z = _dot(n, w_ref[...])
    ct, st = ct_ref[pl.ds(t, 1), :], st_ref[pl.ds(t, 1), :]
    cos = ct * rope_ref[0] - st * rope_ref[1]
    sdn = st * rope_ref[2] + ct * rope_ref[3]
    sup = st * rope_ref[4] + ct * rope_ref[5]
    ka, va = z[:, O_KA:O_VA], z[:, O_VA:O_QB]
    kb = _rope(z[:, O_KB:O_VB], cos, sdn, sup)
    vb = z[:, O_VB:]
    qa = (z[:, :O_KA] * QSCALE).astype(BF16)
    qb = [(_rope(z[:, O_QB + j * PAIR:O_QB + (j + 1) * PAIR], cos, sdn, sup) * QSCALE).astype(BF16)
          for j in range(N_PAIRS_B)]

    @pl.when(t >= n_tiles - A_ROWS // TQ)
    def _():
        cak_ref[0] = ka
        cav_ref[0] = va

    @pl.when(t == n_tiles - 1)
    def _():
        cbk_ref[0] = kb[TQ - WINDOW_B:, :]
        cbv_ref[0] = vb[TQ - WINDOW_B:, :]

    slot = lax.rem(t, NBLK_A)
    row0 = pl.multiple_of(slot * TQ, TQ)
    for p in range(N_PAIRS_A):
        for half in range(2):
            sl = slice(p * PAIR, (p + 1) * PAIR)
            ka_scr[p, pl.ds(half * WIN_A + row0, TQ), :] = _keep_half(ka[:, sl], half, 0.0).astype(BF16)
            va_scr[p, half, pl.ds(row0, TQ), :] = _keep_half(va[:, sl], half, 1.0).astype(BF16)
    kinds = []
    for ps in range(NBLK_A):
        d = lax.rem(t - ps + NBLK_A, NBLK_A)
        kinds.append(jnp.where(t >= d, NBLK_A - 1 - d, NEG_KIND))
    oa = []
    for p in range(N_PAIRS_A):
        s = _dot_nt(qa[:, p * PAIR:(p + 1) * PAIR], ka_scr[p])
        probs = []
        for half in range(2):
            s_blocks = [s[:, half * WIN_A + ps * TQ:half * WIN_A + (ps + 1) * TQ]
                        + bias_scr[kinds[ps], 2 * p + half] for ps in range(NBLK_A)]
            probs.append(_exp_scores(s_blocks)[0])
        oa.append(_pv_pair(probs, [va_scr[p, 0], va_scr[p, 1]]))
    oa = jnp.concatenate(oa, axis=-1)

    kb_sw, vb_sw = pltpu.roll(kb, HEAD_DIM, axis=1), pltpu.roll(vb, HEAD_DIM, axis=1)
    for g in range(N_KV_B):
        for half in range(2):
            kb_scr[g, pl.ds(half * WIN_B + WINDOW_B, TQ), :] = _kv_b_variant(kb, kb_sw, g, half, 0.0)
            vb_scr[g, half, pl.ds(WINDOW_B, TQ), :] = _kv_b_variant(vb, vb_sw, g, half, 1.0)
    maskb = maskb_scr[jnp.where(t >= 1, 0, 1)]
    ob = [None] * N_PAIRS_B
    for g in range(N_KV_B):
        pairs = range(g * PAIRS_PER_KV, (g + 1) * PAIRS_PER_KV)
        s = _dot_nt(jnp.concatenate([qb[p] for p in pairs], axis=0), kb_scr[g])
        for i, p in enumerate(pairs):
            probs, extras = [], []
            for half in range(2):
                sb = s[i * TQ:(i + 1) * TQ, half * WIN_B:(half + 1) * WIN_B] + maskb
                pr, ex = _exp_scores([sb], sinks_ref[2 * p + half] * LOG2E)
                probs.append(pr)
                extras.append(ex)
            ob[p] = _pv_pair(probs, [vb_scr[g, 0], vb_scr[g, 1]], extras)
    ob = jnp.concatenate(ob, axis=-1)
    for g in range(N_KV_B):
        for half in range(2):
            kb_scr[g, pl.ds(half * WIN_B, WINDOW_B), :] = kb_scr[g, pl.ds(half * WIN_B + TQ, WINDOW_B), :]
            vb_scr[g, half, pl.ds(0, WINDOW_B), :] = vb_scr[g, half, pl.ds(TQ, WINDOW_B), :]

    o_ref[0, :, :WIDTH_A] = _rms(oa, goa_ref[...]).astype(BF16)
    o_ref[0, :, WIDTH_A:] = _rms(ob, gob_ref[...]).astype(BF16)


def _mix_prompt(sinks, x, g, w_bf16, rope_rows, ct, st, rbv, goa, gob, *, layer):
    b, s, _ = x.shape
    nt = s // TQ
    cur = lambda bi, t: (bi, t, 0)
    keep_a = lambda bi, t: (bi, jnp.maximum(t - (nt - A_ROWS // TQ), 0), 0)
    seq = lambda bi, t: (bi, 0, 0)
    return pl.pallas_call(
        functools.partial(_mix_prompt_kernel, n_tiles=nt),
        grid=(b, nt),
        in_specs=[
            pl.BlockSpec(memory_space=pltpu.SMEM),
            pl.BlockSpec((1, TQ, D_MODEL), cur),
            _resident((1, D_MODEL)),
            _resident((D_MODEL, IN_WIDTH), layer),
            _resident(rope_rows.shape),
            _resident(ct.shape),
            _resident(st.shape),
            _resident((N_HEADS_A, BIAS_PERIOD)),
            _resident((1, WIDTH_A)),
            _resident((1, WIDTH_B)),
        ],
        out_specs=[
            pl.BlockSpec((1, TQ, MIX_WIDTH), cur),
            pl.BlockSpec((1, TQ, WIDTH_A), keep_a),
            pl.BlockSpec((1, TQ, WIDTH_A), keep_a),
            pl.BlockSpec((1, WINDOW_B, KV_WIDTH_B), seq),
            pl.BlockSpec((1, WINDOW_B, KV_WIDTH_B), seq),
        ],
        out_shape=[
            jax.ShapeDtypeStruct((b, s, MIX_WIDTH), BF16),
            jax.ShapeDtypeStruct((b, A_ROWS, WIDTH_A), F32),
            jax.ShapeDtypeStruct((b, A_ROWS, WIDTH_A), F32),
            jax.ShapeDtypeStruct((b, WINDOW_B, KV_WIDTH_B), F32),
            jax.ShapeDtypeStruct((b, WINDOW_B, KV_WIDTH_B), F32),
        ],
        scratch_shapes=[
            pltpu.VMEM((N_PAIRS_A, 2 * WIN_A, PAIR), BF16),
            pltpu.VMEM((N_PAIRS_A, 2, WIN_A, PAIR), BF16),
            pltpu.VMEM((N_KV_B, 2 * WIN_B, PAIR), BF16),
            pltpu.VMEM((N_KV_B, 2, WIN_B, PAIR), BF16),
            pltpu.VMEM((NBLK_A + 1, N_HEADS_A, TQ, TQ), F32),
            pltpu.VMEM((2, TQ, WIN_B), F32),
        ],
        compiler_params=pltpu.CompilerParams(
            dimension_semantics=("arbitrary", "arbitrary"),
            vmem_limit_bytes=V7X_VMEM_LIMIT_BYTES),
        name="mix_prompt",
    )(sinks, x, g, w_bf16, rope_rows, ct, st, rbv, goa, gob)


def _in_proj_kernel(x_ref, g_ref, w_ref, rope_ref, qa_ref, ka_ref, va_ref, qb_ref, kb_ref, vb_ref):
    n = _rms(x_ref[...], g_ref[...]).astype(BF16)
    z = _dot(n, w_ref[...])
    cos, sdn, sup = rope_ref[0], rope_ref[1], rope_ref[2]
    qa_ref[...] = (z[:, :O_KA] * QSCALE).astype(BF16)
    ka_ref[...] = z[:, O_KA:O_VA]
    va_ref[...] = z[:, O_VA:O_QB]
    for j in range(N_PAIRS_B):
        sl = slice(O_QB + j * PAIR, O_QB + (j + 1) * PAIR)
        qb_ref[:, j * PAIR:(j + 1) * PAIR] = (_rope(z[:, sl], cos, sdn, sup) * QSCALE).astype(BF16)
    kb_ref[...] = _rope(z[:, O_KB:O_VB], cos, sdn, sup)
    vb_ref[...] = z[:, O_VB:]


def _in_proj(x, g, w_bf16, rope_tabs, *, tm, layer):
    n_tok = x.shape[0]
    row = lambda i: (i, 0)
    widths = (WIDTH_A, WIDTH_A, WIDTH_A, WIDTH_B, KV_WIDTH_B, KV_WIDTH_B)
    dtypes = (BF16, F32, F32, BF16, F32, F32)
    return pl.pallas_call(
        _in_proj_kernel,
        grid=(n_tok // tm,),
        in_specs=[
            pl.BlockSpec((tm, D_MODEL), row),
            _resident((1, D_MODEL)),
            _resident((D_MODEL, IN_WIDTH), layer),
            _resident(rope_tabs.shape),
        ],
        out_specs=[pl.BlockSpec((tm, w), row) for w in widths],
        out_shape=[jax.ShapeDtypeStruct((n_tok, w), d) for w, d in zip(widths, dtypes)],
        compiler_params=pltpu.CompilerParams(
            dimension_semantics=("arbitrary",), vmem_limit_bytes=V7X_VMEM_LIMIT_BYTES),
        name="in_proj",
    )(x, g, w_bf16, rope_tabs)


def _attn_sample_kernel(sinks_ref, qa_ref, kac_ref, kan_ref, vac_ref, van_ref,
                        qb_ref, kbc_ref, kbn_ref, vbc_ref, vbn_ref,
                        rbv_ref, goa_ref, gob_ref,
                        o_ref, nak_ref, nav_ref, nbk_ref, nbv_ref, bias_scr):
    t_s = qa_ref.shape[0]
    la = kac_ref.shape[1]

    @pl.when(pl.program_id(0) == 0)
    def _init():
        for h in range(N_HEADS_A):
            bias_scr[h] = _rolled_bias_rows(rbv_ref, h, t_s)

    for new_ref, cache_ref, fresh_ref in ((nak_ref, kac_ref, kan_ref), (nav_ref, vac_ref, van_ref),
                                          (nbk_ref, kbc_ref, kbn_ref), (nbv_ref, vbc_ref, vbn_ref)):
        kept = new_ref.shape[1] - t_s
        new_ref[0, :kept, :] = cache_ref[0, cache_ref.shape[1] - kept:, :]
        new_ref[0, kept:, :] = fresh_ref[...]

    oa = []
    for p in range(N_PAIRS_A):
        sl = slice(p * PAIR, (p + 1) * PAIR)
        qp = qa_ref[:, sl]
        kc, kn, vc, vn = kac_ref[0, :, sl], kan_ref[:, sl], vac_ref[0, :, sl], van_ref[:, sl]
        probs, vals = [], []
        for half in range(2):
            h = 2 * p + half
            s_blocks = [_dot_nt(qp, _keep_half(kc, half, 0.0).astype(BF16)) + bias_scr[h, :, :la],
                        _dot_nt(qp, _keep_half(kn, half, 0.0).astype(BF16)) + bias_scr[h, :, la:la + t_s]]
            probs.append(_exp_scores(s_blocks)[0])
            vals.append(jnp.concatenate([_keep_half(vc, half, 1.0), _keep_half(vn, half, 1.0)],
                                        axis=0).astype(BF16))
        oa.append(_pv_pair(probs, vals))
    oa = jnp.concatenate(oa, axis=-1)

    kb = jnp.concatenate([kbc_ref[0], kbn_ref[...]], axis=0)
    vb = jnp.concatenate([vbc_ref[0], vbn_ref[...]], axis=0)
    kb_sw, vb_sw = pltpu.roll(kb, HEAD_DIM, axis=1), pltpu.roll(vb, HEAD_DIM, axis=1)
    ob = []
    for p in range(N_PAIRS_B):
        g = p // PAIRS_PER_KV
        qp = qb_ref[:, p * PAIR:(p + 1) * PAIR]
        probs, extras, vals = [], [], []
        for half in range(2):
            s = _dot_nt(qp, _kv_b_variant(kb, kb_sw, g, half, 0.0))
            pr, ex = _exp_scores([s], sinks_ref[2 * p + half] * LOG2E)
            probs.append(pr)
            extras.append(ex)
            vals.append(_kv_b_variant(vb, vb_sw, g, half, 1.0))
        ob.append(_pv_pair(probs, vals, extras))
    ob = jnp.concatenate(ob, axis=-1)
    o_ref[:, :WIDTH_A] = _rms(oa, goa_ref[...]).astype(BF16)
    o_ref[:, WIDTH_A:] = _rms(ob, gob_ref[...]).astype(BF16)


def _attn_sample(sinks, qa, ka_cache, ka, va_cache, va, qb, kb_cache, kb, vb_cache, vb,
                 rbv, goa, gob, *, t_s, keep_a, keep_b, layer):
    n_tok = qa.shape[0]
    n_seq = n_tok // t_s
    la, lb = ka_cache.shape[2], kb_cache.shape[2]
    assert t_s <= keep_a <= la + t_s and t_s <= keep_b <= lb + t_s
    row = lambda i: (i, 0)
    seq3 = lambda i: (i, 0, 0)
    cache = lambda rows, width: pl.BlockSpec((None, 1, rows, width), lambda i: (layer, i, 0, 0))
    return pl.pallas_call(
        _attn_sample_kernel,
        grid=(n_tok // t_s,),
        in_specs=[
            pl.BlockSpec(memory_space=pltpu.SMEM),
            pl.BlockSpec((t_s, WIDTH_A), row),
            cache(la, WIDTH_A),
            pl.BlockSpec((t_s, WIDTH_A), row),
            cache(la, WIDTH_A),
            pl.BlockSpec((t_s, WIDTH_A), row),
            pl.BlockSpec((t_s, WIDTH_B), row),
            cache(lb, KV_WIDTH_B),
            pl.BlockSpec((t_s, KV_WIDTH_B), row),
            cache(lb, KV_WIDTH_B),
            pl.BlockSpec((t_s, KV_WIDTH_B), row),
            _resident((N_HEADS_A, BIAS_PERIOD)),
            _resident((1, WIDTH_A)),
            _resident((1, WIDTH_B)),
        ],
        out_specs=[
            pl.BlockSpec((t_s, MIX_WIDTH), row),
            pl.BlockSpec((1, keep_a, WIDTH_A), seq3),
            pl.BlockSpec((1, keep_a, WIDTH_A), seq3),
            pl.BlockSpec((1, keep_b, KV_WIDTH_B), seq3),
            pl.BlockSpec((1, keep_b, KV_WIDTH_B), seq3),
        ],
        out_shape=[
            jax.ShapeDtypeStruct((n_tok, MIX_WIDTH), BF16),
            jax.ShapeDtypeStruct((n_seq, keep_a, WIDTH_A), F32),
            jax.ShapeDtypeStruct((n_seq, keep_a, WIDTH_A), F32),
            jax.ShapeDtypeStruct((n_seq, keep_b, KV_WIDTH_B), F32),
            jax.ShapeDtypeStruct((n_seq, keep_b, KV_WIDTH_B), F32),
        ],
        scratch_shapes=[pltpu.VMEM((N_HEADS_A, t_s, BIAS_PERIOD), F32)],
        compiler_params=pltpu.CompilerParams(
            dimension_semantics=("arbitrary",), vmem_limit_bytes=V7X_VMEM_LIMIT_BYTES),
        name="attn_sample",
    )(sinks, qa, ka_cache, ka, va_cache, va, qb, kb_cache, kb, vb_cache, vb, rbv, goa, gob)


def _tail_kernel(h_ref, o_ref, p_ref, wout_ref, gffn_ref, wgu_ref, wdown_ref, wgate_ref,
                 wproj_ref, gfin_ref, out_ref, *, final):
    h = h_ref[...] + _dot(o_ref[...], wout_ref[...])
    gu = _dot(_rms(h, gffn_ref[...]).astype(BF16), wgu_ref[...])
    act = (jax.nn.silu(gu[:, :D_FF]) * gu[:, D_FF:]).astype(BF16)
    h = h + _dot(act, wdown_ref[...])
    gate = jax.nn.sigmoid(_dot(h.astype(BF16), wgate_ref[...]))
    h = h + gate * _dot(p_ref[...].astype(BF16), wproj_ref[...])
    if final:
        h = _rms(h, gfin_ref[...])
    out_ref[...] = h


def _tail(h, o, p, wout, gffn, wgu, wdown, wgate, wproj, gfin, *, tm, final, layer):
    n_tok = h.shape[0]
    row = lambda i: (i, 0)
    return pl.pallas_call(
        functools.partial(_tail_kernel, final=final),
        grid=(n_tok // tm,),
        in_specs=[
            pl.BlockSpec((tm, D_MODEL), row),
            pl.BlockSpec((tm, MIX_WIDTH), row),
            pl.BlockSpec((None, tm, D_PLE), lambda i: (layer, i, 0)),
            _resident((MIX_WIDTH, D_MODEL), layer),
            _resident((1, D_MODEL)),
            _resident((D_MODEL, 2 * D_FF), layer),
            _resident((D_FF, D_MODEL), layer),
            _resident((D_MODEL, D_MODEL), layer),
            _resident((D_PLE, D_MODEL), layer),
            _resident((1, D_MODEL)),
        ],
        out_specs=pl.BlockSpec((tm, D_MODEL), row),
        out_shape=jax.ShapeDtypeStruct((n_tok, D_MODEL), F32),
        compiler_params=pltpu.CompilerParams(
            dimension_semantics=("arbitrary",), vmem_limit_bytes=V7X_VMEM_LIMIT_BYTES),
        name="layer_tail",
    )(h, o, p, wout, gffn, wgu, wdown, wgate, wproj, gfin)


CAST_ROWS = 256


def _cast_kernel(w_ref, o_ref):
    o_ref[...] = w_ref[...].astype(o_ref.dtype)


def _to_bf16(w):
    depth, rows, cols = w.shape
    blk = pl.BlockSpec((1, CAST_ROWS, cols), lambda d, r: (d, r, 0))
    return pl.pallas_call(
        _cast_kernel,
        grid=(depth, rows // CAST_ROWS),
        in_specs=[blk],
        out_specs=blk,
        out_shape=jax.ShapeDtypeStruct(w.shape, BF16),
        compiler_params=pltpu.CompilerParams(dimension_semantics=("arbitrary", "arbitrary")),
        name="cast_bf16",
    )(w)


def _rope_lane_tables(pos):
    half = ROT_DIM // 2
    d = np.arange(PAIR) % HEAD_DIM
    inv = ROPE_THETA ** (-(2.0 * (d % half)) / ROT_DIM)
    ang = np.where(d < ROT_DIM, np.asarray(pos, np.float64)[:, None] * inv[None, :], 0.0)
    lower = (d < half).astype(np.float64)[None, :]
    upper = ((d >= half) & (d < ROT_DIM)).astype(np.float64)[None, :]
    return np.cos(ang), np.sin(ang), lower, upper


def _rope_tables_direct(pos):
    cos, sin, lower, upper = _rope_lane_tables(pos)
    return jnp.asarray(np.stack([cos, -sin * lower, sin * upper]), F32)


def _rope_tables_split(n_tiles, tile):
    cr, sr, lower, upper = _rope_lane_tables(np.arange(tile))
    ct, st, _, _ = _rope_lane_tables(np.arange(n_tiles) * tile)
    rows = np.stack([cr, sr, -cr * lower, -sr * lower, cr * upper, sr * upper])
    return jnp.asarray(rows, F32), jnp.asarray(ct, F32), jnp.asarray(st, F32)


def _rel_bias_row(rel_bias):
    u = np.arange(BIAS_PERIOD)
    diff = np.where(u < BIAS_PERIOD // 2 + A_ROWS // 2, u, u - BIAS_PERIOD)
    idx = np.clip(A_ROWS - diff, -REL_CLIP, REL_CLIP) + REL_CLIP
    return rel_bias.astype(F32)[:, idx]


def kernel(x_prompt, x_sample, p_prompt, p_sample, cache_a_k, cache_a_v, cache_b_k, cache_b_v,
           g_mix_norm, w_in, rel_bias_a, sinks_b, g_out_a, g_out_b, w_out, g_ffn_norm,
           w_gate_up, w_down, w_ple_proj, w_ple_gate, g_final):
    b_p, s_p, _ = x_prompt.shape
    b_s, t_s, _ = x_sample.shape
    depth = w_in.shape[0]
    la_c, lb_c = cache_a_k.shape[2], cache_b_k.shape[2]
    keep_a_s = min(A_ROWS, la_c + t_s)
    keep_b_s = min(WINDOW_B, lb_c + t_s)
    tm_s = 512
    tm_tail = 512
    assert s_p % TQ == 0 and s_p >= A_ROWS and (b_s * t_s) % tm_s == 0 and tm_s % t_s == 0
    assert la_c == A_ROWS and lb_c == WINDOW_B and t_s == CHUNK

    rope_rows, rope_ct, rope_st = _rope_tables_split(s_p // TQ, TQ)
    rope_s = _rope_tables_direct(PAST_LEN + np.arange(tm_s) % t_s)
    row2 = lambda a: a.reshape(1, -1).astype(F32)
    g_fin = row2(g_final)

    hp = x_prompt
    hs = x_sample.reshape(b_s * t_s, D_MODEL)
    outs = [[] for _ in range(8)]
    w_in_b, w_out_b, w_gu_b, w_down_b, w_gate_b, w_proj_b = (
        _to_bf16(w) for w in (w_in, w_out, w_gate_up, w_down, w_ple_gate, w_ple_proj))
    p_prompt_f = p_prompt.reshape(depth, b_p * s_p, D_PLE)
    p_sample_f = p_sample.reshape(depth, b_s * t_s, D_PLE)
    cak_s = cache_a_k.reshape(depth, b_s, la_c, WIDTH_A)
    cav_s = cache_a_v.reshape(depth, b_s, la_c, WIDTH_A)
    cbk_s = cache_b_k.reshape(depth, b_s, lb_c, KV_WIDTH_B)
    cbv_s = cache_b_v.reshape(depth, b_s, lb_c, KV_WIDTH_B)
    for i in range(depth):
        tail_w = (w_out_b, row2(g_ffn_norm[i]), w_gu_b, w_down_b, w_gate_b, w_proj_b, g_fin)
        g_mix = row2(g_mix_norm[i])
        goa, gob = row2(g_out_a[i]), row2(g_out_b[i])
        sinks = sinks_b[i].astype(F32)
        rbv = _rel_bias_row(rel_bias_a[i])
        final = i == depth - 1

        o, cak, cav, cbk, cbv = _mix_prompt(sinks, hp, g_mix, w_in_b, rope_rows, rope_ct, rope_st,
                                            rbv, goa, gob, layer=i)
        hp = _tail(hp.reshape(b_p * s_p, D_MODEL), o.reshape(b_p * s_p, MIX_WIDTH),
                   p_prompt_f, *tail_w, tm=tm_tail, final=final, layer=i)
        hp = hp.reshape(b_p, s_p, D_MODEL)
        outs[0].append(cak.reshape(b_p, A_ROWS, N_HEADS_A, HEAD_DIM))
        outs[1].append(cav.reshape(b_p, A_ROWS, N_HEADS_A, HEAD_DIM))
        outs[2].append(cbk.reshape(b_p, WINDOW_B, N_KV_B, HEAD_DIM))
        outs[3].append(cbv.reshape(b_p, WINDOW_B, N_KV_B, HEAD_DIM))

        qa, ka, va, qb, kb, vb = _in_proj(hs, g_mix, w_in_b, rope_s, tm=tm_s, layer=i)
        o, nak, nav, nbk, nbv = _attn_sample(sinks, qa, cak_s, ka, cav_s, va, qb, cbk_s, kb, cbv_s, vb,
                                             rbv, goa, gob, t_s=t_s, keep_a=keep_a_s, keep_b=keep_b_s,
                                             layer=i)
        hs = _tail(hs, o, p_sample_f, *tail_w, tm=tm_tail, final=final, layer=i)
        outs[4].append(nak.reshape(b_s, keep_a_s, N_HEADS_A, HEAD_DIM))
        outs[5].append(nav.reshape(b_s, keep_a_s, N_HEADS_A, HEAD_DIM))
        outs[6].append(nbk.reshape(b_s, keep_b_s, N_KV_B, HEAD_DIM))
        outs[7].append(nbv.reshape(b_s, keep_b_s, N_KV_B, HEAD_DIM))

    y_sample = hs.reshape(b_s, t_s, D_MODEL)
    return (hp, y_sample) + tuple(jnp.stack(o) for o in outs)
```

```python
import functools
import math

import numpy as np
import jax
import jax.numpy as jnp
from jax import lax
from jax.experimental import pallas as pl
from jax.experimental.pallas import tpu as pltpu

D_MODEL = 1024
CHUNK = 64
HEAD_DIM = 64
N_HEADS_A = 8
N_HEADS_B = 8
N_KV_B = 2
GROUP_B = N_HEADS_B // N_KV_B
WIDTH_A = N_HEADS_A * HEAD_DIM
WIDTH_B = N_HEADS_B * HEAD_DIM
KV_WIDTH_B = N_KV_B * HEAD_DIM
MIX_WIDTH = WIDTH_A + WIDTH_B
IN_WIDTH = 3 * WIDTH_A + WIDTH_B + 2 * KV_WIDTH_B
PREV_CHUNKS_A = 8
A_ROWS = PREV_CHUNKS_A * CHUNK
REL_CLIP = 256
WINDOW_B = 128
PREV_CHUNKS_B = WINDOW_B // CHUNK
ROT_DIM = HEAD_DIM // 4
ROPE_THETA = 500000.0
D_FF = 2816
D_PLE = 256
PAST_LEN = 2048
RMS_EPS = 1e-6
LOG2E = math.log2(math.e)
QSCALE = HEAD_DIM ** -0.5 * LOG2E
PAIR = 2 * HEAD_DIM
N_PAIRS_A = WIDTH_A // PAIR
N_PAIRS_B = WIDTH_B // PAIR
PAIRS_PER_KV = GROUP_B // 2
NEG_INF = float("-inf")
BIAS_PERIOD = 1024

V7X_VMEM_LIMIT_BYTES = 56 * 1024 * 1024

BF16 = jnp.bfloat16
F32 = jnp.float32

O_KA, O_VA, O_QB = WIDTH_A, 2 * WIDTH_A, 3 * WIDTH_A
O_KB = O_QB + WIDTH_B
O_VB = O_KB + KV_WIDTH_B


def _rms(x, g):
    return x * lax.rsqrt(jnp.mean(x * x, axis=-1, keepdims=True) + RMS_EPS) * g


def _dot(a, b):
    return jnp.dot(a, b, preferred_element_type=F32)


def _dot_nt(a, b):
    return lax.dot_general(a, b, (((1,), (1,)), ((), ())), preferred_element_type=F32)


def _resident(shape, layer=None):
    nd = len(shape)
    if layer is None:
        return pl.BlockSpec(shape, lambda *_: (0,) * nd, pipeline_mode=pl.Buffered(1))
    return pl.BlockSpec((None,) + tuple(shape), lambda *_: (layer,) + (0,) * nd,
                        pipeline_mode=pl.Buffered(1))


def _rope(x, cos, sdn, sup):
    half = ROT_DIM // 2
    return (x * cos + pltpu.roll(x, PAIR - half, axis=1) * sdn
            + pltpu.roll(x, half, axis=1) * sup)


def _low_lanes(shape):
    return lax.broadcasted_iota(jnp.int32, shape, 1) < HEAD_DIM


def _keep_half(x, half, fill):
    keep = _low_lanes(x.shape) if half == 0 else ~_low_lanes(x.shape)
    return jnp.where(keep, x, jnp.full_like(x, fill))


def _kv_b_variant(x, swapped, g, half, fill):
    return _keep_half(x if g == half else swapped, half, fill).astype(BF16)


def _exp_scores(s_blocks, sink=None):
    if len({s.shape for s in s_blocks}) == 1:
        m = jnp.max(functools.reduce(jnp.maximum, s_blocks), axis=-1, keepdims=True)
    else:
        m = functools.reduce(jnp.maximum, [jnp.max(s, axis=-1, keepdims=True) for s in s_blocks])
    if sink is not None:
        m = jnp.maximum(m, sink)
    p = jnp.concatenate([jnp.exp2(s - m).astype(BF16) for s in s_blocks], axis=-1)
    return p, (None if sink is None else jnp.exp2(sink - m))


def _pv_pair(ps, vs, extras=(None, None)):
    x = [_dot(p, v) for p, v in zip(ps, vs)]
    low = _low_lanes(x[0].shape)
    out = jnp.where(low, x[0], x[1])
    den = pltpu.roll(jnp.where(low, x[1], x[0]), HEAD_DIM, axis=1)
    if extras[0] is not None:
        den = den + jnp.where(low, extras[0], extras[1])
    return out / den


def _rolled_bias_rows(rbv_ref, h, rows):
    x = jnp.broadcast_to(rbv_ref[h:h + 1, :] * LOG2E, (rows, BIAS_PERIOD))
    return pltpu.roll(x, 0, axis=1, stride=1, stride_axis=0)


def _band_valid(rows, col0, cols, prev_chunks):
    r = lax.broadcasted_iota(jnp.int32, (rows, cols), 0) // CHUNK
    c = (lax.broadcasted_iota(jnp.int32, (rows, cols), 1) + col0) // CHUNK
    return (c >= r) & (c <= r + prev_chunks)


TQ = 256
NBLK_A = A_ROWS // TQ + 1
WIN_A = NBLK_A * TQ
WIN_B = WINDOW_B + TQ
NEG_KIND = NBLK_A


def _mix_prompt_kernel(sinks_ref, x_ref, g_ref, w_ref, rope_ref, ct_ref, st_ref, rbv_ref,
                       goa_ref, gob_ref,
                       o_ref, cak_ref, cav_ref, cbk_ref, cbv_ref,
                       ka_scr, va_scr, kb_scr, vb_scr, bias_scr, maskb_scr, *, n_tiles):
    b, t = pl.program_id(0), pl.program_id(1)

    @pl.when((b == 0) & (t == 0))
    def _init():
        ka_scr[...] = jnp.zeros_like(ka_scr)
        va_scr[...] = jnp.zeros_like(va_scr)
        kb_scr[...] = jnp.zeros_like(kb_scr)
        vb_scr[...] = jnp.zeros_like(vb_scr)
        for h in range(N_HEADS_A):
            rows = _rolled_bias_rows(rbv_ref, h, TQ)
            for j in range(NBLK_A):
                valid = _band_valid(TQ, j * TQ, TQ, PREV_CHUNKS_A)
                bias_scr[j, h] = jnp.where(valid, rows[:, j * TQ:(j + 1) * TQ], NEG_INF)
            bias_scr[NEG_KIND, h] = jnp.full((TQ, TQ), NEG_INF, F32)
        band = jnp.where(_band_valid(TQ, 0, WIN_B, PREV_CHUNKS_B), jnp.zeros((TQ, WIN_B), F32), NEG_INF)
        col = lax.broadcasted_iota(jnp.int32, (TQ, WIN_B), 1)
        maskb_scr[0] = band
        maskb_scr[1] = jnp.where(col >= WINDOW_B, band, NEG_INF)

    n = _rms(x_ref[0], g_ref[...]).astype(BF16)
    z = _dot(n, w_ref[...])
    ct, st = ct_ref[pl.ds(t, 1), :], st_ref[pl.ds(t, 1), :]
    cos = ct * rope_ref[0] - st * rope_ref[1]
    sdn = st * rope_ref[2] + ct * rope_ref[3]
    sup = st * rope_ref[4] + ct * rope_ref[5]
    ka, va = z[:, O_KA:O_VA], z[:, O_VA:O_QB]
    kb = _rope(z[:, O_KB:O_VB], cos, sdn, sup)
    vb = z[:, O_VB:]
    qa = (z[:, :O_KA] * QSCALE).astype(BF16)
    qb = [(_rope(z[:, O_QB + j * PAIR:O_QB + (j + 1) * PAIR], cos, sdn, sup) * QSCALE).astype(BF16)
          for j in range(N_PAIRS_B)]

    @pl.when(t >= n_tiles - A_ROWS // TQ)
    def _():
        cak_ref[0] = ka
        cav_ref[0] = va

    @pl.when(t == n_tiles - 1)
    def _():
        cbk_ref[0] = kb[TQ - WINDOW_B:, :]
        cbv_ref[0] = vb[TQ - WINDOW_B:, :]

    slot = lax.rem(t, NBLK_A)
    row0 = pl.multiple_of(slot * TQ, TQ)
    for p in range(N_PAIRS_A):
        for half in range(2):
            sl = slice(p * PAIR, (p + 1) * PAIR)
            ka_scr[p, pl.ds(half * WIN_A + row0, TQ), :] = _keep_half(ka[:, sl], half, 0.0).astype(BF16)
            va_scr[p, half, pl.ds(row0, TQ), :] = _keep_half(va[:, sl], half, 1.0).astype(BF16)
    kinds = []
    for ps in range(NBLK_A):
        d = lax.rem(t - ps + NBLK_A, NBLK_A)
        kinds.append(jnp.where(t >= d, NBLK_A - 1 - d, NEG_KIND))
    oa = []
    for p in range(N_PAIRS_A):
        s = _dot_nt(qa[:, p * PAIR:(p + 1) * PAIR], ka_scr[p])
        probs = []
        for half in range(2):
            s_blocks = [s[:, half * WIN_A + ps * TQ:half * WIN_A + (ps + 1) * TQ]
                        + bias_scr[kinds[ps], 2 * p + half] for ps in range(NBLK_A)]
            probs.append(_exp_scores(s_blocks)[0])
        oa.append(_pv_pair(probs, [va_scr[p, 0], va_scr[p, 1]]))
    oa = jnp.concatenate(oa, axis=-1)

    kb_sw, vb_sw = pltpu.roll(kb, HEAD_DIM, axis=1), pltpu.roll(vb, HEAD_DIM, axis=1)
    for g in range(N_KV_B):
        for half in range(2):
            kb_scr[g, pl.ds(half * WIN_B + WINDOW_B, TQ), :] = _kv_b_variant(kb, kb_sw, g, half, 0.0)
            vb_scr[g, half, pl.ds(WINDOW_B, TQ), :] = _kv_b_variant(vb, vb_sw, g, half, 1.0)
    maskb = maskb_scr[jnp.where(t >= 1, 0, 1)]
    ob = [None] * N_PAIRS_B
    for g in range(N_KV_B):
        pairs = range(g * PAIRS_PER_KV, (g + 1) * PAIRS_PER_KV)
        s = _dot_nt(jnp.concatenate([qb[p] for p in pairs], axis=0), kb_scr[g])
        for i, p in enumerate(pairs):
            probs, extras = [], []
            for half in range(2):
                sb = s[i * TQ:(i + 1) * TQ, half * WIN_B:(half + 1) * WIN_B] + maskb
                pr, ex = _exp_scores([sb], sinks_ref[2 * p + half] * LOG2E)
                probs.append(pr)
                extras.append(ex)
            ob[p] = _pv_pair(probs, [vb_scr[g, 0], vb_scr[g, 1]], extras)
    ob = jnp.concatenate(ob, axis=-1)
    for g in range(N_KV_B):
        for half in range(2):
            kb_scr[g, pl.ds(half * WIN_B, WINDOW_B), :] = kb_scr[g, pl.ds(half * WIN_B + TQ, WINDOW_B), :]
            vb_scr[g, half, pl.ds(0, WINDOW_B), :] = vb_scr[g, half, pl.ds(TQ, WINDOW_B), :]

    o_ref[0, :, :WIDTH_A] = _rms(oa, goa_ref[...]).astype(BF16)
    o_ref[0, :, WIDTH_A:] = _rms(ob, gob_ref[...]).astype(BF16)


def _mix_prompt(sinks, x, g, w_bf16, rope_rows, ct, st, rbv, goa, gob, *, layer):
    b, s, _ = x.shape
    nt = s // TQ
    cur = lambda bi, t: (bi, t, 0)
    keep_a = lambda bi, t: (bi, jnp.maximum(t - (nt - A_ROWS // TQ), 0), 0)
    seq = lambda bi, t: (bi, 0, 0)
    return pl.pallas_call(
        functools.partial(_mix_prompt_kernel, n_tiles=nt),
        grid=(b, nt),
        in_specs=[
            pl.BlockSpec(memory_space=pltpu.SMEM),
            pl.BlockSpec((1, TQ, D_MODEL), cur),
            _resident((1, D_MODEL)),
            _resident((D_MODEL, IN_WIDTH), layer),
            _resident(rope_rows.shape),
            _resident(ct.shape),
            _resident(st.shape),
            _resident((N_HEADS_A, BIAS_PERIOD)),
            _resident((1, WIDTH_A)),
            _resident((1, WIDTH_B)),
        ],
        out_specs=[
            pl.BlockSpec((1, TQ, MIX_WIDTH), cur),
            pl.BlockSpec((1, TQ, WIDTH_A), keep_a),
            pl.BlockSpec((1, TQ, WIDTH_A), keep_a),
            pl.BlockSpec((1, WINDOW_B, KV_WIDTH_B), seq),
            pl.BlockSpec((1, WINDOW_B, KV_WIDTH_B), seq),
        ],
        out_shape=[
            jax.ShapeDtypeStruct((b, s, MIX_WIDTH), BF16),
            jax.ShapeDtypeStruct((b, A_ROWS, WIDTH_A), F32),
            jax.ShapeDtypeStruct((b, A_ROWS, WIDTH_A), F32),
            jax.ShapeDtypeStruct((b, WINDOW_B, KV_WIDTH_B), F32),
            jax.ShapeDtypeStruct((b, WINDOW_B, KV_WIDTH_B), F32),
        ],
        scratch_shapes=[
            pltpu.VMEM((N_PAIRS_A, 2 * WIN_A, PAIR), BF16),
            pltpu.VMEM((N_PAIRS_A, 2, WIN_A, PAIR), BF16),
            pltpu.VMEM((N_KV_B, 2 * WIN_B, PAIR), BF16),
            pltpu.VMEM((N_KV_B, 2, WIN_B, PAIR), BF16),
            pltpu.VMEM((NBLK_A + 1, N_HEADS_A, TQ, TQ), F32),
            pltpu.VMEM((2, TQ, WIN_B), F32),
        ],
        compiler_params=pltpu.CompilerParams(
            dimension_semantics=("arbitrary", "arbitrary"),
            vmem_limit_bytes=V7X_VMEM_LIMIT_BYTES),
        name="mix_prompt",
    )(sinks, x, g, w_bf16, rope_rows, ct, st, rbv, goa, gob)


def _in_proj_kernel(x_ref, g_ref, w_ref, rope_ref, qa_ref, ka_ref, va_ref, qb_ref, kb_ref, vb_ref):
    n = _rms(x_ref[...], g_ref[...]).astype(BF16)
    z = _dot(n, w_ref[...])
    cos, sdn, sup = rope_ref[0], rope_ref[1], rope_ref[2]
    qa_ref[...] = (z[:, :O_KA] * QSCALE).astype(BF16)
    ka_ref[...] = z[:, O_KA:O_VA]
    va_ref[...] = z[:, O_VA:O_QB]
    for j in range(N_PAIRS_B):
        sl = slice(O_QB + j * PAIR, O_QB + (j + 1) * PAIR)
        qb_ref[:, j * PAIR:(j + 1) * PAIR] = (_rope(z[:, sl], cos, sdn, sup) * QSCALE).astype(BF16)
    kb_ref[...] = _rope(z[:, O_KB:O_VB], cos, sdn, sup)
    vb_ref[...] = z[:, O_VB:]


def _in_proj(x, g, w_bf16, rope_tabs, *, tm, layer):
    n_tok = x.shape[0]
    row = lambda i: (i, 0)
    widths = (WIDTH_A, WIDTH_A, WIDTH_A, WIDTH_B, KV_WIDTH_B, KV_WIDTH_B)
    dtypes = (BF16, F32, F32, BF16, F32, F32)
    return pl.pallas_call(
        _in_proj_kernel,
        grid=(n_tok // tm,),
        in_specs=[
            pl.BlockSpec((tm, D_MODEL), row),
            _resident((1, D_MODEL)),
            _resident((D_MODEL, IN_WIDTH), layer),
            _resident(rope_tabs.shape),
        ],
        out_specs=[pl.BlockSpec((tm, w), row) for w in widths],
        out_shape=[jax.ShapeDtypeStruct((n_tok, w), d) for w, d in zip(widths, dtypes)],
        compiler_params=pltpu.CompilerParams(
            dimension_semantics=("arbitrary",), vmem_limit_bytes=V7X_VMEM_LIMIT_BYTES),
        name="in_proj",
    )(x, g, w_bf16, rope_tabs)


N_SAMPLE_IN = 14
N_ROLL_IN = 8


def _attn_sample_kernel(*refs, n_prev, emit):
    (sinks_ref, qa_ref, kac_ref, kan_ref, vac_ref, van_ref, qb_ref, kbc_ref, kbn_ref, vbc_ref,
     vbn_ref, rbv_ref, goa_ref, gob_ref) = refs[:N_SAMPLE_IN]
    n_in = N_SAMPLE_IN + N_ROLL_IN * n_prev
    o_ref, bias_scr = refs[n_in], refs[-1]
    t_s = qa_ref.shape[0]
    la = kac_ref.shape[1]

    @pl.when(pl.program_id(0) == 0)
    def _init():
        for h in range(N_HEADS_A):
            bias_scr[h] = _rolled_bias_rows(rbv_ref, h, t_s)

    if emit:
        own = (kac_ref, kan_ref, vac_ref, van_ref, kbc_ref, kbn_ref, vbc_ref, vbn_ref)
        layers = [refs[N_SAMPLE_IN + N_ROLL_IN * l:N_SAMPLE_IN + N_ROLL_IN * (l + 1)]
                  for l in range(n_prev)] + [own]
        for l, lr in enumerate(layers):
            for j, new_ref in enumerate(refs[n_in + 1:n_in + 5]):
                cache_ref, fresh_ref = lr[2 * j], lr[2 * j + 1]
                kept = new_ref.shape[2] - t_s
                new_ref[l, 0, :kept, :] = cache_ref[0, cache_ref.shape[1] - kept:, :]
                new_ref[l, 0, kept:, :] = fresh_ref[...]

    oa = []
    for p in range(N_PAIRS_A):
        sl = slice(p * PAIR, (p + 1) * PAIR)
        qp = qa_ref[:, sl]
        kc, kn, vc, vn = kac_ref[0, :, sl], kan_ref[:, sl], vac_ref[0, :, sl], van_ref[:, sl]
        probs, vals = [], []
        for half in range(2):
            h = 2 * p + half
            s_blocks = [_dot_nt(qp, _keep_half(kc, half, 0.0).astype(BF16)) + bias_scr[h, :, :la],
                        _dot_nt(qp, _keep_half(kn, half, 0.0).astype(BF16)) + bias_scr[h, :, la:la + t_s]]
            probs.append(_exp_scores(s_blocks)[0])
            vals.append(jnp.concatenate([_keep_half(vc, half, 1.0), _keep_half(vn, half, 1.0)],
                                        axis=0).astype(BF16))
        oa.append(_pv_pair(probs, vals))
    oa = jnp.concatenate(oa, axis=-1)

    kb = jnp.concatenate([kbc_ref[0], kbn_ref[...]], axis=0)
    vb = jnp.concatenate([vbc_ref[0], vbn_ref[...]], axis=0)
    kb_sw, vb_sw = pltpu.roll(kb, HEAD_DIM, axis=1), pltpu.roll(vb, HEAD_DIM, axis=1)
    ob = []
    for p in range(N_PAIRS_B):
        g = p // PAIRS_PER_KV
        qp = qb_ref[:, p * PAIR:(p + 1) * PAIR]
        probs, extras, vals = [], [], []
        for half in range(2):
            s = _dot_nt(qp, _kv_b_variant(kb, kb_sw, g, half, 0.0))
            pr, ex = _exp_scores([s], sinks_ref[2 * p + half] * LOG2E)
            probs.append(pr)
            extras.append(ex)
            vals.append(_kv_b_variant(vb, vb_sw, g, half, 1.0))
        ob.append(_pv_pair(probs, vals, extras))
    ob = jnp.concatenate(ob, axis=-1)
    o_ref[:, :WIDTH_A] = _rms(oa, goa_ref[...]).astype(BF16)
    o_ref[:, WIDTH_A:] = _rms(ob, gob_ref[...]).astype(BF16)


def _attn_sample(sinks, qa, ka_cache, ka, va_cache, va, qb, kb_cache, kb, vb_cache, vb,
                 rbv, goa, gob, *, t_s, keep_a, keep_b, layer, prev_new, emit):
    n_tok = qa.shape[0]
    n_seq = n_tok // t_s
    la, lb = ka_cache.shape[2], kb_cache.shape[2]
    assert t_s <= keep_a <= la + t_s and t_s <= keep_b <= lb + t_s
    row = lambda i: (i, 0)
    cache = lambda l, rows, width: pl.BlockSpec((None, 1, rows, width), lambda i: (l, i, 0, 0))
    new = lambda width: pl.BlockSpec((t_s, width), row)

    def kv_specs(l):
        return [cache(l, la, WIDTH_A), new(WIDTH_A), cache(l, la, WIDTH_A), new(WIDTH_A),
                cache(l, lb, KV_WIDTH_B), new(KV_WIDTH_B), cache(l, lb, KV_WIDTH_B), new(KV_WIDTH_B)]

    own = kv_specs(layer)
    in_specs = ([pl.BlockSpec(memory_space=pltpu.SMEM), new(WIDTH_A)] + own[:4] + [new(WIDTH_B)] + own[4:]
                + [_resident((N_HEADS_A, BIAS_PERIOD)), _resident((1, WIDTH_A)), _resident((1, WIDTH_B))])
    operands = [sinks, qa, ka_cache, ka, va_cache, va, qb, kb_cache, kb, vb_cache, vb, rbv, goa, gob]
    out_specs = [pl.BlockSpec((t_s, MIX_WIDTH), row)]
    out_shape = [jax.ShapeDtypeStruct((n_tok, MIX_WIDTH), BF16)]
    n_prev = 0
    if emit:
        n_prev = len(prev_new)
        for l, (ka_l, va_l, kb_l, vb_l) in enumerate(prev_new):
            in_specs += kv_specs(l)
            operands += [ka_cache, ka_l, va_cache, va_l, kb_cache, kb_l, vb_cache, vb_l]
        n_layers = n_prev + 1
        for keep, width in ((keep_a, WIDTH_A), (keep_a, WIDTH_A), (keep_b, KV_WIDTH_B), (keep_b, KV_WIDTH_B)):
            out_specs.append(pl.BlockSpec((n_layers, 1, keep, width), lambda i: (0, i, 0, 0)))
            out_shape.append(jax.ShapeDtypeStruct((n_layers, n_seq, keep, width), F32))
    return pl.pallas_call(
        functools.partial(_attn_sample_kernel, n_prev=n_prev, emit=emit),
        grid=(n_seq,),
        in_specs=in_specs,
        out_specs=out_specs,
        out_shape=out_shape,
        scratch_shapes=[pltpu.VMEM((N_HEADS_A, t_s, BIAS_PERIOD), F32)],
        compiler_params=pltpu.CompilerParams(
            dimension_semantics=("arbitrary",), vmem_limit_bytes=V7X_VMEM_LIMIT_BYTES),
        name="attn_sample",
    )(*operands)


def _tail_kernel(h_ref, o_ref, p_ref, wout_ref, gffn_ref, wgu_ref, wdown_ref, wgate_ref,
                 wproj_ref, gfin_ref, out_ref, *, final):
    h = h_ref[...] + _dot(o_ref[...], wout_ref[...])
    gu = _dot(_rms(h, gffn_ref[...]).astype(BF16), wgu_ref[...])
    act = (jax.nn.silu(gu[:, :D_FF]) * gu[:, D_FF:]).astype(BF16)
    h = h + _dot(act, wdown_ref[...])
    gate = jax.nn.sigmoid(_dot(h.astype(BF16), wgate_ref[...]))
    h = h + gate * _dot(p_ref[...].astype(BF16), wproj_ref[...])
    if final:
        h = _rms(h, gfin_ref[...])
    out_ref[...] = h


def _tail(h, o, p, wout, gffn, wgu, wdown, wgate, wproj, gfin, *, tm, final, layer):
    n_tok = h.shape[0]
    row = lambda i: (i, 0)
    return pl.pallas_call(
        functools.partial(_tail_kernel, final=final),
        grid=(n_tok // tm,),
        in_specs=[
            pl.BlockSpec((tm, D_MODEL), row),
            pl.BlockSpec((tm, MIX_WIDTH), row),
            pl.BlockSpec((None, tm, D_PLE), lambda i: (layer, i, 0)),
            _resident((MIX_WIDTH, D_MODEL), layer),
            _resident((1, D_MODEL)),
            _resident((D_MODEL, 2 * D_FF), layer),
            _resident((D_FF, D_MODEL), layer),
            _resident((D_MODEL, D_MODEL), layer),
            _resident((D_PLE, D_MODEL), layer),
            _resident((1, D_MODEL)),
        ],
        out_specs=pl.BlockSpec((tm, D_MODEL), row),
        out_shape=jax.ShapeDtypeStruct((n_tok, D_MODEL), F32),
        compiler_params=pltpu.CompilerParams(
            dimension_semantics=("arbitrary",), vmem_limit_bytes=V7X_VMEM_LIMIT_BYTES),
        name="layer_tail",
    )(h, o, p, wout, gffn, wgu, wdown, wgate, wproj, gfin)


CAST_ROWS = 256


def _cast_kernel(w_ref, o_ref):
    o_ref[...] = w_ref[...].astype(o_ref.dtype)


def _to_bf16(w):
    depth, rows, cols = w.shape
    blk = pl.BlockSpec((1, CAST_ROWS, cols), lambda d, r: (d, r, 0))
    return pl.pallas_call(
        _cast_kernel,
        grid=(depth, rows // CAST_ROWS),
        in_specs=[blk],
        out_specs=blk,
        out_shape=jax.ShapeDtypeStruct(w.shape, BF16),
        compiler_params=pltpu.CompilerParams(dimension_semantics=("arbitrary", "arbitrary")),
        name="cast_bf16",
    )(w)


def _rope_lane_tables(pos):
    half = ROT_DIM // 2
    d = np.arange(PAIR) % HEAD_DIM
    inv = ROPE_THETA ** (-(2.0 * (d % half)) / ROT_DIM)
    ang = np.where(d < ROT_DIM, np.asarray(pos, np.float64)[:, None] * inv[None, :], 0.0)
    lower = (d < half).astype(np.float64)[None, :]
    upper = ((d >= half) & (d < ROT_DIM)).astype(np.float64)[None, :]
    return np.cos(ang), np.sin(ang), lower, upper


def _rope_tables_direct(pos):
    cos, sin, lower, upper = _rope_lane_tables(pos)
    return jnp.asarray(np.stack([cos, -sin * lower, sin * upper]), F32)


def _rope_tables_split(n_tiles, tile):
    cr, sr, lower, upper = _rope_lane_tables(np.arange(tile))
    ct, st, _, _ = _rope_lane_tables(np.arange(n_tiles) * tile)
    rows = np.stack([cr, sr, -cr * lower, -sr * lower, cr * upper, sr * upper])
    return jnp.asarray(rows, F32), jnp.asarray(ct, F32), jnp.asarray(st, F32)


def _rel_bias_row(rel_bias):
    u = np.arange(BIAS_PERIOD)
    diff = np.where(u < BIAS_PERIOD // 2 + A_ROWS // 2, u, u - BIAS_PERIOD)
    idx = np.clip(A_ROWS - diff, -REL_CLIP, REL_CLIP) + REL_CLIP
    return rel_bias.astype(F32)[:, idx]


def kernel(x_prompt, x_sample, p_prompt, p_sample, cache_a_k, cache_a_v, cache_b_k, cache_b_v,
           g_mix_norm, w_in, rel_bias_a, sinks_b, g_out_a, g_out_b, w_out, g_ffn_norm,
           w_gate_up, w_down, w_ple_proj, w_ple_gate, g_final):
    b_p, s_p, _ = x_prompt.shape
    b_s, t_s, _ = x_sample.shape
    depth = w_in.shape[0]
    la_c, lb_c = cache_a_k.shape[2], cache_b_k.shape[2]
    keep_a_s = min(A_ROWS, la_c + t_s)
    keep_b_s = min(WINDOW_B, lb_c + t_s)
    tm_s = 512
    tm_tail = 512
    assert s_p % TQ == 0 and s_p >= A_ROWS and (b_s * t_s) % tm_s == 0 and tm_s % t_s == 0
    assert la_c == A_ROWS and lb_c == WINDOW_B and t_s == CHUNK

    rope_rows, rope_ct, rope_st = _rope_tables_split(s_p // TQ, TQ)
    rope_s = _rope_tables_direct(PAST_LEN + np.arange(tm_s) % t_s)
    row2 = lambda a: a.reshape(1, -1).astype(F32)
    g_fin = row2(g_final)

    hp = x_prompt
    hs = x_sample.reshape(b_s * t_s, D_MODEL)
    outs = [[] for _ in range(4)]
    new_rows = []
    w_in_b, w_out_b, w_gu_b, w_down_b, w_gate_b, w_proj_b = (
        _to_bf16(w) for w in (w_in, w_out, w_gate_up, w_down, w_ple_gate, w_ple_proj))
    p_prompt_f = p_prompt.reshape(depth, b_p * s_p, D_PLE)
    p_sample_f = p_sample.reshape(depth, b_s * t_s, D_PLE)
    cak_s = cache_a_k.reshape(depth, b_s, la_c, WIDTH_A)
    cav_s = cache_a_v.reshape(depth, b_s, la_c, WIDTH_A)
    cbk_s = cache_b_k.reshape(depth, b_s, lb_c, KV_WIDTH_B)
    cbv_s = cache_b_v.reshape(depth, b_s, lb_c, KV_WIDTH_B)
    for i in range(depth):
        tail_w = (w_out_b, row2(g_ffn_norm[i]), w_gu_b, w_down_b, w_gate_b, w_proj_b, g_fin)
        g_mix = row2(g_mix_norm[i])
        goa, gob = row2(g_out_a[i]), row2(g_out_b[i])
        sinks = sinks_b[i].astype(F32)
        rbv = _rel_bias_row(rel_bias_a[i])
        final = i == depth - 1

        o, cak, cav, cbk, cbv = _mix_prompt(sinks, hp, g_mix, w_in_b, rope_rows, rope_ct, rope_st,
                                            rbv, goa, gob, layer=i)
        hp = _tail(hp.reshape(b_p * s_p, D_MODEL), o.reshape(b_p * s_p, MIX_WIDTH),
                   p_prompt_f, *tail_w, tm=tm_tail, final=final, layer=i)
        hp = hp.reshape(b_p, s_p, D_MODEL)
        outs[0].append(cak.reshape(b_p, A_ROWS, N_HEADS_A, HEAD_DIM))
        outs[1].append(cav.reshape(b_p, A_ROWS, N_HEADS_A, HEAD_DIM))
        outs[2].append(cbk.reshape(b_p, WINDOW_B, N_KV_B, HEAD_DIM))
        outs[3].append(cbv.reshape(b_p, WINDOW_B, N_KV_B, HEAD_DIM))

        qa, ka, va, qb, kb, vb = _in_proj(hs, g_mix, w_in_b, rope_s, tm=tm_s, layer=i)
        o, *rolled = _attn_sample(sinks, qa, cak_s, ka, cav_s, va, qb, cbk_s, kb, cbv_s, vb,
                                  rbv, goa, gob, t_s=t_s, keep_a=keep_a_s, keep_b=keep_b_s,
                                  layer=i, prev_new=new_rows, emit=final)
        new_rows.append((ka, va, kb, vb))
        hs = _tail(hs, o, p_sample_f, *tail_w, tm=tm_tail, final=final, layer=i)

    nak, nav, nbk, nbv = rolled
    y_sample = hs.reshape(b_s, t_s, D_MODEL)
    return ((hp, y_sample) + tuple(jnp.stack(o) for o in outs)
            + (nak.reshape(depth, b_s, keep_a_s, N_HEADS_A, HEAD_DIM),
               nav.reshape(depth, b_s, keep_a_s, N_HEADS_A, HEAD_DIM),
               nbk.reshape(depth, b_s, keep_b_s, N_KV_B, HEAD_DIM),
               nbv.reshape(depth, b_s, keep_b_s, N_KV_B, HEAD_DIM)))
```

```python
import functools
import math

import numpy as np
import jax
import jax.numpy as jnp
from jax import lax
from jax.experimental import pallas as pl
from jax.experimental.pallas import tpu as pltpu

D_MODEL = 1024
CHUNK = 64
HEAD_DIM = 64
N_HEADS_A = 8
N_HEADS_B = 8
N_KV_B = 2
GROUP_B = N_HEADS_B // N_KV_B
WIDTH_A = N_HEADS_A * HEAD_DIM
WIDTH_B = N_HEADS_B * HEAD_DIM
KV_WIDTH_B = N_KV_B * HEAD_DIM
MIX_WIDTH = WIDTH_A + WIDTH_B
IN_WIDTH = 3 * WIDTH_A + WIDTH_B + 2 * KV_WIDTH_B
PREV_CHUNKS_A = 8
A_ROWS = PREV_CHUNKS_A * CHUNK
REL_CLIP = 256
WINDOW_B = 128
PREV_CHUNKS_B = WINDOW_B // CHUNK
ROT_DIM = HEAD_DIM // 4
ROPE_THETA = 500000.0
D_FF = 2816
D_PLE = 256
PAST_LEN = 2048
RMS_EPS = 1e-6
LOG2E = math.log2(math.e)
QSCALE = HEAD_DIM ** -0.5 * LOG2E
PAIR = 2 * HEAD_DIM
N_PAIRS_A = WIDTH_A // PAIR
N_PAIRS_B = WIDTH_B // PAIR
PAIRS_PER_KV = GROUP_B // 2
NEG_INF = float("-inf")
BIAS_PERIOD = 1024

V7X_VMEM_LIMIT_BYTES = 56 * 1024 * 1024

BF16 = jnp.bfloat16
F32 = jnp.float32

O_KA, O_VA, O_QB = WIDTH_A, 2 * WIDTH_A, 3 * WIDTH_A
O_KB = O_QB + WIDTH_B
O_VB = O_KB + KV_WIDTH_B


def _rms(x, g):
    return x * lax.rsqrt(jnp.mean(x * x, axis=-1, keepdims=True) + RMS_EPS) * g


def _dot(a, b):
    return jnp.dot(a, b, preferred_element_type=F32)


def _dot_nt(a, b):
    return lax.dot_general(a, b, (((1,), (1,)), ((), ())), preferred_element_type=F32)


def _resident(shape, layer=None):
    nd = len(shape)
    if layer is None:
        return pl.BlockSpec(shape, lambda *_: (0,) * nd, pipeline_mode=pl.Buffered(1))
    return pl.BlockSpec((None,) + tuple(shape), lambda *_: (layer,) + (0,) * nd,
                        pipeline_mode=pl.Buffered(1))


def _rope(x, cos, sdn, sup):
    half = ROT_DIM // 2
    return (x * cos + pltpu.roll(x, PAIR - half, axis=1) * sdn
            + pltpu.roll(x, half, axis=1) * sup)


def _low_lanes(shape):
    return lax.broadcasted_iota(jnp.int32, shape, 1) < HEAD_DIM


def _keep_half(x, half, fill):
    keep = _low_lanes(x.shape) if half == 0 else ~_low_lanes(x.shape)
    return jnp.where(keep, x, jnp.full_like(x, fill))


def _kv_b_variant(x, swapped, g, half, fill):
    return _keep_half(x if g == half else swapped, half, fill).astype(BF16)


def _exp_scores(s_blocks, sink=None):
    if len({s.shape for s in s_blocks}) == 1:
        m = jnp.max(functools.reduce(jnp.maximum, s_blocks), axis=-1, keepdims=True)
    else:
        m = functools.reduce(jnp.maximum, [jnp.max(s, axis=-1, keepdims=True) for s in s_blocks])
    if sink is not None:
        m = jnp.maximum(m, sink)
    p = jnp.concatenate([jnp.exp2(s - m).astype(BF16) for s in s_blocks], axis=-1)
    return p, (None if sink is None else jnp.exp2(sink - m))


def _pv_pair(ps, vs, extras=(None, None)):
    x = [_dot(p, v) for p, v in zip(ps, vs)]
    low = _low_lanes(x[0].shape)
    out = jnp.where(low, x[0], x[1])
    den = pltpu.roll(jnp.where(low, x[1], x[0]), HEAD_DIM, axis=1)
    if extras[0] is not None:
        den = den + jnp.where(low, extras[0], extras[1])
    return out / den


def _pv_stacked(probs, v_ext, extras=None):
    m = probs[0].shape[0]
    x = _dot(jnp.concatenate(probs, axis=0), v_ext)
    outs = []
    for i in range(len(probs)):
        den = x[i * m:(i + 1) * m, PAIR:]
        if extras is not None:
            den = den + extras[i]
        outs.append(x[i * m:(i + 1) * m, :PAIR] / den)
    return outs


def _rolled_bias_rows(rbv_ref, h, rows):
    x = jnp.broadcast_to(rbv_ref[h:h + 1, :] * LOG2E, (rows, BIAS_PERIOD))
    return pltpu.roll(x, 0, axis=1, stride=1, stride_axis=0)


def _band_valid(rows, col0, cols, prev_chunks):
    r = lax.broadcasted_iota(jnp.int32, (rows, cols), 0) // CHUNK
    c = (lax.broadcasted_iota(jnp.int32, (rows, cols), 1) + col0) // CHUNK
    return (c >= r) & (c <= r + prev_chunks)


TQ = 256
NBLK_A = A_ROWS // TQ + 1
WIN_A = NBLK_A * TQ
WIN_B = WINDOW_B + TQ
NEG_KIND = NBLK_A


def _mix_prompt_kernel(sinks_ref, x_ref, g_ref, w_ref, rope_ref, ct_ref, st_ref, rbv_ref,
                       goa_ref, gob_ref,
                       o_ref, cak_ref, cav_ref, cbk_ref, cbv_ref,
                       ka_scr, va_scr, kbp_scr, vbp_scr, bias_scr, maskb_scr, *, n_tiles):
    b, t = pl.program_id(0), pl.program_id(1)

    @pl.when((b == 0) & (t == 0))
    def _init():
        ka_scr[...] = jnp.zeros_like(ka_scr)
        kbp_scr[...] = jnp.zeros_like(kbp_scr)
        for v_scr in (va_scr, vbp_scr):
            v_scr[..., :PAIR] = jnp.zeros(v_scr.shape[:-1] + (PAIR,), BF16)
            v_scr[..., PAIR:] = jnp.ones(v_scr.shape[:-1] + (PAIR,), BF16)
        for h in range(N_HEADS_A):
            rows = _rolled_bias_rows(rbv_ref, h, TQ)
            for j in range(NBLK_A):
                valid = _band_valid(TQ, j * TQ, TQ, PREV_CHUNKS_A)
                bias_scr[j, h] = jnp.where(valid, rows[:, j * TQ:(j + 1) * TQ], NEG_INF)
            bias_scr[NEG_KIND, h] = jnp.full((TQ, TQ), NEG_INF, F32)
        band = jnp.where(_band_valid(TQ, 0, WIN_B, PREV_CHUNKS_B), jnp.zeros((TQ, WIN_B), F32), NEG_INF)
        col = lax.broadcasted_iota(jnp.int32, (TQ, WIN_B), 1)
        maskb_scr[0] = band
        maskb_scr[1] = jnp.where(col >= WINDOW_B, band, NEG_INF)

    n = _rms(x_ref[0], g_ref[...]).astype(BF16)
    z = _dot(n, w_ref[...])
    ct, st = ct_ref[pl.ds(t, 1), :], st_ref[pl.ds(t, 1), :]
    cos = ct * rope_ref[0] - st * rope_ref[1]
    sdn = st * rope_ref[2] + ct * rope_ref[3]
    sup = st * rope_ref[4] + ct * rope_ref[5]
    ka, va = z[:, O_KA:O_VA], z[:, O_VA:O_QB]
    kb = _rope(z[:, O_KB:O_VB], cos, sdn, sup)
    vb = z[:, O_VB:]
    qa = [[_keep_half(z[:, j * PAIR:(j + 1) * PAIR] * QSCALE, half, 0.0).astype(BF16)
           for half in range(2)] for j in range(N_PAIRS_A)]
    qb = []
    for j in range(N_PAIRS_B):
        q = _rope(z[:, O_QB + j * PAIR:O_QB + (j + 1) * PAIR], cos, sdn, sup) * QSCALE
        qb.append([_keep_half(q, half, 0.0).astype(BF16) for half in range(2)])

    @pl.when(t >= n_tiles - A_ROWS // TQ)
    def _():
        cak_ref[0] = ka
        cav_ref[0] = va

    @pl.when(t == n_tiles - 1)
    def _():
        cbk_ref[0] = kb[TQ - WINDOW_B:, :]
        cbv_ref[0] = vb[TQ - WINDOW_B:, :]

    slot = lax.rem(t, NBLK_A)
    row0 = pl.multiple_of(slot * TQ, TQ)
    for p in range(N_PAIRS_A):
        sl = slice(p * PAIR, (p + 1) * PAIR)
        ka_scr[p, pl.ds(row0, TQ), :] = ka[:, sl].astype(BF16)
        va_scr[p, pl.ds(row0, TQ), :PAIR] = va[:, sl].astype(BF16)
    kinds = []
    for ps in range(NBLK_A):
        d = lax.rem(t - ps + NBLK_A, NBLK_A)
        kinds.append(jnp.where(t >= d, NBLK_A - 1 - d, NEG_KIND))
    low = _low_lanes((TQ, PAIR))
    oa = []
    for p in range(N_PAIRS_A):
        s = _dot_nt(jnp.concatenate(qa[p], axis=0), ka_scr[p])
        probs = []
        for half in range(2):
            s_blocks = [s[half * TQ:(half + 1) * TQ, ps * TQ:(ps + 1) * TQ]
                        + bias_scr[kinds[ps], 2 * p + half] for ps in range(NBLK_A)]
            probs.append(_exp_scores(s_blocks)[0])
        o = _pv_stacked(probs, va_scr[p])
        oa.append(jnp.where(low, o[0], o[1]))
    oa = jnp.concatenate(oa, axis=-1)

    kb_sw, vb_sw = pltpu.roll(kb, HEAD_DIM, axis=1), pltpu.roll(vb, HEAD_DIM, axis=1)
    ones = jnp.ones((TQ, PAIR), BF16)
    slot_b = lax.rem(t, 2)
    maskb = maskb_scr[jnp.where(t >= 1, 0, 1)]
    ob_half = [[None, None] for _ in range(N_PAIRS_B)]
    for c, (k_c, v_c) in enumerate(((kb, vb), (kb_sw, vb_sw))):
        k_c, v_c = k_c.astype(BF16), v_c.astype(BF16)
        kbp_scr[c, slot_b] = k_c[TQ - WINDOW_B:, :]
        vbp_scr[c, slot_b, :, :PAIR] = v_c[TQ - WINDOW_B:, :]
        k_win = jnp.concatenate([kbp_scr[c, 1 - slot_b], k_c], axis=0)
        v_win = jnp.concatenate([vbp_scr[c, 1 - slot_b], jnp.concatenate([v_c, ones], axis=1)], axis=0)
        heads = [(p, half) for p in range(N_PAIRS_B) for half in range(2)
                 if (half == p // PAIRS_PER_KV) == (c == 0)]
        s = _dot_nt(jnp.concatenate([qb[p][half] for p, half in heads], axis=0), k_win)
        probs, extras = [], []
        for i, (p, half) in enumerate(heads):
            pr, ex = _exp_scores([s[i * TQ:(i + 1) * TQ, :] + maskb], sinks_ref[2 * p + half] * LOG2E)
            probs.append(pr)
            extras.append(ex)
        for (p, half), o in zip(heads, _pv_stacked(probs, v_win, extras)):
            ob_half[p][half] = o
    ob = jnp.concatenate([jnp.where(low, o[0], o[1]) for o in ob_half], axis=-1)

    o_ref[0, :, :WIDTH_A] = _rms(oa, goa_ref[...]).astype(BF16)
    o_ref[0, :, WIDTH_A:] = _rms(ob, gob_ref[...]).astype(BF16)


def _mix_prompt(sinks, x, g, w_bf16, rope_rows, ct, st, rbv, goa, gob, *, layer):
    b, s, _ = x.shape
    nt = s // TQ
    cur = lambda bi, t: (bi, t, 0)
    keep_a = lambda bi, t: (bi, jnp.maximum(t - (nt - A_ROWS // TQ), 0), 0)
    seq = lambda bi, t: (bi, 0, 0)
    return pl.pallas_call(
        functools.partial(_mix_prompt_kernel, n_tiles=nt),
        grid=(b, nt),
        in_specs=[
            pl.BlockSpec(memory_space=pltpu.SMEM),
            pl.BlockSpec((1, TQ, D_MODEL), cur),
            _resident((1, D_MODEL)),
            _resident((D_MODEL, IN_WIDTH), layer),
            _resident(rope_rows.shape),
            _resident(ct.shape),
            _resident(st.shape),
            _resident((N_HEADS_A, BIAS_PERIOD)),
            _resident((1, WIDTH_A)),
            _resident((1, WIDTH_B)),
        ],
        out_specs=[
            pl.BlockSpec((1, TQ, MIX_WIDTH), cur),
            pl.BlockSpec((1, TQ, WIDTH_A), keep_a),
            pl.BlockSpec((1, TQ, WIDTH_A), keep_a),
            pl.BlockSpec((1, WINDOW_B, KV_WIDTH_B), seq),
            pl.BlockSpec((1, WINDOW_B, KV_WIDTH_B), seq),
        ],
        out_shape=[
            jax.ShapeDtypeStruct((b, s, MIX_WIDTH), BF16),
            jax.ShapeDtypeStruct((b, A_ROWS, WIDTH_A), F32),
            jax.ShapeDtypeStruct((b, A_ROWS, WIDTH_A), F32),
            jax.ShapeDtypeStruct((b, WINDOW_B, KV_WIDTH_B), F32),
            jax.ShapeDtypeStruct((b, WINDOW_B, KV_WIDTH_B), F32),
        ],
        scratch_shapes=[
            pltpu.VMEM((N_PAIRS_A, WIN_A, PAIR), BF16),
            pltpu.VMEM((N_PAIRS_A, WIN_A, 2 * PAIR), BF16),
            pltpu.VMEM((2, 2, WINDOW_B, PAIR), BF16),
            pltpu.VMEM((2, 2, WINDOW_B, 2 * PAIR), BF16),
            pltpu.VMEM((NBLK_A + 1, N_HEADS_A, TQ, TQ), F32),
            pltpu.VMEM((2, TQ, WIN_B), F32),
        ],
        compiler_params=pltpu.CompilerParams(
            dimension_semantics=("arbitrary", "arbitrary"),
            vmem_limit_bytes=V7X_VMEM_LIMIT_BYTES),
        name="mix_prompt",
    )(sinks, x, g, w_bf16, rope_rows, ct, st, rbv, goa, gob)


def _in_proj_kernel(x_ref, g_ref, w_ref, rope_ref, qa_ref, ka_ref, va_ref, qb_ref, kb_ref, vb_ref):
    n = _rms(x_ref[...], g_ref[...]).astype(BF16)
    z = _dot(n, w_ref[...])
    cos, sdn, sup = rope_ref[0], rope_ref[1], rope_ref[2]
    qa_ref[...] = (z[:, :O_KA] * QSCALE).astype(BF16)
    ka_ref[...] = z[:, O_KA:O_VA]
    va_ref[...] = z[:, O_VA:O_QB]
    for j in range(N_PAIRS_B):
        sl = slice(O_QB + j * PAIR, O_QB + (j + 1) * PAIR)
        qb_ref[:, j * PAIR:(j + 1) * PAIR] = (_rope(z[:, sl], cos, sdn, sup) * QSCALE).astype(BF16)
    kb_ref[...] = _rope(z[:, O_KB:O_VB], cos, sdn, sup)
    vb_ref[...] = z[:, O_VB:]


def _in_proj(x, g, w_bf16, rope_tabs, *, tm, layer):
    n_tok = x.shape[0]
    row = lambda i: (i, 0)
    widths = (WIDTH_A, WIDTH_A, WIDTH_A, WIDTH_B, KV_WIDTH_B, KV_WIDTH_B)
    dtypes = (BF16, F32, F32, BF16, F32, F32)
    return pl.pallas_call(
        _in_proj_kernel,
        grid=(n_tok // tm,),
        in_specs=[
            pl.BlockSpec((tm, D_MODEL), row),
            _resident((1, D_MODEL)),
            _resident((D_MODEL, IN_WIDTH), layer),
            _resident(rope_tabs.shape),
        ],
        out_specs=[pl.BlockSpec((tm, w), row) for w in widths],
        out_shape=[jax.ShapeDtypeStruct((n_tok, w), d) for w, d in zip(widths, dtypes)],
        compiler_params=pltpu.CompilerParams(
            dimension_semantics=("arbitrary",), vmem_limit_bytes=V7X_VMEM_LIMIT_BYTES),
        name="in_proj",
    )(x, g, w_bf16, rope_tabs)


N_SAMPLE_IN = 14
N_ROLL_IN = 8


def _attn_sample_kernel(*refs, n_prev, emit):
    (sinks_ref, qa_ref, kac_ref, kan_ref, vac_ref, van_ref, qb_ref, kbc_ref, kbn_ref, vbc_ref,
     vbn_ref, rbv_ref, goa_ref, gob_ref) = refs[:N_SAMPLE_IN]
    n_in = N_SAMPLE_IN + N_ROLL_IN * n_prev
    o_ref, bias_scr = refs[n_in], refs[-1]
    t_s = qa_ref.shape[0]
    la = kac_ref.shape[1]

    @pl.when(pl.program_id(0) == 0)
    def _init():
        for h in range(N_HEADS_A):
            bias_scr[h] = _rolled_bias_rows(rbv_ref, h, t_s)

    if emit:
        own = (kac_ref, kan_ref, vac_ref, van_ref, kbc_ref, kbn_ref, vbc_ref, vbn_ref)
        layers = [refs[N_SAMPLE_IN + N_ROLL_IN * l:N_SAMPLE_IN + N_ROLL_IN * (l + 1)]
                  for l in range(n_prev)] + [own]
        for l, lr in enumerate(layers):
            for j, new_ref in enumerate(refs[n_in + 1:n_in + 5]):
                cache_ref, fresh_ref = lr[2 * j], lr[2 * j + 1]
                kept = new_ref.shape[2] - t_s
                new_ref[l, 0, :kept, :] = cache_ref[0, cache_ref.shape[1] - kept:, :]
                new_ref[l, 0, kept:, :] = fresh_ref[...]

    oa = []
    for p in range(N_PAIRS_A):
        sl = slice(p * PAIR, (p + 1) * PAIR)
        qp = qa_ref[:, sl]
        kc, kn, vc, vn = kac_ref[0, :, sl], kan_ref[:, sl], vac_ref[0, :, sl], van_ref[:, sl]
        probs, vals = [], []
        for half in range(2):
            h = 2 * p + half
            s_blocks = [_dot_nt(qp, _keep_half(kc, half, 0.0).astype(BF16)) + bias_scr[h, :, :la],
                        _dot_nt(qp, _keep_half(kn, half, 0.0).astype(BF16)) + bias_scr[h, :, la:la + t_s]]
            probs.append(_exp_scores(s_blocks)[0])
            vals.append(jnp.concatenate([_keep_half(vc, half, 1.0), _keep_half(vn, half, 1.0)],
                                        axis=0).astype(BF16))
        oa.append(_pv_pair(probs, vals))
    oa = jnp.concatenate(oa, axis=-1)

    kb = jnp.concatenate([kbc_ref[0], kbn_ref[...]], axis=0)
    vb = jnp.concatenate([vbc_ref[0], vbn_ref[...]], axis=0)
    kb_sw, vb_sw = pltpu.roll(kb, HEAD_DIM, axis=1), pltpu.roll(vb, HEAD_DIM, axis=1)
    ob = []
    for p in range(N_PAIRS_B):
        g = p // PAIRS_PER_KV
        qp = qb_ref[:, p * PAIR:(p + 1) * PAIR]
        probs, extras, vals = [], [], []
        for half in range(2):
            s = _dot_nt(qp, _kv_b_variant(kb, kb_sw, g, half, 0.0))
            pr, ex = _exp_scores([s], sinks_ref[2 * p + half] * LOG2E)
            probs.append(pr)
            extras.append(ex)
            vals.append(_kv_b_variant(vb, vb_sw, g, half, 1.0))
        ob.append(_pv_pair(probs, vals, extras))
    ob = jnp.concatenate(ob, axis=-1)
    o_ref[:, :WIDTH_A] = _rms(oa, goa_ref[...]).astype(BF16)
    o_ref[:, WIDTH_A:] = _rms(ob, gob_ref[...]).astype(BF16)


def _attn_sample(sinks, qa, ka_cache, ka, va_cache, va, qb, kb_cache, kb, vb_cache, vb,
                 rbv, goa, gob, *, t_s, keep_a, keep_b, layer, prev_new, emit):
    n_tok = qa.shape[0]
    n_seq = n_tok // t_s
    la, lb = ka_cache.shape[2], kb_cache.shape[2]
    assert t_s <= keep_a <= la + t_s and t_s <= keep_b <= lb + t_s
    row = lambda i: (i, 0)
    cache = lambda l, rows, width: pl.BlockSpec((None, 1, rows, width), lambda i: (l, i, 0, 0))
    new = lambda width: pl.BlockSpec((t_s, width), row)

    def kv_specs(l):
        return [cache(l, la, WIDTH_A), new(WIDTH_A), cache(l, la, WIDTH_A), new(WIDTH_A),
                cache(l, lb, KV_WIDTH_B), new(KV_WIDTH_B), cache(l, lb, KV_WIDTH_B), new(KV_WIDTH_B)]

    own = kv_specs(layer)
    in_specs = ([pl.BlockSpec(memory_space=pltpu.SMEM), new(WIDTH_A)] + own[:4] + [new(WIDTH_B)] + own[4:]
                + [_resident((N_HEADS_A, BIAS_PERIOD)), _resident((1, WIDTH_A)), _resident((1, WIDTH_B))])
    operands = [sinks, qa, ka_cache, ka, va_cache, va, qb, kb_cache, kb, vb_cache, vb, rbv, goa, gob]
    out_specs = [pl.BlockSpec((t_s, MIX_WIDTH), row)]
    out_shape = [jax.ShapeDtypeStruct((n_tok, MIX_WIDTH), BF16)]
    n_prev = 0
    if emit:
        n_prev = len(prev_new)
        for l, (ka_l, va_l, kb_l, vb_l) in enumerate(prev_new):
            in_specs += kv_specs(l)
            operands += [ka_cache, ka_l, va_cache, va_l, kb_cache, kb_l, vb_cache, vb_l]
        n_layers = n_prev + 1
        for keep, width in ((keep_a, WIDTH_A), (keep_a, WIDTH_A), (keep_b, KV_WIDTH_B), (keep_b, KV_WIDTH_B)):
            out_specs.append(pl.BlockSpec((n_layers, 1, keep, width), lambda i: (0, i, 0, 0)))
            out_shape.append(jax.ShapeDtypeStruct((n_layers, n_seq, keep, width), F32))
    return pl.pallas_call(
        functools.partial(_attn_sample_kernel, n_prev=n_prev, emit=emit),
        grid=(n_seq,),
        in_specs=in_specs,
        out_specs=out_specs,
        out_shape=out_shape,
        scratch_shapes=[pltpu.VMEM((N_HEADS_A, t_s, BIAS_PERIOD), F32)],
        compiler_params=pltpu.CompilerParams(
            dimension_semantics=("arbitrary",), vmem_limit_bytes=V7X_VMEM_LIMIT_BYTES),
        name="attn_sample",
    )(*operands)


def _tail_kernel(h_ref, o_ref, p_ref, wout_ref, gffn_ref, wgu_ref, wdown_ref, wgate_ref,
                 wproj_ref, gfin_ref, out_ref, *, final):
    h = h_ref[...] + _dot(o_ref[...], wout_ref[...])
    gu = _dot(_rms(h, gffn_ref[...]).astype(BF16), wgu_ref[...])
    act = (jax.nn.silu(gu[:, :D_FF]) * gu[:, D_FF:]).astype(BF16)
    h = h + _dot(act, wdown_ref[...])
    gate = jax.nn.sigmoid(_dot(h.astype(BF16), wgate_ref[...]))
    h = h + gate * _dot(p_ref[...].astype(BF16), wproj_ref[...])
    if final:
        h = _rms(h, gfin_ref[...])
    out_ref[...] = h


def _tail(h, o, p, wout, gffn, wgu, wdown, wgate, wproj, gfin, *, tm, final, layer):
    n_tok = h.shape[0]
    row = lambda i: (i, 0)
    return pl.pallas_call(
        functools.partial(_tail_kernel, final=final),
        grid=(n_tok // tm,),
        in_specs=[
            pl.BlockSpec((tm, D_MODEL), row),
            pl.BlockSpec((tm, MIX_WIDTH), row),
            pl.BlockSpec((None, tm, D_PLE), lambda i: (layer, i, 0)),
            _resident((MIX_WIDTH, D_MODEL), layer),
            _resident((1, D_MODEL)),
            _resident((D_MODEL, 2 * D_FF), layer),
            _resident((D_FF, D_MODEL), layer),
            _resident((D_MODEL, D_MODEL), layer),
            _resident((D_PLE, D_MODEL), layer),
            _resident((1, D_MODEL)),
        ],
        out_specs=pl.BlockSpec((tm, D_MODEL), row),
        out_shape=jax.ShapeDtypeStruct((n_tok, D_MODEL), F32),
        compiler_params=pltpu.CompilerParams(
            dimension_semantics=("arbitrary",), vmem_limit_bytes=V7X_VMEM_LIMIT_BYTES),
        name="layer_tail",
    )(h, o, p, wout, gffn, wgu, wdown, wgate, wproj, gfin)


CAST_ROWS = 256


def _cast_kernel(w_ref, o_ref):
    o_ref[...] = w_ref[...].astype(o_ref.dtype)


def _to_bf16(w):
    depth, rows, cols = w.shape
    blk = pl.BlockSpec((1, CAST_ROWS, cols), lambda d, r: (d, r, 0))
    return pl.pallas_call(
        _cast_kernel,
        grid=(depth, rows // CAST_ROWS),
        in_specs=[blk],
        out_specs=blk,
        out_shape=jax.ShapeDtypeStruct(w.shape, BF16),
        compiler_params=pltpu.CompilerParams(dimension_semantics=("arbitrary", "arbitrary")),
        name="cast_bf16",
    )(w)


def _rope_lane_tables(pos):
    half = ROT_DIM // 2
    d = np.arange(PAIR) % HEAD_DIM
    inv = ROPE_THETA ** (-(2.0 * (d % half)) / ROT_DIM)
    ang = np.where(d < ROT_DIM, np.asarray(pos, np.float64)[:, None] * inv[None, :], 0.0)
    lower = (d < half).astype(np.float64)[None, :]
    upper = ((d >= half) & (d < ROT_DIM)).astype(np.float64)[None, :]
    return np.cos(ang), np.sin(ang), lower, upper


def _rope_tables_direct(pos):
    cos, sin, lower, upper = _rope_lane_tables(pos)
    return jnp.asarray(np.stack([cos, -sin * lower, sin * upper]), F32)


def _rope_tables_split(n_tiles, tile):
    cr, sr, lower, upper = _rope_lane_tables(np.arange(tile))
    ct, st, _, _ = _rope_lane_tables(np.arange(n_tiles) * tile)
    rows = np.stack([cr, sr, -cr * lower, -sr * lower, cr * upper, sr * upper])
    return jnp.asarray(rows, F32), jnp.asarray(ct, F32), jnp.asarray(st, F32)


def _rel_bias_row(rel_bias):
    u = np.arange(BIAS_PERIOD)
    diff = np.where(u < BIAS_PERIOD // 2 + A_ROWS // 2, u, u - BIAS_PERIOD)
    idx = np.clip(A_ROWS - diff, -REL_CLIP, REL_CLIP) + REL_CLIP
    return rel_bias.astype(F32)[:, idx]


def kernel(x_prompt, x_sample, p_prompt, p_sample, cache_a_k, cache_a_v, cache_b_k, cache_b_v,
           g_mix_norm, w_in, rel_bias_a, sinks_b, g_out_a, g_out_b, w_out, g_ffn_norm,
           w_gate_up, w_down, w_ple_proj, w_ple_gate, g_final):
    b_p, s_p, _ = x_prompt.shape
    b_s, t_s, _ = x_sample.shape
    depth = w_in.shape[0]
    la_c, lb_c = cache_a_k.shape[2], cache_b_k.shape[2]
    keep_a_s = min(A_ROWS, la_c + t_s)
    keep_b_s = min(WINDOW_B, lb_c + t_s)
    tm_s = 512
    tm_tail = 512
    assert s_p % TQ == 0 and s_p >= A_ROWS and (b_s * t_s) % tm_s == 0 and tm_s % t_s == 0
    assert la_c == A_ROWS and lb_c == WINDOW_B and t_s == CHUNK

    rope_rows, rope_ct, rope_st = _rope_tables_split(s_p // TQ, TQ)
    rope_s = _rope_tables_direct(PAST_LEN + np.arange(tm_s) % t_s)
    row2 = lambda a: a.reshape(1, -1).astype(F32)
    g_fin = row2(g_final)

    hp = x_prompt
    hs = x_sample.reshape(b_s * t_s, D_MODEL)
    outs = [[] for _ in range(4)]
    new_rows = []
    w_in_b, w_out_b, w_gu_b, w_down_b, w_gate_b, w_proj_b = (
        _to_bf16(w) for w in (w_in, w_out, w_gate_up, w_down, w_ple_gate, w_ple_proj))
    p_prompt_f = p_prompt.reshape(depth, b_p * s_p, D_PLE)
    p_sample_f = p_sample.reshape(depth, b_s * t_s, D_PLE)
    cak_s = cache_a_k.reshape(depth, b_s, la_c, WIDTH_A)
    cav_s = cache_a_v.reshape(depth, b_s, la_c, WIDTH_A)
    cbk_s = cache_b_k.reshape(depth, b_s, lb_c, KV_WIDTH_B)
    cbv_s = cache_b_v.reshape(depth, b_s, lb_c, KV_WIDTH_B)
    for i in range(depth):
        tail_w = (w_out_b, row2(g_ffn_norm[i]), w_gu_b, w_down_b, w_gate_b, w_proj_b, g_fin)
        g_mix = row2(g_mix_norm[i])
        goa, gob = row2(g_out_a[i]), row2(g_out_b[i])
        sinks = sinks_b[i].astype(F32)
        rbv = _rel_bias_row(rel_bias_a[i])
        final = i == depth - 1

        o, cak, cav, cbk, cbv = _mix_prompt(sinks, hp, g_mix, w_in_b, rope_rows, rope_ct, rope_st,
                                            rbv, goa, gob, layer=i)
        hp = _tail(hp.reshape(b_p * s_p, D_MODEL), o.reshape(b_p * s_p, MIX_WIDTH),
                   p_prompt_f, *tail_w, tm=tm_tail, final=final, layer=i)
        hp = hp.reshape(b_p, s_p, D_MODEL)
        outs[0].append(cak.reshape(b_p, A_ROWS, N_HEADS_A, HEAD_DIM))
        outs[1].append(cav.reshape(b_p, A_ROWS, N_HEADS_A, HEAD_DIM))
        outs[2].append(cbk.reshape(b_p, WINDOW_B, N_KV_B, HEAD_DIM))
        outs[3].append(cbv.reshape(b_p, WINDOW_B, N_KV_B, HEAD_DIM))

        qa, ka, va, qb, kb, vb = _in_proj(hs, g_mix, w_in_b, rope_s, tm=tm_s, layer=i)
        o, *rolled = _attn_sample(sinks, qa, cak_s, ka, cav_s, va, qb, cbk_s, kb, cbv_s, vb,
                                  rbv, goa, gob, t_s=t_s, keep_a=keep_a_s, keep_b=keep_b_s,
                                  layer=i, prev_new=new_rows, emit=final)
        new_rows.append((ka, va, kb, vb))
        hs = _tail(hs, o, p_sample_f, *tail_w, tm=tm_tail, final=final, layer=i)

    nak, nav, nbk, nbv = rolled
    y_sample = hs.reshape(b_s, t_s, D_MODEL)
    return ((hp, y_sample) + tuple(jnp.stack(o) for o in outs)
            + (nak.reshape(depth, b_s, keep_a_s, N_HEADS_A, HEAD_DIM),
               nav.reshape(depth, b_s, keep_a_s, N_HEADS_A, HEAD_DIM),
               nbk.reshape(depth, b_s, keep_b_s, N_KV_B, HEAD_DIM),
               nbv.reshape(depth, b_s, keep_b_s, N_KV_B, HEAD_DIM)))
```

```python
import functools
import math

import numpy as np
import jax
import jax.numpy as jnp
from jax import lax
from jax.experimental import pallas as pl
from jax.experimental.pallas import tpu as pltpu

D_MODEL = 1024
CHUNK = 64
HEAD_DIM = 64
N_HEADS_A = 8
N_HEADS_B = 8
N_KV_B = 2
GROUP_B = N_HEADS_B // N_KV_B
WIDTH_A = N_HEADS_A * HEAD_DIM
WIDTH_B = N_HEADS_B * HEAD_DIM
KV_WIDTH_B = N_KV_B * HEAD_DIM
MIX_WIDTH = WIDTH_A + WIDTH_B
IN_WIDTH = 3 * WIDTH_A + WIDTH_B + 2 * KV_WIDTH_B
PREV_CHUNKS_A = 8
A_ROWS = PREV_CHUNKS_A * CHUNK
REL_CLIP = 256
WINDOW_B = 128
PREV_CHUNKS_B = WINDOW_B // CHUNK
ROT_DIM = HEAD_DIM // 4
ROPE_THETA = 500000.0
D_FF = 2816
D_PLE = 256
PAST_LEN = 2048
RMS_EPS = 1e-6
LOG2E = math.log2(math.e)
QSCALE = HEAD_DIM ** -0.5 * LOG2E
PAIR = 2 * HEAD_DIM
N_PAIRS_A = WIDTH_A // PAIR
N_PAIRS_B = WIDTH_B // PAIR
PAIRS_PER_KV = GROUP_B // 2
NEG_INF = float("-inf")
BIAS_PERIOD = 1024

V7X_VMEM_LIMIT_BYTES = 56 * 1024 * 1024

BF16 = jnp.bfloat16
F32 = jnp.float32

O_KA, O_VA, O_QB = WIDTH_A, 2 * WIDTH_A, 3 * WIDTH_A
O_KB = O_QB + WIDTH_B
O_VB = O_KB + KV_WIDTH_B


def _rms(x, g):
    return x * lax.rsqrt(jnp.mean(x * x, axis=-1, keepdims=True) + RMS_EPS) * g


def _dot(a, b):
    return jnp.dot(a, b, preferred_element_type=F32)


def _dot_nt(a, b):
    return lax.dot_general(a, b, (((1,), (1,)), ((), ())), preferred_element_type=F32)


def _resident(shape, layer=None):
    nd = len(shape)
    if layer is None:
        return pl.BlockSpec(shape, lambda *_: (0,) * nd, pipeline_mode=pl.Buffered(1))
    return pl.BlockSpec((None,) + tuple(shape), lambda *_: (layer,) + (0,) * nd,
                        pipeline_mode=pl.Buffered(1))


def _rope(x, cos, sdn, sup):
    half = ROT_DIM // 2
    return (x * cos + pltpu.roll(x, PAIR - half, axis=1) * sdn
            + pltpu.roll(x, half, axis=1) * sup)


def _low_lanes(shape):
    return lax.broadcasted_iota(jnp.int32, shape, 1) < HEAD_DIM


def _keep_half(x, half, fill):
    keep = _low_lanes(x.shape) if half == 0 else ~_low_lanes(x.shape)
    return jnp.where(keep, x, jnp.full_like(x, fill))


def _kv_b_variant(x, swapped, g, half, fill):
    return _keep_half(x if g == half else swapped, half, fill).astype(BF16)


def _exp_scores(s_blocks, sink=None):
    if len({s.shape for s in s_blocks}) == 1:
        m = jnp.max(functools.reduce(jnp.maximum, s_blocks), axis=-1, keepdims=True)
    else:
        m = functools.reduce(jnp.maximum, [jnp.max(s, axis=-1, keepdims=True) for s in s_blocks])
    if sink is not None:
        m = jnp.maximum(m, sink)
    p = jnp.concatenate([jnp.exp2(s - m).astype(BF16) for s in s_blocks], axis=-1)
    return p, (None if sink is None else jnp.exp2(sink - m))


def _pv_pair(ps, vs, extras=(None, None)):
    x = [_dot(p, v) for p, v in zip(ps, vs)]
    low = _low_lanes(x[0].shape)
    out = jnp.where(low, x[0], x[1])
    den = pltpu.roll(jnp.where(low, x[1], x[0]), HEAD_DIM, axis=1)
    if extras[0] is not None:
        den = den + jnp.where(low, extras[0], extras[1])
    return out / den


def _pv_stacked(probs, v_ext, extras=None):
    m = probs[0].shape[0]
    x = _dot(jnp.concatenate(probs, axis=0), v_ext)
    outs = []
    for i in range(len(probs)):
        den = x[i * m:(i + 1) * m, PAIR:]
        if extras is not None:
            den = den + extras[i]
        outs.append(x[i * m:(i + 1) * m, :PAIR] / den)
    return outs


def _rolled_bias_rows(rbv_ref, h, rows):
    x = jnp.broadcast_to(rbv_ref[h:h + 1, :] * LOG2E, (rows, BIAS_PERIOD))
    return pltpu.roll(x, 0, axis=1, stride=1, stride_axis=0)


def _band_valid(rows, col0, cols, prev_chunks):
    r = lax.broadcasted_iota(jnp.int32, (rows, cols), 0) // CHUNK
    c = (lax.broadcasted_iota(jnp.int32, (rows, cols), 1) + col0) // CHUNK
    return (c >= r) & (c <= r + prev_chunks)


TQ = 256
NBLK_A = A_ROWS // TQ + 1
WIN_A = NBLK_A * TQ
WIN_B = WINDOW_B + TQ
NEG_KIND = NBLK_A


def _mix_prompt_kernel(sinks_ref, x_ref, g_ref, w_ref, rope_ref, ct_ref, st_ref, rbv_ref,
                       goa_ref, gob_ref,
                       o_ref, cak_ref, cav_ref, cbk_ref, cbv_ref,
                       ka_scr, va_scr, kbp_scr, vbp_scr, bias_scr, maskb_scr, *, n_tiles):
    b, t = pl.program_id(0), pl.program_id(1)

    @pl.when((b == 0) & (t == 0))
    def _init():
        ka_scr[...] = jnp.zeros_like(ka_scr)
        kbp_scr[...] = jnp.zeros_like(kbp_scr)
        for v_scr in (va_scr, vbp_scr):
            v_scr[..., :PAIR] = jnp.zeros(v_scr.shape[:-1] + (PAIR,), BF16)
            v_scr[..., PAIR:] = jnp.ones(v_scr.shape[:-1] + (PAIR,), BF16)
        for h in range(N_HEADS_A):
            rows = _rolled_bias_rows(rbv_ref, h, TQ)
            for j in range(NBLK_A):
                valid = _band_valid(TQ, j * TQ, TQ, PREV_CHUNKS_A)
                bias_scr[j, h] = jnp.where(valid, rows[:, j * TQ:(j + 1) * TQ], NEG_INF)
            bias_scr[NEG_KIND, h] = jnp.full((TQ, TQ), NEG_INF, F32)
        band = jnp.where(_band_valid(TQ, 0, WIN_B, PREV_CHUNKS_B), jnp.zeros((TQ, WIN_B), F32), NEG_INF)
        col = lax.broadcasted_iota(jnp.int32, (TQ, WIN_B), 1)
        maskb_scr[0] = band
        maskb_scr[1] = jnp.where(col >= WINDOW_B, band, NEG_INF)

    n = _rms(x_ref[0], g_ref[...]).astype(BF16)
    z = _dot(n, w_ref[...])
    ct, st = ct_ref[pl.ds(t, 1), :], st_ref[pl.ds(t, 1), :]
    cos = ct * rope_ref[0] - st * rope_ref[1]
    sdn = st * rope_ref[2] + ct * rope_ref[3]
    sup = st * rope_ref[4] + ct * rope_ref[5]
    ka, va = z[:, O_KA:O_VA], z[:, O_VA:O_QB]
    kb = _rope(z[:, O_KB:O_VB], cos, sdn, sup)
    vb = z[:, O_VB:]
    qa = [[_keep_half(z[:, j * PAIR:(j + 1) * PAIR] * QSCALE, half, 0.0).astype(BF16)
           for half in range(2)] for j in range(N_PAIRS_A)]
    qb = []
    for j in range(N_PAIRS_B):
        q = _rope(z[:, O_QB + j * PAIR:O_QB + (j + 1) * PAIR], cos, sdn, sup) * QSCALE
        qb.append([_keep_half(q, half, 0.0).astype(BF16) for half in range(2)])

    @pl.when(t >= n_tiles - A_ROWS // TQ)
    def _():
        cak_ref[0] = ka
        cav_ref[0] = va

    @pl.when(t == n_tiles - 1)
    def _():
        cbk_ref[0] = kb[TQ - WINDOW_B:, :]
        cbv_ref[0] = vb[TQ - WINDOW_B:, :]

    slot = lax.rem(t, NBLK_A)
    row0 = pl.multiple_of(slot * TQ, TQ)
    for p in range(N_PAIRS_A):
        sl = slice(p * PAIR, (p + 1) * PAIR)
        ka_scr[p, pl.ds(row0, TQ), :] = ka[:, sl].astype(BF16)
        va_scr[p, pl.ds(row0, TQ), :PAIR] = va[:, sl].astype(BF16)
    kinds = []
    for ps in range(NBLK_A):
        d = lax.rem(t - ps + NBLK_A, NBLK_A)
        kinds.append(jnp.where(t >= d, NBLK_A - 1 - d, NEG_KIND))
    low = _low_lanes((TQ, PAIR))
    oa = []
    for p in range(N_PAIRS_A):
        s = _dot_nt(jnp.concatenate(qa[p], axis=0), ka_scr[p])
        probs = []
        for half in range(2):
            s_blocks = [s[half * TQ:(half + 1) * TQ, ps * TQ:(ps + 1) * TQ]
                        + bias_scr[kinds[ps], 2 * p + half] for ps in range(NBLK_A)]
            probs.append(_exp_scores(s_blocks)[0])
        o = _pv_stacked(probs, va_scr[p])
        oa.append(jnp.where(low, o[0], o[1]))
    oa = jnp.concatenate(oa, axis=-1)

    kb_sw, vb_sw = pltpu.roll(kb, HEAD_DIM, axis=1), pltpu.roll(vb, HEAD_DIM, axis=1)
    ones = jnp.ones((TQ, PAIR), BF16)
    slot_b = lax.rem(t, 2)
    maskb = maskb_scr[jnp.where(t >= 1, 0, 1)]
    ob_half = [[None, None] for _ in range(N_PAIRS_B)]
    for c, (k_c, v_c) in enumerate(((kb, vb), (kb_sw, vb_sw))):
        k_c, v_c = k_c.astype(BF16), v_c.astype(BF16)
        kbp_scr[c, slot_b] = k_c[TQ - WINDOW_B:, :]
        vbp_scr[c, slot_b, :, :PAIR] = v_c[TQ - WINDOW_B:, :]
        k_win = jnp.concatenate([kbp_scr[c, 1 - slot_b], k_c], axis=0)
        v_win = jnp.concatenate([vbp_scr[c, 1 - slot_b], jnp.concatenate([v_c, ones], axis=1)], axis=0)
        heads = [(p, half) for p in range(N_PAIRS_B) for half in range(2)
                 if (half == p // PAIRS_PER_KV) == (c == 0)]
        s = _dot_nt(jnp.concatenate([qb[p][half] for p, half in heads], axis=0), k_win)
        probs, extras = [], []
        for i, (p, half) in enumerate(heads):
            pr, ex = _exp_scores([s[i * TQ:(i + 1) * TQ, :] + maskb], sinks_ref[2 * p + half] * LOG2E)
            probs.append(pr)
            extras.append(ex)
        for (p, half), o in zip(heads, _pv_stacked(probs, v_win, extras)):
            ob_half[p][half] = o
    ob = jnp.concatenate([jnp.where(low, o[0], o[1]) for o in ob_half], axis=-1)

    o_ref[0, :, :WIDTH_A] = _rms(oa, goa_ref[...]).astype(BF16)
    o_ref[0, :, WIDTH_A:] = _rms(ob, gob_ref[...]).astype(BF16)


def _mix_prompt(sinks, x, g, w_bf16, rope_rows, ct, st, rbv, goa, gob, *, layer):
    b, s, _ = x.shape
    nt = s // TQ
    cur = lambda bi, t: (bi, t, 0)
    keep_a = lambda bi, t: (bi, jnp.maximum(t - (nt - A_ROWS // TQ), 0), 0)
    seq = lambda bi, t: (bi, 0, 0)
    return pl.pallas_call(
        functools.partial(_mix_prompt_kernel, n_tiles=nt),
        grid=(b, nt),
        in_specs=[
            pl.BlockSpec(memory_space=pltpu.SMEM),
            pl.BlockSpec((1, TQ, D_MODEL), cur),
            _resident((1, D_MODEL)),
            _resident((D_MODEL, IN_WIDTH), layer),
            _resident(rope_rows.shape),
            _resident(ct.shape),
            _resident(st.shape),
            _resident((N_HEADS_A, BIAS_PERIOD)),
            _resident((1, WIDTH_A)),
            _resident((1, WIDTH_B)),
        ],
        out_specs=[
            pl.BlockSpec((1, TQ, MIX_WIDTH), cur),
            pl.BlockSpec((1, TQ, WIDTH_A), keep_a),
            pl.BlockSpec((1, TQ, WIDTH_A), keep_a),
            pl.BlockSpec((1, WINDOW_B, KV_WIDTH_B), seq),
            pl.BlockSpec((1, WINDOW_B, KV_WIDTH_B), seq),
        ],
        out_shape=[
            jax.ShapeDtypeStruct((b, s, MIX_WIDTH), BF16),
            jax.ShapeDtypeStruct((b, A_ROWS, WIDTH_A), F32),
            jax.ShapeDtypeStruct((b, A_ROWS, WIDTH_A), F32),
            jax.ShapeDtypeStruct((b, WINDOW_B, KV_WIDTH_B), F32),
            jax.ShapeDtypeStruct((b, WINDOW_B, KV_WIDTH_B), F32),
        ],
        scratch_shapes=[
            pltpu.VMEM((N_PAIRS_A, WIN_A, PAIR), BF16),
            pltpu.VMEM((N_PAIRS_A, WIN_A, 2 * PAIR), BF16),
            pltpu.VMEM((2, 2, WINDOW_B, PAIR), BF16),
            pltpu.VMEM((2, 2, WINDOW_B, 2 * PAIR), BF16),
            pltpu.VMEM((NBLK_A + 1, N_HEADS_A, TQ, TQ), F32),
            pltpu.VMEM((2, TQ, WIN_B), F32),
        ],
        compiler_params=pltpu.CompilerParams(
            dimension_semantics=("arbitrary", "arbitrary"),
            vmem_limit_bytes=V7X_VMEM_LIMIT_BYTES),
        name="mix_prompt",
    )(sinks, x, g, w_bf16, rope_rows, ct, st, rbv, goa, gob)


def _in_proj_kernel(x_ref, g_ref, w_ref, rope_ref, qa_ref, ka_ref, va_ref, qb_ref, kb_ref, vb_ref):
    n = _rms(x_ref[...], g_ref[...]).astype(BF16)
    z = _dot(n, w_ref[...])
    cos, sdn, sup = rope_ref[0], rope_ref[1], rope_ref[2]
    qa_ref[...] = (z[:, :O_KA] * QSCALE).astype(BF16)
    ka_ref[...] = z[:, O_KA:O_VA]
    va_ref[...] = z[:, O_VA:O_QB]
    for j in range(N_PAIRS_B):
        sl = slice(O_QB + j * PAIR, O_QB + (j + 1) * PAIR)
        qb_ref[:, j * PAIR:(j + 1) * PAIR] = (_rope(z[:, sl], cos, sdn, sup) * QSCALE).astype(BF16)
    kb_ref[...] = _rope(z[:, O_KB:O_VB], cos, sdn, sup)
    vb_ref[...] = z[:, O_VB:]


def _in_proj(x, g, w_bf16, rope_tabs, *, tm, layer):
    n_tok = x.shape[0]
    row = lambda i: (i, 0)
    widths = (WIDTH_A, WIDTH_A, WIDTH_A, WIDTH_B, KV_WIDTH_B, KV_WIDTH_B)
    dtypes = (BF16, F32, F32, BF16, F32, F32)
    return pl.pallas_call(
        _in_proj_kernel,
        grid=(n_tok // tm,),
        in_specs=[
            pl.BlockSpec((tm, D_MODEL), row),
            _resident((1, D_MODEL)),
            _resident((D_MODEL, IN_WIDTH), layer),
            _resident(rope_tabs.shape),
        ],
        out_specs=[pl.BlockSpec((tm, w), row) for w in widths],
        out_shape=[jax.ShapeDtypeStruct((n_tok, w), d) for w, d in zip(widths, dtypes)],
        compiler_params=pltpu.CompilerParams(
            dimension_semantics=("arbitrary",), vmem_limit_bytes=V7X_VMEM_LIMIT_BYTES),
        name="in_proj",
    )(x, g, w_bf16, rope_tabs)


N_SAMPLE_IN = 14
N_ROLL_IN = 8


def _attn_sample_kernel(*refs, n_prev, emit):
    (sinks_ref, qa_ref, kac_ref, kan_ref, vac_ref, van_ref, qb_ref, kbc_ref, kbn_ref, vbc_ref,
     vbn_ref, rbv_ref, goa_ref, gob_ref) = refs[:N_SAMPLE_IN]
    n_in = N_SAMPLE_IN + N_ROLL_IN * n_prev
    o_ref, bias_scr = refs[n_in], refs[-1]
    t_s = qa_ref.shape[0]
    la = kac_ref.shape[1]

    @pl.when(pl.program_id(0) == 0)
    def _init():
        for h in range(N_HEADS_A):
            bias_scr[h] = _rolled_bias_rows(rbv_ref, h, t_s)

    if emit:
        own = (kac_ref, kan_ref, vac_ref, van_ref, kbc_ref, kbn_ref, vbc_ref, vbn_ref)
        layers = [refs[N_SAMPLE_IN + N_ROLL_IN * l:N_SAMPLE_IN + N_ROLL_IN * (l + 1)]
                  for l in range(n_prev)] + [own]
        for l, lr in enumerate(layers):
            for j, new_ref in enumerate(refs[n_in + 1:n_in + 5]):
                cache_ref, fresh_ref = lr[2 * j], lr[2 * j + 1]
                kept = new_ref.shape[2] - t_s
                new_ref[l, 0, :kept, :] = cache_ref[0, cache_ref.shape[1] - kept:, :]
                new_ref[l, 0, kept:, :] = fresh_ref[...]

    oa = []
    for p in range(N_PAIRS_A):
        sl = slice(p * PAIR, (p + 1) * PAIR)
        qp = qa_ref[:, sl]
        kc, kn, vc, vn = kac_ref[0, :, sl], kan_ref[:, sl], vac_ref[0, :, sl], van_ref[:, sl]
        probs, vals = [], []
        for half in range(2):
            h = 2 * p + half
            s_blocks = [_dot_nt(qp, _keep_half(kc, half, 0.0).astype(BF16)) + bias_scr[h, :, :la],
                        _dot_nt(qp, _keep_half(kn, half, 0.0).astype(BF16)) + bias_scr[h, :, la:la + t_s]]
            probs.append(_exp_scores(s_blocks)[0])
            vals.append(jnp.concatenate([_keep_half(vc, half, 1.0), _keep_half(vn, half, 1.0)],
                                        axis=0).astype(BF16))
        oa.append(_pv_pair(probs, vals))
    oa = jnp.concatenate(oa, axis=-1)

    kb = jnp.concatenate([kbc_ref[0], kbn_ref[...]], axis=0)
    vb = jnp.concatenate([vbc_ref[0], vbn_ref[...]], axis=0)
    kb_sw, vb_sw = pltpu.roll(kb, HEAD_DIM, axis=1), pltpu.roll(vb, HEAD_DIM, axis=1)
    ob = []
    for p in range(N_PAIRS_B):
        g = p // PAIRS_PER_KV
        qp = qb_ref[:, p * PAIR:(p + 1) * PAIR]
        probs, extras, vals = [], [], []
        for half in range(2):
            s = _dot_nt(qp, _kv_b_variant(kb, kb_sw, g, half, 0.0))
            pr, ex = _exp_scores([s], sinks_ref[2 * p + half] * LOG2E)
            probs.append(pr)
            extras.append(ex)
            vals.append(_kv_b_variant(vb, vb_sw, g, half, 1.0))
        ob.append(_pv_pair(probs, vals, extras))
    ob = jnp.concatenate(ob, axis=-1)
    o_ref[:, :WIDTH_A] = _rms(oa, goa_ref[...]).astype(BF16)
    o_ref[:, WIDTH_A:] = _rms(ob, gob_ref[...]).astype(BF16)


def _attn_sample(sinks, qa, ka_cache, ka, va_cache, va, qb, kb_cache, kb, vb_cache, vb,
                 rbv, goa, gob, *, t_s, keep_a, keep_b, layer, prev_new, emit):
    n_tok = qa.shape[0]
    n_seq = n_tok // t_s
    la, lb = ka_cache.shape[2], kb_cache.shape[2]
    assert t_s <= keep_a <= la + t_s and t_s <= keep_b <= lb + t_s
    row = lambda i: (i, 0)
    cache = lambda l, rows, width: pl.BlockSpec((None, 1, rows, width), lambda i: (l, i, 0, 0))
    new = lambda width: pl.BlockSpec((t_s, width), row)

    def kv_specs(l):
        return [cache(l, la, WIDTH_A), new(WIDTH_A), cache(l, la, WIDTH_A), new(WIDTH_A),
                cache(l, lb, KV_WIDTH_B), new(KV_WIDTH_B), cache(l, lb, KV_WIDTH_B), new(KV_WIDTH_B)]

    own = kv_specs(layer)
    in_specs = ([pl.BlockSpec(memory_space=pltpu.SMEM), new(WIDTH_A)] + own[:4] + [new(WIDTH_B)] + own[4:]
                + [_resident((N_HEADS_A, BIAS_PERIOD)), _resident((1, WIDTH_A)), _resident((1, WIDTH_B))])
    operands = [sinks, qa, ka_cache, ka, va_cache, va, qb, kb_cache, kb, vb_cache, vb, rbv, goa, gob]
    out_specs = [pl.BlockSpec((t_s, MIX_WIDTH), row)]
    out_shape = [jax.ShapeDtypeStruct((n_tok, MIX_WIDTH), BF16)]
    n_prev = 0
    if emit:
        n_prev = len(prev_new)
        for l, (ka_l, va_l, kb_l, vb_l) in enumerate(prev_new):
            in_specs += kv_specs(l)
            operands += [ka_cache, ka_l, va_cache, va_l, kb_cache, kb_l, vb_cache, vb_l]
        n_layers = n_prev + 1
        for keep, width in ((keep_a, WIDTH_A), (keep_a, WIDTH_A), (keep_b, KV_WIDTH_B), (keep_b, KV_WIDTH_B)):
            out_specs.append(pl.BlockSpec((n_layers, 1, keep, width), lambda i: (0, i, 0, 0)))
            out_shape.append(jax.ShapeDtypeStruct((n_layers, n_seq, keep, width), F32))
    return pl.pallas_call(
        functools.partial(_attn_sample_kernel, n_prev=n_prev, emit=emit),
        grid=(n_seq,),
        in_specs=in_specs,
        out_specs=out_specs,
        out_shape=out_shape,
        scratch_shapes=[pltpu.VMEM((N_HEADS_A, t_s, BIAS_PERIOD), F32)],
        compiler_params=pltpu.CompilerParams(
            dimension_semantics=("arbitrary",), vmem_limit_bytes=V7X_VMEM_LIMIT_BYTES),
        name="attn_sample",
    )(*operands)


def _tail_kernel(h_ref, o_ref, p_ref, wout_ref, gffn_ref, wgu_ref, wdown_ref, wgate_ref,
                 wproj_ref, gfin_ref, out_ref, *, final, n_sub):
    rows = h_ref.shape[0] // n_sub
    subs = [pl.ds(i * rows, rows) for i in range(n_sub)]
    h = [h_ref[sl, :] + _dot(o_ref[sl, :], wout_ref[...]) for sl in subs]
    pp = [_dot(p_ref[sl, :].astype(BF16), wproj_ref[...]) for sl in subs]
    gu = [_dot(_rms(x, gffn_ref[...]).astype(BF16), wgu_ref[...]) for x in h]
    act = [(jax.nn.silu(x[:, :D_FF]) * x[:, D_FF:]).astype(BF16) for x in gu]
    h = [x + _dot(a, wdown_ref[...]) for x, a in zip(h, act)]
    gate = [jax.nn.sigmoid(_dot(x.astype(BF16), wgate_ref[...])) for x in h]
    for sl, x, g, e in zip(subs, h, gate, pp):
        x = x + g * e
        if final:
            x = _rms(x, gfin_ref[...])
        out_ref[sl, :] = x


def _tail(h, o, p, wout, gffn, wgu, wdown, wgate, wproj, gfin, *, tm, final, layer):
    n_tok = h.shape[0]
    row = lambda i: (i, 0)
    return pl.pallas_call(
        functools.partial(_tail_kernel, final=final, n_sub=2),
        grid=(n_tok // tm,),
        in_specs=[
            pl.BlockSpec((tm, D_MODEL), row),
            pl.BlockSpec((tm, MIX_WIDTH), row),
            pl.BlockSpec((None, tm, D_PLE), lambda i: (layer, i, 0)),
            _resident((MIX_WIDTH, D_MODEL), layer),
            _resident((1, D_MODEL)),
            _resident((D_MODEL, 2 * D_FF), layer),
            _resident((D_FF, D_MODEL), layer),
            _resident((D_MODEL, D_MODEL), layer),
            _resident((D_PLE, D_MODEL), layer),
            _resident((1, D_MODEL)),
        ],
        out_specs=pl.BlockSpec((tm, D_MODEL), row),
        out_shape=jax.ShapeDtypeStruct((n_tok, D_MODEL), F32),
        compiler_params=pltpu.CompilerParams(
            dimension_semantics=("arbitrary",), vmem_limit_bytes=V7X_VMEM_LIMIT_BYTES),
        name="layer_tail",
    )(h, o, p, wout, gffn, wgu, wdown, wgate, wproj, gfin)


CAST_ROWS = 256


def _cast_kernel(w_ref, o_ref):
    o_ref[...] = w_ref[...].astype(o_ref.dtype)


def _to_bf16(w):
    depth, rows, cols = w.shape
    blk = pl.BlockSpec((1, CAST_ROWS, cols), lambda d, r: (d, r, 0))
    return pl.pallas_call(
        _cast_kernel,
        grid=(depth, rows // CAST_ROWS),
        in_specs=[blk],
        out_specs=blk,
        out_shape=jax.ShapeDtypeStruct(w.shape, BF16),
        compiler_params=pltpu.CompilerParams(dimension_semantics=("arbitrary", "arbitrary")),
        name="cast_bf16",
    )(w)


def _rope_lane_tables(pos):
    half = ROT_DIM // 2
    d = np.arange(PAIR) % HEAD_DIM
    inv = ROPE_THETA ** (-(2.0 * (d % half)) / ROT_DIM)
    ang = np.where(d < ROT_DIM, np.asarray(pos, np.float64)[:, None] * inv[None, :], 0.0)
    lower = (d < half).astype(np.float64)[None, :]
    upper = ((d >= half) & (d < ROT_DIM)).astype(np.float64)[None, :]
    return np.cos(ang), np.sin(ang), lower, upper


def _rope_tables_direct(pos):
    cos, sin, lower, upper = _rope_lane_tables(pos)
    return jnp.asarray(np.stack([cos, -sin * lower, sin * upper]), F32)


def _rope_tables_split(n_tiles, tile):
    cr, sr, lower, upper = _rope_lane_tables(np.arange(tile))
    ct, st, _, _ = _rope_lane_tables(np.arange(n_tiles) * tile)
    rows = np.stack([cr, sr, -cr * lower, -sr * lower, cr * upper, sr * upper])
    return jnp.asarray(rows, F32), jnp.asarray(ct, F32), jnp.asarray(st, F32)


def _rel_bias_row(rel_bias):
    u = np.arange(BIAS_PERIOD)
    diff = np.where(u < BIAS_PERIOD // 2 + A_ROWS // 2, u, u - BIAS_PERIOD)
    idx = np.clip(A_ROWS - diff, -REL_CLIP, REL_CLIP) + REL_CLIP
    return rel_bias.astype(F32)[:, idx]


def kernel(x_prompt, x_sample, p_prompt, p_sample, cache_a_k, cache_a_v, cache_b_k, cache_b_v,
           g_mix_norm, w_in, rel_bias_a, sinks_b, g_out_a, g_out_b, w_out, g_ffn_norm,
           w_gate_up, w_down, w_ple_proj, w_ple_gate, g_final):
    b_p, s_p, _ = x_prompt.shape
    b_s, t_s, _ = x_sample.shape
    depth = w_in.shape[0]
    la_c, lb_c = cache_a_k.shape[2], cache_b_k.shape[2]
    keep_a_s = min(A_ROWS, la_c + t_s)
    keep_b_s = min(WINDOW_B, lb_c + t_s)
    tm_s = 512
    tm_tail = 512
    assert s_p % TQ == 0 and s_p >= A_ROWS and (b_s * t_s) % tm_s == 0 and tm_s % t_s == 0
    assert la_c == A_ROWS and lb_c == WINDOW_B and t_s == CHUNK

    rope_rows, rope_ct, rope_st = _rope_tables_split(s_p // TQ, TQ)
    rope_s = _rope_tables_direct(PAST_LEN + np.arange(tm_s) % t_s)
    row2 = lambda a: a.reshape(1, -1).astype(F32)
    g_fin = row2(g_final)

    hp = x_prompt
    hs = x_sample.reshape(b_s * t_s, D_MODEL)
    outs = [[] for _ in range(4)]
    new_rows = []
    w_in_b, w_out_b, w_gu_b, w_down_b, w_gate_b, w_proj_b = (
        _to_bf16(w) for w in (w_in, w_out, w_gate_up, w_down, w_ple_gate, w_ple_proj))
    p_prompt_f = p_prompt.reshape(depth, b_p * s_p, D_PLE)
    p_sample_f = p_sample.reshape(depth, b_s * t_s, D_PLE)
    cak_s = cache_a_k.reshape(depth, b_s, la_c, WIDTH_A)
    cav_s = cache_a_v.reshape(depth, b_s, la_c, WIDTH_A)
    cbk_s = cache_b_k.reshape(depth, b_s, lb_c, KV_WIDTH_B)
    cbv_s = cache_b_v.reshape(depth, b_s, lb_c, KV_WIDTH_B)
    for i in range(depth):
        tail_w = (w_out_b, row2(g_ffn_norm[i]), w_gu_b, w_down_b, w_gate_b, w_proj_b, g_fin)
        g_mix = row2(g_mix_norm[i])
        goa, gob = row2(g_out_a[i]), row2(g_out_b[i])
        sinks = sinks_b[i].astype(F32)
        rbv = _rel_bias_row(rel_bias_a[i])
        final = i == depth - 1

        o, cak, cav, cbk, cbv = _mix_prompt(sinks, hp, g_mix, w_in_b, rope_rows, rope_ct, rope_st,
                                            rbv, goa, gob, layer=i)
        hp = _tail(hp.reshape(b_p * s_p, D_MODEL), o.reshape(b_p * s_p, MIX_WIDTH),
                   p_prompt_f, *tail_w, tm=tm_tail, final=final, layer=i)
        hp = hp.reshape(b_p, s_p, D_MODEL)
        outs[0].append(cak.reshape(b_p, A_ROWS, N_HEADS_A, HEAD_DIM))
        outs[1].append(cav.reshape(b_p, A_ROWS, N_HEADS_A, HEAD_DIM))
        outs[2].append(cbk.reshape(b_p, WINDOW_B, N_KV_B, HEAD_DIM))
        outs[3].append(cbv.reshape(b_p, WINDOW_B, N_KV_B, HEAD_DIM))

        qa, ka, va, qb, kb, vb = _in_proj(hs, g_mix, w_in_b, rope_s, tm=tm_s, layer=i)
        o, *rolled = _attn_sample(sinks, qa, cak_s, ka, cav_s, va, qb, cbk_s, kb, cbv_s, vb,
                                  rbv, goa, gob, t_s=t_s, keep_a=keep_a_s, keep_b=keep_b_s,
                                  layer=i, prev_new=new_rows, emit=final)
        new_rows.append((ka, va, kb, vb))
        hs = _tail(hs, o, p_sample_f, *tail_w, tm=tm_tail, final=final, layer=i)

    nak, nav, nbk, nbv = rolled
    y_sample = hs.reshape(b_s, t_s, D_MODEL)
    return ((hp, y_sample) + tuple(jnp.stack(o) for o in outs)
            + (nak.reshape(depth, b_s, keep_a_s, N_HEADS_A, HEAD_DIM),
               nav.reshape(depth, b_s, keep_a_s, N_HEADS_A, HEAD_DIM),
               nbk.reshape(depth, b_s, keep_b_s, N_KV_B, HEAD_DIM),
               nbv.reshape(depth, b_s, keep_b_s, N_KV_B, HEAD_DIM)))
```

```python
import functools
import math

import numpy as np
import jax
import jax.numpy as jnp
from jax import lax
from jax.experimental import pallas as pl
from jax.experimental.pallas import tpu as pltpu

D_MODEL = 1024
CHUNK = 64
HEAD_DIM = 64
N_HEADS_A = 8
N_HEADS_B = 8
N_KV_B = 2
GROUP_B = N_HEADS_B // N_KV_B
WIDTH_A = N_HEADS_A * HEAD_DIM
WIDTH_B = N_HEADS_B * HEAD_DIM
KV_WIDTH_B = N_KV_B * HEAD_DIM
MIX_WIDTH = WIDTH_A + WIDTH_B
IN_WIDTH = 3 * WIDTH_A + WIDTH_B + 2 * KV_WIDTH_B
PREV_CHUNKS_A = 8
A_ROWS = PREV_CHUNKS_A * CHUNK
REL_CLIP = 256
WINDOW_B = 128
PREV_CHUNKS_B = WINDOW_B // CHUNK
ROT_DIM = HEAD_DIM // 4
ROPE_THETA = 500000.0
D_FF = 2816
D_PLE = 256
PAST_LEN = 2048
RMS_EPS = 1e-6
LOG2E = math.log2(math.e)
QSCALE = HEAD_DIM ** -0.5 * LOG2E
PAIR = 2 * HEAD_DIM
N_PAIRS_A = WIDTH_A // PAIR
N_PAIRS_B = WIDTH_B // PAIR
PAIRS_PER_KV = GROUP_B // 2
NEG_INF = float("-inf")
BIAS_PERIOD = 1024

V7X_VMEM_LIMIT_BYTES = 56 * 1024 * 1024

BF16 = jnp.bfloat16
F32 = jnp.float32

O_KA, O_VA, O_QB = WIDTH_A, 2 * WIDTH_A, 3 * WIDTH_A
O_KB = O_QB + WIDTH_B
O_VB = O_KB + KV_WIDTH_B


def _rms(x, g):
    return x * lax.rsqrt(jnp.mean(x * x, axis=-1, keepdims=True) + RMS_EPS) * g


def _dot(a, b):
    return jnp.dot(a, b, preferred_element_type=F32)


def _dot_nt(a, b):
    return lax.dot_general(a, b, (((1,), (1,)), ((), ())), preferred_element_type=F32)


def _resident(shape, layer=None):
    nd = len(shape)
    if layer is None:
        return pl.BlockSpec(shape, lambda *_: (0,) * nd, pipeline_mode=pl.Buffered(1))
    return pl.BlockSpec((None,) + tuple(shape), lambda *_: (layer,) + (0,) * nd,
                        pipeline_mode=pl.Buffered(1))


def _rope(x, cos, sdn, sup):
    half = ROT_DIM // 2
    return (x * cos + pltpu.roll(x, PAIR - half, axis=1) * sdn
            + pltpu.roll(x, half, axis=1) * sup)


def _low_lanes(shape):
    return lax.broadcasted_iota(jnp.int32, shape, 1) < HEAD_DIM


def _keep_half(x, half, fill):
    keep = _low_lanes(x.shape) if half == 0 else ~_low_lanes(x.shape)
    return jnp.where(keep, x, jnp.full_like(x, fill))


def _exp_scores(s_blocks, sink=None):
    if len({s.shape for s in s_blocks}) == 1:
        m = jnp.max(functools.reduce(jnp.maximum, s_blocks), axis=-1, keepdims=True)
    else:
        m = functools.reduce(jnp.maximum, [jnp.max(s, axis=-1, keepdims=True) for s in s_blocks])
    if sink is not None:
        m = jnp.maximum(m, sink)
    p = jnp.concatenate([jnp.exp2(s - m).astype(BF16) for s in s_blocks], axis=-1)
    return p, (None if sink is None else jnp.exp2(sink - m))


def _pv_stacked(probs, v_ext, extras=None):
    m = probs[0].shape[0]
    x = _dot(jnp.concatenate(probs, axis=0), v_ext)
    outs = []
    for i in range(len(probs)):
        den = x[i * m:(i + 1) * m, PAIR:]
        if extras is not None:
            den = den + extras[i]
        outs.append(x[i * m:(i + 1) * m, :PAIR] / den)
    return outs


def _rolled_bias_rows(rbv_ref, h, rows):
    x = jnp.broadcast_to(rbv_ref[h:h + 1, :] * LOG2E, (rows, BIAS_PERIOD))
    return pltpu.roll(x, 0, axis=1, stride=1, stride_axis=0)


def _band_valid(rows, col0, cols, prev_chunks):
    r = lax.broadcasted_iota(jnp.int32, (rows, cols), 0) // CHUNK
    c = (lax.broadcasted_iota(jnp.int32, (rows, cols), 1) + col0) // CHUNK
    return (c >= r) & (c <= r + prev_chunks)


TQ = 256
NBLK_A = A_ROWS // TQ + 1
WIN_A = NBLK_A * TQ
WIN_B = WINDOW_B + TQ
NEG_KIND = NBLK_A


def _mix_prompt_kernel(sinks_ref, x_ref, g_ref, w_ref, rope_ref, ct_ref, st_ref, rbv_ref,
                       goa_ref, gob_ref,
                       o_ref, cak_ref, cav_ref, cbk_ref, cbv_ref,
                       ka_scr, va_scr, kbp_scr, vbp_scr, bias_scr, maskb_scr, *, n_tiles):
    b, t = pl.program_id(0), pl.program_id(1)

    @pl.when((b == 0) & (t == 0))
    def _init():
        ka_scr[...] = jnp.zeros_like(ka_scr)
        kbp_scr[...] = jnp.zeros_like(kbp_scr)
        for v_scr in (va_scr, vbp_scr):
            v_scr[..., :PAIR] = jnp.zeros(v_scr.shape[:-1] + (PAIR,), BF16)
            v_scr[..., PAIR:] = jnp.ones(v_scr.shape[:-1] + (PAIR,), BF16)
        for h in range(N_HEADS_A):
            rows = _rolled_bias_rows(rbv_ref, h, TQ)
            for j in range(NBLK_A):
                valid = _band_valid(TQ, j * TQ, TQ, PREV_CHUNKS_A)
                bias_scr[j, h] = jnp.where(valid, rows[:, j * TQ:(j + 1) * TQ], NEG_INF)
            bias_scr[NEG_KIND, h] = jnp.full((TQ, TQ), NEG_INF, F32)
        band = jnp.where(_band_valid(TQ, 0, WIN_B, PREV_CHUNKS_B), jnp.zeros((TQ, WIN_B), F32), NEG_INF)
        col = lax.broadcasted_iota(jnp.int32, (TQ, WIN_B), 1)
        maskb_scr[0] = band
        maskb_scr[1] = jnp.where(col >= WINDOW_B, band, NEG_INF)

    n = _rms(x_ref[0], g_ref[...]).astype(BF16)
    z = _dot(n, w_ref[...])
    ct, st = ct_ref[pl.ds(t, 1), :], st_ref[pl.ds(t, 1), :]
    cos = ct * rope_ref[0] - st * rope_ref[1]
    sdn = st * rope_ref[2] + ct * rope_ref[3]
    sup = st * rope_ref[4] + ct * rope_ref[5]
    ka, va = z[:, O_KA:O_VA], z[:, O_VA:O_QB]
    kb = _rope(z[:, O_KB:O_VB], cos, sdn, sup)
    vb = z[:, O_VB:]
    qa = [[_keep_half(z[:, j * PAIR:(j + 1) * PAIR] * QSCALE, half, 0.0).astype(BF16)
           for half in range(2)] for j in range(N_PAIRS_A)]
    qb = []
    for j in range(N_PAIRS_B):
        q = _rope(z[:, O_QB + j * PAIR:O_QB + (j + 1) * PAIR], cos, sdn, sup) * QSCALE
        qb.append([_keep_half(q, half, 0.0).astype(BF16) for half in range(2)])

    @pl.when(t >= n_tiles - A_ROWS // TQ)
    def _():
        cak_ref[0] = ka
        cav_ref[0] = va

    @pl.when(t == n_tiles - 1)
    def _():
        cbk_ref[0] = kb[TQ - WINDOW_B:, :]
        cbv_ref[0] = vb[TQ - WINDOW_B:, :]

    slot = lax.rem(t, NBLK_A)
    row0 = pl.multiple_of(slot * TQ, TQ)
    for p in range(N_PAIRS_A):
        sl = slice(p * PAIR, (p + 1) * PAIR)
        ka_scr[p, pl.ds(row0, TQ), :] = ka[:, sl].astype(BF16)
        va_scr[p, pl.ds(row0, TQ), :PAIR] = va[:, sl].astype(BF16)
    kinds = []
    for ps in range(NBLK_A):
        d = lax.rem(t - ps + NBLK_A, NBLK_A)
        kinds.append(jnp.where(t >= d, NBLK_A - 1 - d, NEG_KIND))
    low = _low_lanes((TQ, PAIR))
    oa = []
    for p in range(N_PAIRS_A):
        s = _dot_nt(jnp.concatenate(qa[p], axis=0), ka_scr[p])
        probs = []
        for half in range(2):
            s_blocks = [s[half * TQ:(half + 1) * TQ, ps * TQ:(ps + 1) * TQ]
                        + bias_scr[kinds[ps], 2 * p + half] for ps in range(NBLK_A)]
            probs.append(_exp_scores(s_blocks)[0])
        o = _pv_stacked(probs, va_scr[p])
        oa.append(jnp.where(low, o[0], o[1]))
    oa = jnp.concatenate(oa, axis=-1)

    kb_sw, vb_sw = pltpu.roll(kb, HEAD_DIM, axis=1), pltpu.roll(vb, HEAD_DIM, axis=1)
    ones = jnp.ones((TQ, PAIR), BF16)
    slot_b = lax.rem(t, 2)
    maskb = maskb_scr[jnp.where(t >= 1, 0, 1)]
    ob_half = [[None, None] for _ in range(N_PAIRS_B)]
    for c, (k_c, v_c) in enumerate(((kb, vb), (kb_sw, vb_sw))):
        k_c, v_c = k_c.astype(BF16), v_c.astype(BF16)
        kbp_scr[c, slot_b] = k_c[TQ - WINDOW_B:, :]
        vbp_scr[c, slot_b, :, :PAIR] = v_c[TQ - WINDOW_B:, :]
        k_win = jnp.concatenate([kbp_scr[c, 1 - slot_b], k_c], axis=0)
        v_win = jnp.concatenate([vbp_scr[c, 1 - slot_b], jnp.concatenate([v_c, ones], axis=1)], axis=0)
        heads = [(p, half) for p in range(N_PAIRS_B) for half in range(2)
                 if (half == p // PAIRS_PER_KV) == (c == 0)]
        s = _dot_nt(jnp.concatenate([qb[p][half] for p, half in heads], axis=0), k_win)
        probs, extras = [], []
        for i, (p, half) in enumerate(heads):
            pr, ex = _exp_scores([s[i * TQ:(i + 1) * TQ, :] + maskb], sinks_ref[2 * p + half] * LOG2E)
            probs.append(pr)
            extras.append(ex)
        for (p, half), o in zip(heads, _pv_stacked(probs, v_win, extras)):
            ob_half[p][half] = o
    ob = jnp.concatenate([jnp.where(low, o[0], o[1]) for o in ob_half], axis=-1)

    o_ref[0, :, :WIDTH_A] = _rms(oa, goa_ref[...]).astype(BF16)
    o_ref[0, :, WIDTH_A:] = _rms(ob, gob_ref[...]).astype(BF16)


def _mix_prompt(sinks, x, g, w_bf16, rope_rows, ct, st, rbv, goa, gob, *, layer):
    b, s, _ = x.shape
    nt = s // TQ
    cur = lambda bi, t: (bi, t, 0)
    keep_a = lambda bi, t: (bi, jnp.maximum(t - (nt - A_ROWS // TQ), 0), 0)
    seq = lambda bi, t: (bi, 0, 0)
    return pl.pallas_call(
        functools.partial(_mix_prompt_kernel, n_tiles=nt),
        grid=(b, nt),
        in_specs=[
            pl.BlockSpec(memory_space=pltpu.SMEM),
            pl.BlockSpec((1, TQ, D_MODEL), cur),
            _resident((1, D_MODEL)),
            _resident((D_MODEL, IN_WIDTH), layer),
            _resident(rope_rows.shape),
            _resident(ct.shape),
            _resident(st.shape),
            _resident((N_HEADS_A, BIAS_PERIOD)),
            _resident((1, WIDTH_A)),
            _resident((1, WIDTH_B)),
        ],
        out_specs=[
            pl.BlockSpec((1, TQ, MIX_WIDTH), cur),
            pl.BlockSpec((1, TQ, WIDTH_A), keep_a),
            pl.BlockSpec((1, TQ, WIDTH_A), keep_a),
            pl.BlockSpec((1, WINDOW_B, KV_WIDTH_B), seq),
            pl.BlockSpec((1, WINDOW_B, KV_WIDTH_B), seq),
        ],
        out_shape=[
            jax.ShapeDtypeStruct((b, s, MIX_WIDTH), BF16),
            jax.ShapeDtypeStruct((b, A_ROWS, WIDTH_A), F32),
            jax.ShapeDtypeStruct((b, A_ROWS, WIDTH_A), F32),
            jax.ShapeDtypeStruct((b, WINDOW_B, KV_WIDTH_B), F32),
            jax.ShapeDtypeStruct((b, WINDOW_B, KV_WIDTH_B), F32),
        ],
        scratch_shapes=[
            pltpu.VMEM((N_PAIRS_A, WIN_A, PAIR), BF16),
            pltpu.VMEM((N_PAIRS_A, WIN_A, 2 * PAIR), BF16),
            pltpu.VMEM((2, 2, WINDOW_B, PAIR), BF16),
            pltpu.VMEM((2, 2, WINDOW_B, 2 * PAIR), BF16),
            pltpu.VMEM((NBLK_A + 1, N_HEADS_A, TQ, TQ), F32),
            pltpu.VMEM((2, TQ, WIN_B), F32),
        ],
        compiler_params=pltpu.CompilerParams(
            dimension_semantics=("arbitrary", "arbitrary"),
            vmem_limit_bytes=V7X_VMEM_LIMIT_BYTES),
        name="mix_prompt",
    )(sinks, x, g, w_bf16, rope_rows, ct, st, rbv, goa, gob)


def _in_proj_kernel(x_ref, g_ref, w_ref, rope_ref, qa_ref, ka_ref, va_ref, qb_ref, kb_ref, vb_ref):
    n = _rms(x_ref[...], g_ref[...]).astype(BF16)
    z = _dot(n, w_ref[...])
    cos, sdn, sup = rope_ref[0], rope_ref[1], rope_ref[2]
    qa_ref[...] = (z[:, :O_KA] * QSCALE).astype(BF16)
    ka_ref[...] = z[:, O_KA:O_VA]
    va_ref[...] = z[:, O_VA:O_QB]
    for j in range(N_PAIRS_B):
        sl = slice(O_QB + j * PAIR, O_QB + (j + 1) * PAIR)
        qb_ref[:, j * PAIR:(j + 1) * PAIR] = (_rope(z[:, sl], cos, sdn, sup) * QSCALE).astype(BF16)
    kb_ref[...] = _rope(z[:, O_KB:O_VB], cos, sdn, sup)
    vb_ref[...] = z[:, O_VB:]


def _in_proj(x, g, w_bf16, rope_tabs, *, tm, layer):
    n_tok = x.shape[0]
    row = lambda i: (i, 0)
    widths = (WIDTH_A, WIDTH_A, WIDTH_A, WIDTH_B, KV_WIDTH_B, KV_WIDTH_B)
    dtypes = (BF16, F32, F32, BF16, F32, F32)
    return pl.pallas_call(
        _in_proj_kernel,
        grid=(n_tok // tm,),
        in_specs=[
            pl.BlockSpec((tm, D_MODEL), row),
            _resident((1, D_MODEL)),
            _resident((D_MODEL, IN_WIDTH), layer),
            _resident(rope_tabs.shape),
        ],
        out_specs=[pl.BlockSpec((tm, w), row) for w in widths],
        out_shape=[jax.ShapeDtypeStruct((n_tok, w), d) for w, d in zip(widths, dtypes)],
        compiler_params=pltpu.CompilerParams(
            dimension_semantics=("arbitrary",), vmem_limit_bytes=V7X_VMEM_LIMIT_BYTES),
        name="in_proj",
    )(x, g, w_bf16, rope_tabs)


N_SAMPLE_IN = 14
N_ROLL_IN = 8


def _attn_sample_kernel(*refs, n_prev, emit):
    (sinks_ref, qa_ref, kac_ref, kan_ref, vac_ref, van_ref, qb_ref, kbc_ref, kbn_ref, vbc_ref,
     vbn_ref, rbv_ref, goa_ref, gob_ref) = refs[:N_SAMPLE_IN]
    n_in = N_SAMPLE_IN + N_ROLL_IN * n_prev
    o_ref, bias_scr = refs[n_in], refs[-1]
    t_s = qa_ref.shape[0]
    la = kac_ref.shape[1]

    @pl.when(pl.program_id(0) == 0)
    def _init():
        for h in range(N_HEADS_A):
            bias_scr[h] = _rolled_bias_rows(rbv_ref, h, t_s)

    if emit:
        own = (kac_ref, kan_ref, vac_ref, van_ref, kbc_ref, kbn_ref, vbc_ref, vbn_ref)
        layers = [refs[N_SAMPLE_IN + N_ROLL_IN * l:N_SAMPLE_IN + N_ROLL_IN * (l + 1)]
                  for l in range(n_prev)] + [own]
        for l, lr in enumerate(layers):
            for j, new_ref in enumerate(refs[n_in + 1:n_in + 5]):
                cache_ref, fresh_ref = lr[2 * j], lr[2 * j + 1]
                kept = new_ref.shape[2] - t_s
                new_ref[l, 0, :kept, :] = cache_ref[0, cache_ref.shape[1] - kept:, :]
                new_ref[l, 0, kept:, :] = fresh_ref[...]

    low = _low_lanes((t_s, PAIR))
    ones_a = jnp.ones((la + t_s, PAIR), BF16)
    oa = []
    for p in range(N_PAIRS_A):
        sl = slice(p * PAIR, (p + 1) * PAIR)
        qp = qa_ref[:, sl]
        k = jnp.concatenate([kac_ref[0, :, sl], kan_ref[:, sl]], axis=0).astype(BF16)
        v = jnp.concatenate([vac_ref[0, :, sl], van_ref[:, sl]], axis=0).astype(BF16)
        s = _dot_nt(jnp.concatenate([_keep_half(qp, half, 0.0) for half in range(2)], axis=0), k)
        probs = [_exp_scores([s[half * t_s:(half + 1) * t_s, :] + bias_scr[2 * p + half, :, :la + t_s]])[0]
                 for half in range(2)]
        o = _pv_stacked(probs, jnp.concatenate([v, ones_a], axis=1))
        oa.append(jnp.where(low, o[0], o[1]))
    oa = jnp.concatenate(oa, axis=-1)

    kb = jnp.concatenate([kbc_ref[0], kbn_ref[...]], axis=0)
    vb = jnp.concatenate([vbc_ref[0], vbn_ref[...]], axis=0)
    ones_b = jnp.ones((kb.shape[0], PAIR), BF16)
    ob_half = [[None, None] for _ in range(N_PAIRS_B)]
    for c in range(2):
        k_c = (kb if c == 0 else pltpu.roll(kb, HEAD_DIM, axis=1)).astype(BF16)
        v_c = (vb if c == 0 else pltpu.roll(vb, HEAD_DIM, axis=1)).astype(BF16)
        heads = [(p, half) for p in range(N_PAIRS_B) for half in range(2)
                 if (half == p // PAIRS_PER_KV) == (c == 0)]
        s = _dot_nt(jnp.concatenate([_keep_half(qb_ref[:, p * PAIR:(p + 1) * PAIR], half, 0.0)
                                     for p, half in heads], axis=0), k_c)
        probs, extras = [], []
        for i, (p, half) in enumerate(heads):
            pr, ex = _exp_scores([s[i * t_s:(i + 1) * t_s, :]], sinks_ref[2 * p + half] * LOG2E)
            probs.append(pr)
            extras.append(ex)
        for (p, half), o in zip(heads, _pv_stacked(probs, jnp.concatenate([v_c, ones_b], axis=1), extras)):
            ob_half[p][half] = o
    ob = jnp.concatenate([jnp.where(low, o[0], o[1]) for o in ob_half], axis=-1)
    o_ref[:, :WIDTH_A] = _rms(oa, goa_ref[...]).astype(BF16)
    o_ref[:, WIDTH_A:] = _rms(ob, gob_ref[...]).astype(BF16)


def _attn_sample(sinks, qa, ka_cache, ka, va_cache, va, qb, kb_cache, kb, vb_cache, vb,
                 rbv, goa, gob, *, t_s, keep_a, keep_b, layer, prev_new, emit):
    n_tok = qa.shape[0]
    n_seq = n_tok // t_s
    la, lb = ka_cache.shape[2], kb_cache.shape[2]
    assert t_s <= keep_a <= la + t_s and t_s <= keep_b <= lb + t_s
    row = lambda i: (i, 0)
    cache = lambda l, rows, width: pl.BlockSpec((None, 1, rows, width), lambda i: (l, i, 0, 0))
    new = lambda width: pl.BlockSpec((t_s, width), row)

    def kv_specs(l):
        return [cache(l, la, WIDTH_A), new(WIDTH_A), cache(l, la, WIDTH_A), new(WIDTH_A),
                cache(l, lb, KV_WIDTH_B), new(KV_WIDTH_B), cache(l, lb, KV_WIDTH_B), new(KV_WIDTH_B)]

    own = kv_specs(layer)
    in_specs = ([pl.BlockSpec(memory_space=pltpu.SMEM), new(WIDTH_A)] + own[:4] + [new(WIDTH_B)] + own[4:]
                + [_resident((N_HEADS_A, BIAS_PERIOD)), _resident((1, WIDTH_A)), _resident((1, WIDTH_B))])
    operands = [sinks, qa, ka_cache, ka, va_cache, va, qb, kb_cache, kb, vb_cache, vb, rbv, goa, gob]
    out_specs = [pl.BlockSpec((t_s, MIX_WIDTH), row)]
    out_shape = [jax.ShapeDtypeStruct((n_tok, MIX_WIDTH), BF16)]
    n_prev = 0
    if emit:
        n_prev = len(prev_new)
        for l, (ka_l, va_l, kb_l, vb_l) in enumerate(prev_new):
            in_specs += kv_specs(l)
            operands += [ka_cache, ka_l, va_cache, va_l, kb_cache, kb_l, vb_cache, vb_l]
        n_layers = n_prev + 1
        for keep, width in ((keep_a, WIDTH_A), (keep_a, WIDTH_A), (keep_b, KV_WIDTH_B), (keep_b, KV_WIDTH_B)):
            out_specs.append(pl.BlockSpec((n_layers, 1, keep, width), lambda i: (0, i, 0, 0)))
            out_shape.append(jax.ShapeDtypeStruct((n_layers, n_seq, keep, width), F32))
    return pl.pallas_call(
        functools.partial(_attn_sample_kernel, n_prev=n_prev, emit=emit),
        grid=(n_seq,),
        in_specs=in_specs,
        out_specs=out_specs,
        out_shape=out_shape,
        scratch_shapes=[pltpu.VMEM((N_HEADS_A, t_s, BIAS_PERIOD), F32)],
        compiler_params=pltpu.CompilerParams(
            dimension_semantics=("arbitrary",), vmem_limit_bytes=V7X_VMEM_LIMIT_BYTES),
        name="attn_sample",
    )(*operands)


def _tail_kernel(h_ref, o_ref, p_ref, wout_ref, gffn_ref, wgu_ref, wdown_ref, wgate_ref,
                 wproj_ref, gfin_ref, out_ref, *, final, n_sub):
    rows = h_ref.shape[0] // n_sub
    subs = [pl.ds(i * rows, rows) for i in range(n_sub)]
    h = [h_ref[sl, :] + _dot(o_ref[sl, :], wout_ref[...]) for sl in subs]
    pp = [_dot(p_ref[sl, :].astype(BF16), wproj_ref[...]) for sl in subs]
    gu = [_dot(_rms(x, gffn_ref[...]).astype(BF16), wgu_ref[...]) for x in h]
    act = [(jax.nn.silu(x[:, :D_FF]) * x[:, D_FF:]).astype(BF16) for x in gu]
    h = [x + _dot(a, wdown_ref[...]) for x, a in zip(h, act)]
    gate = [jax.nn.sigmoid(_dot(x.astype(BF16), wgate_ref[...])) for x in h]
    for sl, x, g, e in zip(subs, h, gate, pp):
        x = x + g * e
        if final:
            x = _rms(x, gfin_ref[...])
        out_ref[sl, :] = x


def _tail(h, o, p, wout, gffn, wgu, wdown, wgate, wproj, gfin, *, tm, final, layer):
    n_tok = h.shape[0]
    row = lambda i: (i, 0)
    return pl.pallas_call(
        functools.partial(_tail_kernel, final=final, n_sub=2),
        grid=(n_tok // tm,),
        in_specs=[
            pl.BlockSpec((tm, D_MODEL), row),
            pl.BlockSpec((tm, MIX_WIDTH), row),
            pl.BlockSpec((None, tm, D_PLE), lambda i: (layer, i, 0)),
            _resident((MIX_WIDTH, D_MODEL), layer),
            _resident((1, D_MODEL)),
            _resident((D_MODEL, 2 * D_FF), layer),
            _resident((D_FF, D_MODEL), layer),
            _resident((D_MODEL, D_MODEL), layer),
            _resident((D_PLE, D_MODEL), layer),
            _resident((1, D_MODEL)),
        ],
        out_specs=pl.BlockSpec((tm, D_MODEL), row),
        out_shape=jax.ShapeDtypeStruct((n_tok, D_MODEL), F32),
        compiler_params=pltpu.CompilerParams(
            dimension_semantics=("arbitrary",), vmem_limit_bytes=V7X_VMEM_LIMIT_BYTES),
        name="layer_tail",
    )(h, o, p, wout, gffn, wgu, wdown, wgate, wproj, gfin)


CAST_ROWS = 256


def _cast_kernel(w_ref, o_ref):
    o_ref[...] = w_ref[...].astype(o_ref.dtype)


def _to_bf16(w):
    depth, rows, cols = w.shape
    blk = pl.BlockSpec((1, CAST_ROWS, cols), lambda d, r: (d, r, 0))
    return pl.pallas_call(
        _cast_kernel,
        grid=(depth, rows // CAST_ROWS),
        in_specs=[blk],
        out_specs=blk,
        out_shape=jax.ShapeDtypeStruct(w.shape, BF16),
        compiler_params=pltpu.CompilerParams(dimension_semantics=("arbitrary", "arbitrary")),
        name="cast_bf16",
    )(w)


def _rope_lane_tables(pos):
    half = ROT_DIM // 2
    d = np.arange(PAIR) % HEAD_DIM
    inv = ROPE_THETA ** (-(2.0 * (d % half)) / ROT_DIM)
    ang = np.where(d < ROT_DIM, np.asarray(pos, np.float64)[:, None] * inv[None, :], 0.0)
    lower = (d < half).astype(np.float64)[None, :]
    upper = ((d >= half) & (d < ROT_DIM)).astype(np.float64)[None, :]
    return np.cos(ang), np.sin(ang), lower, upper


def _rope_tables_direct(pos):
    cos, sin, lower, upper = _rope_lane_tables(pos)
    return jnp.asarray(np.stack([cos, -sin * lower, sin * upper]), F32)


def _rope_tables_split(n_tiles, tile):
    cr, sr, lower, upper = _rope_lane_tables(np.arange(tile))
    ct, st, _, _ = _rope_lane_tables(np.arange(n_tiles) * tile)
    rows = np.stack([cr, sr, -cr * lower, -sr * lower, cr * upper, sr * upper])
    return jnp.asarray(rows, F32), jnp.asarray(ct, F32), jnp.asarray(st, F32)


def _rel_bias_row(rel_bias):
    u = np.arange(BIAS_PERIOD)
    diff = np.where(u < BIAS_PERIOD // 2 + A_ROWS // 2, u, u - BIAS_PERIOD)
    idx = np.clip(A_ROWS - diff, -REL_CLIP, REL_CLIP) + REL_CLIP
    return rel_bias.astype(F32)[:, idx]


def kernel(x_prompt, x_sample, p_prompt, p_sample, cache_a_k, cache_a_v, cache_b_k, cache_b_v,
           g_mix_norm, w_in, rel_bias_a, sinks_b, g_out_a, g_out_b, w_out, g_ffn_norm,
           w_gate_up, w_down, w_ple_proj, w_ple_gate, g_final):
    b_p, s_p, _ = x_prompt.shape
    b_s, t_s, _ = x_sample.shape
    depth = w_in.shape[0]
    la_c, lb_c = cache_a_k.shape[2], cache_b_k.shape[2]
    keep_a_s = min(A_ROWS, la_c + t_s)
    keep_b_s = min(WINDOW_B, lb_c + t_s)
    tm_s = 512
    tm_tail = 512
    assert s_p % TQ == 0 and s_p >= A_ROWS and (b_s * t_s) % tm_s == 0 and tm_s % t_s == 0
    assert la_c == A_ROWS and lb_c == WINDOW_B and t_s == CHUNK

    rope_rows, rope_ct, rope_st = _rope_tables_split(s_p // TQ, TQ)
    rope_s = _rope_tables_direct(PAST_LEN + np.arange(tm_s) % t_s)
    row2 = lambda a: a.reshape(1, -1).astype(F32)
    g_fin = row2(g_final)

    hp = x_prompt
    hs = x_sample.reshape(b_s * t_s, D_MODEL)
    outs = [[] for _ in range(4)]
    new_rows = []
    w_in_b, w_out_b, w_gu_b, w_down_b, w_gate_b, w_proj_b = (
        _to_bf16(w) for w in (w_in, w_out, w_gate_up, w_down, w_ple_gate, w_ple_proj))
    p_prompt_f = p_prompt.reshape(depth, b_p * s_p, D_PLE)
    p_sample_f = p_sample.reshape(depth, b_s * t_s, D_PLE)
    cak_s = cache_a_k.reshape(depth, b_s, la_c, WIDTH_A)
    cav_s = cache_a_v.reshape(depth, b_s, la_c, WIDTH_A)
    cbk_s = cache_b_k.reshape(depth, b_s, lb_c, KV_WIDTH_B)
    cbv_s = cache_b_v.reshape(depth, b_s, lb_c, KV_WIDTH_B)
    for i in range(depth):
        tail_w = (w_out_b, row2(g_ffn_norm[i]), w_gu_b, w_down_b, w_gate_b, w_proj_b, g_fin)
        g_mix = row2(g_mix_norm[i])
        goa, gob = row2(g_out_a[i]), row2(g_out_b[i])
        sinks = sinks_b[i].astype(F32)
        rbv = _rel_bias_row(rel_bias_a[i])
        final = i == depth - 1

        o, cak, cav, cbk, cbv = _mix_prompt(sinks, hp, g_mix, w_in_b, rope_rows, rope_ct, rope_st,
                                            rbv, goa, gob, layer=i)
        hp = _tail(hp.reshape(b_p * s_p, D_MODEL), o.reshape(b_p * s_p, MIX_WIDTH),
                   p_prompt_f, *tail_w, tm=tm_tail, final=final, layer=i)
        hp = hp.reshape(b_p, s_p, D_MODEL)
        outs[0].append(cak.reshape(b_p, A_ROWS, N_HEADS_A, HEAD_DIM))
        outs[1].append(cav.reshape(b_p, A_ROWS, N_HEADS_A, HEAD_DIM))
        outs[2].append(cbk.reshape(b_p, WINDOW_B, N_KV_B, HEAD_DIM))
        outs[3].append(cbv.reshape(b_p, WINDOW_B, N_KV_B, HEAD_DIM))

        qa, ka, va, qb, kb, vb = _in_proj(hs, g_mix, w_in_b, rope_s, tm=tm_s, layer=i)
        o, *rolled = _attn_sample(sinks, qa, cak_s, ka, cav_s, va, qb, cbk_s, kb, cbv_s, vb,
                                  rbv, goa, gob, t_s=t_s, keep_a=keep_a_s, keep_b=keep_b_s,
                                  layer=i, prev_new=new_rows, emit=final)
        new_rows.append((ka, va, kb, vb))
        hs = _tail(hs, o, p_sample_f, *tail_w, tm=tm_tail, final=final, layer=i)

    nak, nav, nbk, nbv = rolled
    y_sample = hs.reshape(b_s, t_s, D_MODEL)
    return ((hp, y_sample) + tuple(jnp.stack(o) for o in outs)
            + (nak.reshape(depth, b_s, keep_a_s, N_HEADS_A, HEAD_DIM),
               nav.reshape(depth, b_s, keep_a_s, N_HEADS_A, HEAD_DIM),
               nbk.reshape(depth, b_s, keep_b_s, N_KV_B, HEAD_DIM),
               nbv.reshape(depth, b_s, keep_b_s, N_KV_B, HEAD_DIM)))
```

```python
import functools
import math

import numpy as np
import jax
import jax.numpy as jnp
from jax import lax
from jax.experimental import pallas as pl
from jax.experimental.pallas import tpu as pltpu

D_MODEL = 1024
CHUNK = 64
HEAD_DIM = 64
N_HEADS_A = 8
N_HEADS_B = 8
N_KV_B = 2
GROUP_B = N_HEADS_B // N_KV_B
WIDTH_A = N_HEADS_A * HEAD_DIM
WIDTH_B = N_HEADS_B * HEAD_DIM
KV_WIDTH_B = N_KV_B * HEAD_DIM
MIX_WIDTH = WIDTH_A + WIDTH_B
IN_WIDTH = 3 * WIDTH_A + WIDTH_B + 2 * KV_WIDTH_B
PREV_CHUNKS_A = 8
A_ROWS = PREV_CHUNKS_A * CHUNK
REL_CLIP = 256
WINDOW_B = 128
PREV_CHUNKS_B = WINDOW_B // CHUNK
ROT_DIM = HEAD_DIM // 4
ROPE_THETA = 500000.0
D_FF = 2816
D_PLE = 256
PAST_LEN = 2048
RMS_EPS = 1e-6
LOG2E = math.log2(math.e)
QSCALE = HEAD_DIM ** -0.5 * LOG2E
PAIR = 2 * HEAD_DIM
N_PAIRS_A = WIDTH_A // PAIR
N_PAIRS_B = WIDTH_B // PAIR
PAIRS_PER_KV = GROUP_B // 2
NEG_INF = float("-inf")
BIAS_PERIOD = 1024

V7X_VMEM_LIMIT_BYTES = 56 * 1024 * 1024

BF16 = jnp.bfloat16
F32 = jnp.float32

O_KA, O_VA, O_QB = WIDTH_A, 2 * WIDTH_A, 3 * WIDTH_A
O_KB = O_QB + WIDTH_B
O_VB = O_KB + KV_WIDTH_B


def _rms(x, g):
    return x * lax.rsqrt(jnp.mean(x * x, axis=-1, keepdims=True) + RMS_EPS) * g


def _dot(a, b):
    return jnp.dot(a, b, preferred_element_type=F32)


def _dot_nt(a, b):
    return lax.dot_general(a, b, (((1,), (1,)), ((), ())), preferred_element_type=F32)


def _resident(shape, layer=None):
    nd = len(shape)
    if layer is None:
        return pl.BlockSpec(shape, lambda *_: (0,) * nd, pipeline_mode=pl.Buffered(1))
    return pl.BlockSpec((None,) + tuple(shape), lambda *_: (layer,) + (0,) * nd,
                        pipeline_mode=pl.Buffered(1))


def _rope(x, cos, sdn, sup):
    half = ROT_DIM // 2
    return (x * cos + pltpu.roll(x, PAIR - half, axis=1) * sdn
            + pltpu.roll(x, half, axis=1) * sup)


def _low_lanes(shape):
    return lax.broadcasted_iota(jnp.int32, shape, 1) < HEAD_DIM


def _keep_half(x, half, fill):
    keep = _low_lanes(x.shape) if half == 0 else ~_low_lanes(x.shape)
    return jnp.where(keep, x, jnp.full_like(x, fill))


def _exp_scores(s_blocks, sink=None):
    if len({s.shape for s in s_blocks}) == 1:
        m = jnp.max(functools.reduce(jnp.maximum, s_blocks), axis=-1, keepdims=True)
    else:
        m = functools.reduce(jnp.maximum, [jnp.max(s, axis=-1, keepdims=True) for s in s_blocks])
    if sink is not None:
        m = jnp.maximum(m, sink)
    p = jnp.concatenate([jnp.exp2(s - m).astype(BF16) for s in s_blocks], axis=-1)
    return p, (None if sink is None else jnp.exp2(sink - m))


def _pv_stacked(probs, v_ext, extras=None):
    m = probs[0].shape[0]
    x = _dot(jnp.concatenate(probs, axis=0), v_ext)
    outs = []
    for i in range(len(probs)):
        den = x[i * m:(i + 1) * m, PAIR:]
        if extras is not None:
            den = den + extras[i]
        outs.append(x[i * m:(i + 1) * m, :PAIR] / den)
    return outs


def _rolled_bias_rows(rbv_ref, h, rows):
    x = jnp.broadcast_to(rbv_ref[h:h + 1, :] * LOG2E, (rows, BIAS_PERIOD))
    return pltpu.roll(x, 0, axis=1, stride=1, stride_axis=0)


def _band_valid(rows, col0, cols, prev_chunks):
    r = lax.broadcasted_iota(jnp.int32, (rows, cols), 0) // CHUNK
    c = (lax.broadcasted_iota(jnp.int32, (rows, cols), 1) + col0) // CHUNK
    return (c >= r) & (c <= r + prev_chunks)


TQ = 256
NBLK_A = A_ROWS // TQ + 1
WIN_A = NBLK_A * TQ
WIN_B = WINDOW_B + TQ
NEG_KIND = NBLK_A


def _mix_prompt_kernel(sinks_ref, x_ref, g_ref, w_ref, rope_ref, ct_ref, st_ref, rbv_ref,
                       goa_ref, gob_ref,
                       o_ref, cak_ref, cav_ref, cbk_ref, cbv_ref,
                       ka_scr, va_scr, kbp_scr, vbp_scr, bias_scr, maskb_scr, *, n_tiles):
    b, t = pl.program_id(0), pl.program_id(1)

    @pl.when((b == 0) & (t == 0))
    def _init():
        ka_scr[...] = jnp.zeros_like(ka_scr)
        kbp_scr[...] = jnp.zeros_like(kbp_scr)
        for v_scr in (va_scr, vbp_scr):
            v_scr[..., :PAIR] = jnp.zeros(v_scr.shape[:-1] + (PAIR,), BF16)
            v_scr[..., PAIR:] = jnp.ones(v_scr.shape[:-1] + (PAIR,), BF16)
        for h in range(N_HEADS_A):
            rows = _rolled_bias_rows(rbv_ref, h, TQ)
            for j in range(NBLK_A):
                valid = _band_valid(TQ, j * TQ, TQ, PREV_CHUNKS_A)
                bias_scr[j, h] = jnp.where(valid, rows[:, j * TQ:(j + 1) * TQ], NEG_INF)
            bias_scr[NEG_KIND, h] = jnp.full((TQ, TQ), NEG_INF, F32)
        band = jnp.where(_band_valid(TQ, 0, WIN_B, PREV_CHUNKS_B), jnp.zeros((TQ, WIN_B), F32), NEG_INF)
        col = lax.broadcasted_iota(jnp.int32, (TQ, WIN_B), 1)
        maskb_scr[0] = band
        maskb_scr[1] = jnp.where(col >= WINDOW_B, band, NEG_INF)

    n = _rms(x_ref[0], g_ref[...]).astype(BF16)
    slot = lax.rem(t, NBLK_A)
    row0 = pl.multiple_of(slot * TQ, TQ)
    ka = _dot(n, w_ref[:, O_KA:O_VA])
    for p in range(N_PAIRS_A):
        ka_scr[p, pl.ds(row0, TQ), :] = ka[:, p * PAIR:(p + 1) * PAIR].astype(BF16)
    za = _dot(n, w_ref[:, :O_KA])
    qa = [[_keep_half(za[:, j * PAIR:(j + 1) * PAIR] * QSCALE, half, 0.0).astype(BF16)
           for half in range(2)] for j in range(N_PAIRS_A)]
    va = _dot(n, w_ref[:, O_VA:O_QB])
    for p in range(N_PAIRS_A):
        va_scr[p, pl.ds(row0, TQ), :PAIR] = va[:, p * PAIR:(p + 1) * PAIR].astype(BF16)
    zb = _dot(n, w_ref[:, O_QB:])
    ct, st = ct_ref[pl.ds(t, 1), :], st_ref[pl.ds(t, 1), :]
    cos = ct * rope_ref[0] - st * rope_ref[1]
    sdn = st * rope_ref[2] + ct * rope_ref[3]
    sup = st * rope_ref[4] + ct * rope_ref[5]
    kb = _rope(zb[:, WIDTH_B:WIDTH_B + KV_WIDTH_B], cos, sdn, sup)
    vb = zb[:, WIDTH_B + KV_WIDTH_B:]
    qb = []
    for j in range(N_PAIRS_B):
        q = _rope(zb[:, j * PAIR:(j + 1) * PAIR], cos, sdn, sup) * QSCALE
        qb.append([_keep_half(q, half, 0.0).astype(BF16) for half in range(2)])

    @pl.when(t >= n_tiles - A_ROWS // TQ)
    def _():
        cak_ref[0] = ka
        cav_ref[0] = va

    @pl.when(t == n_tiles - 1)
    def _():
        cbk_ref[0] = kb[TQ - WINDOW_B:, :]
        cbv_ref[0] = vb[TQ - WINDOW_B:, :]

    kinds = []
    for ps in range(NBLK_A):
        d = lax.rem(t - ps + NBLK_A, NBLK_A)
        kinds.append(jnp.where(t >= d, NBLK_A - 1 - d, NEG_KIND))
    low = _low_lanes((TQ, PAIR))
    oa = []
    for p in range(N_PAIRS_A):
        s = _dot_nt(jnp.concatenate(qa[p], axis=0), ka_scr[p])
        probs = []
        for half in range(2):
            s_blocks = [s[half * TQ:(half + 1) * TQ, ps * TQ:(ps + 1) * TQ]
                        + bias_scr[kinds[ps], 2 * p + half] for ps in range(NBLK_A)]
            probs.append(_exp_scores(s_blocks)[0])
        o = _pv_stacked(probs, va_scr[p])
        oa.append(jnp.where(low, o[0], o[1]))
    oa = jnp.concatenate(oa, axis=-1)

    kb_sw, vb_sw = pltpu.roll(kb, HEAD_DIM, axis=1), pltpu.roll(vb, HEAD_DIM, axis=1)
    ones = jnp.ones((TQ, PAIR), BF16)
    slot_b = lax.rem(t, 2)
    maskb = maskb_scr[jnp.where(t >= 1, 0, 1)]
    ob_half = [[None, None] for _ in range(N_PAIRS_B)]
    for c, (k_c, v_c) in enumerate(((kb, vb), (kb_sw, vb_sw))):
        k_c, v_c = k_c.astype(BF16), v_c.astype(BF16)
        kbp_scr[c, slot_b] = k_c[TQ - WINDOW_B:, :]
        vbp_scr[c, slot_b, :, :PAIR] = v_c[TQ - WINDOW_B:, :]
        k_win = jnp.concatenate([kbp_scr[c, 1 - slot_b], k_c], axis=0)
        v_win = jnp.concatenate([vbp_scr[c, 1 - slot_b], jnp.concatenate([v_c, ones], axis=1)], axis=0)
        heads = [(p, half) for p in range(N_PAIRS_B) for half in range(2)
                 if (half == p // PAIRS_PER_KV) == (c == 0)]
        s = _dot_nt(jnp.concatenate([qb[p][half] for p, half in heads], axis=0), k_win)
        probs, extras = [], []
        for i, (p, half) in enumerate(heads):
            pr, ex = _exp_scores([s[i * TQ:(i + 1) * TQ, :] + maskb], sinks_ref[2 * p + half] * LOG2E)
            probs.append(pr)
            extras.append(ex)
        for (p, half), o in zip(heads, _pv_stacked(probs, v_win, extras)):
            ob_half[p][half] = o
    ob = jnp.concatenate([jnp.where(low, o[0], o[1]) for o in ob_half], axis=-1)

    o_ref[0, :, :WIDTH_A] = _rms(oa, goa_ref[...]).astype(BF16)
    o_ref[0, :, WIDTH_A:] = _rms(ob, gob_ref[...]).astype(BF16)


def _mix_prompt(sinks, x, g, w_bf16, rope_rows, ct, st, rbv, goa, gob, *, layer):
    b, s, _ = x.shape
    nt = s // TQ
    cur = lambda bi, t: (bi, t, 0)
    keep_a = lambda bi, t: (bi, jnp.maximum(t - (nt - A_ROWS // TQ), 0), 0)
    seq = lambda bi, t: (bi, 0, 0)
    return pl.pallas_call(
        functools.partial(_mix_prompt_kernel, n_tiles=nt),
        grid=(b, nt),
        in_specs=[
            pl.BlockSpec(memory_space=pltpu.SMEM),
            pl.BlockSpec((1, TQ, D_MODEL), cur),
            _resident((1, D_MODEL)),
            _resident((D_MODEL, IN_WIDTH), layer),
            _resident(rope_rows.shape),
            _resident(ct.shape),
            _resident(st.shape),
            _resident((N_HEADS_A, BIAS_PERIOD)),
            _resident((1, WIDTH_A)),
            _resident((1, WIDTH_B)),
        ],
        out_specs=[
            pl.BlockSpec((1, TQ, MIX_WIDTH), cur),
            pl.BlockSpec((1, TQ, WIDTH_A), keep_a),
            pl.BlockSpec((1, TQ, WIDTH_A), keep_a),
            pl.BlockSpec((1, WINDOW_B, KV_WIDTH_B), seq),
            pl.BlockSpec((1, WINDOW_B, KV_WIDTH_B), seq),
        ],
        out_shape=[
            jax.ShapeDtypeStruct((b, s, MIX_WIDTH), BF16),
            jax.ShapeDtypeStruct((b, A_ROWS, WIDTH_A), F32),
            jax.ShapeDtypeStruct((b, A_ROWS, WIDTH_A), F32),
            jax.ShapeDtypeStruct((b, WINDOW_B, KV_WIDTH_B), F32),
            jax.ShapeDtypeStruct((b, WINDOW_B, KV_WIDTH_B), F32),
        ],
        scratch_shapes=[
            pltpu.VMEM((N_PAIRS_A, WIN_A, PAIR), BF16),
            pltpu.VMEM((N_PAIRS_A, WIN_A, 2 * PAIR), BF16),
            pltpu.VMEM((2, 2, WINDOW_B, PAIR), BF16),
            pltpu.VMEM((2, 2, WINDOW_B, 2 * PAIR), BF16),
            pltpu.VMEM((NBLK_A + 1, N_HEADS_A, TQ, TQ), F32),
            pltpu.VMEM((2, TQ, WIN_B), F32),
        ],
        compiler_params=pltpu.CompilerParams(
            dimension_semantics=("arbitrary", "arbitrary"),
            vmem_limit_bytes=V7X_VMEM_LIMIT_BYTES),
        name="mix_prompt",
    )(sinks, x, g, w_bf16, rope_rows, ct, st, rbv, goa, gob)


def _in_proj_kernel(x_ref, g_ref, w_ref, rope_ref, qa_ref, ka_ref, va_ref, qb_ref, kb_ref, vb_ref):
    n = _rms(x_ref[...], g_ref[...]).astype(BF16)
    z = _dot(n, w_ref[...])
    cos, sdn, sup = rope_ref[0], rope_ref[1], rope_ref[2]
    qa_ref[...] = (z[:, :O_KA] * QSCALE).astype(BF16)
    ka_ref[...] = z[:, O_KA:O_VA]
    va_ref[...] = z[:, O_VA:O_QB]
    for j in range(N_PAIRS_B):
        sl = slice(O_QB + j * PAIR, O_QB + (j + 1) * PAIR)
        qb_ref[:, j * PAIR:(j + 1) * PAIR] = (_rope(z[:, sl], cos, sdn, sup) * QSCALE).astype(BF16)
    kb_ref[...] = _rope(z[:, O_KB:O_VB], cos, sdn, sup)
    vb_ref[...] = z[:, O_VB:]


def _in_proj(x, g, w_bf16, rope_tabs, *, tm, layer):
    n_tok = x.shape[0]
    row = lambda i: (i, 0)
    widths = (WIDTH_A, WIDTH_A, WIDTH_A, WIDTH_B, KV_WIDTH_B, KV_WIDTH_B)
    dtypes = (BF16, F32, F32, BF16, F32, F32)
    return pl.pallas_call(
        _in_proj_kernel,
        grid=(n_tok // tm,),
        in_specs=[
            pl.BlockSpec((tm, D_MODEL), row),
            _resident((1, D_MODEL)),
            _resident((D_MODEL, IN_WIDTH), layer),
            _resident(rope_tabs.shape),
        ],
        out_specs=[pl.BlockSpec((tm, w), row) for w in widths],
        out_shape=[jax.ShapeDtypeStruct((n_tok, w), d) for w, d in zip(widths, dtypes)],
        compiler_params=pltpu.CompilerParams(
            dimension_semantics=("arbitrary",), vmem_limit_bytes=V7X_VMEM_LIMIT_BYTES),
        name="in_proj",
    )(x, g, w_bf16, rope_tabs)


N_SAMPLE_IN = 14
N_ROLL_IN = 8


def _attn_sample_kernel(*refs, n_prev, emit):
    (sinks_ref, qa_ref, kac_ref, kan_ref, vac_ref, van_ref, qb_ref, kbc_ref, kbn_ref, vbc_ref,
     vbn_ref, rbv_ref, goa_ref, gob_ref) = refs[:N_SAMPLE_IN]
    n_in = N_SAMPLE_IN + N_ROLL_IN * n_prev
    o_ref, bias_scr = refs[n_in], refs[-1]
    t_s = qa_ref.shape[0]
    la = kac_ref.shape[1]

    @pl.when(pl.program_id(0) == 0)
    def _init():
        for h in range(N_HEADS_A):
            bias_scr[h] = _rolled_bias_rows(rbv_ref, h, t_s)

    if emit:
        own = (kac_ref, kan_ref, vac_ref, van_ref, kbc_ref, kbn_ref, vbc_ref, vbn_ref)
        layers = [refs[N_SAMPLE_IN + N_ROLL_IN * l:N_SAMPLE_IN + N_ROLL_IN * (l + 1)]
                  for l in range(n_prev)] + [own]
        for l, lr in enumerate(layers):
            for j, new_ref in enumerate(refs[n_in + 1:n_in + 5]):
                cache_ref, fresh_ref = lr[2 * j], lr[2 * j + 1]
                kept = new_ref.shape[2] - t_s
                new_ref[l, 0, :kept, :] = cache_ref[0, cache_ref.shape[1] - kept:, :]
                new_ref[l, 0, kept:, :] = fresh_ref[...]

    low = _low_lanes((t_s, PAIR))
    ones_a = jnp.ones((la + t_s, PAIR), BF16)
    oa = []
    for p in range(N_PAIRS_A):
        sl = slice(p * PAIR, (p + 1) * PAIR)
        qp = qa_ref[:, sl]
        k = jnp.concatenate([kac_ref[0, :, sl], kan_ref[:, sl]], axis=0).astype(BF16)
        v = jnp.concatenate([vac_ref[0, :, sl], van_ref[:, sl]], axis=0).astype(BF16)
        s = _dot_nt(jnp.concatenate([_keep_half(qp, half, 0.0) for half in range(2)], axis=0), k)
        probs = [_exp_scores([s[half * t_s:(half + 1) * t_s, :] + bias_scr[2 * p + half, :, :la + t_s]])[0]
                 for half in range(2)]
        o = _pv_stacked(probs, jnp.concatenate([v, ones_a], axis=1))
        oa.append(jnp.where(low, o[0], o[1]))
    oa = jnp.concatenate(oa, axis=-1)

    kb = jnp.concatenate([kbc_ref[0], kbn_ref[...]], axis=0)
    vb = jnp.concatenate([vbc_ref[0], vbn_ref[...]], axis=0)
    ones_b = jnp.ones((kb.shape[0], PAIR), BF16)
    ob_half = [[None, None] for _ in range(N_PAIRS_B)]
    for c in range(2):
        k_c = (kb if c == 0 else pltpu.roll(kb, HEAD_DIM, axis=1)).astype(BF16)
        v_c = (vb if c == 0 else pltpu.roll(vb, HEAD_DIM, axis=1)).astype(BF16)
        heads = [(p, half) for p in range(N_PAIRS_B) for half in range(2)
                 if (half == p // PAIRS_PER_KV) == (c == 0)]
        s = _dot_nt(jnp.concatenate([_keep_half(qb_ref[:, p * PAIR:(p + 1) * PAIR], half, 0.0)
                                     for p, half in heads], axis=0), k_c)
        probs, extras = [], []
        for i, (p, half) in enumerate(heads):
            pr, ex = _exp_scores([s[i * t_s:(i + 1) * t_s, :]], sinks_ref[2 * p + half] * LOG2E)
            probs.append(pr)
            extras.append(ex)
        for (p, half), o in zip(heads, _pv_stacked(probs, jnp.concatenate([v_c, ones_b], axis=1), extras)):
            ob_half[p][half] = o
    ob = jnp.concatenate([jnp.where(low, o[0], o[1]) for o in ob_half], axis=-1)
    o_ref[:, :WIDTH_A] = _rms(oa, goa_ref[...]).astype(BF16)
    o_ref[:, WIDTH_A:] = _rms(ob, gob_ref[...]).astype(BF16)


def _attn_sample(sinks, qa, ka_cache, ka, va_cache, va, qb, kb_cache, kb, vb_cache, vb,
                 rbv, goa, gob, *, t_s, keep_a, keep_b, layer, prev_new, emit):
    n_tok = qa.shape[0]
    n_seq = n_tok // t_s
    la, lb = ka_cache.shape[2], kb_cache.shape[2]
    assert t_s <= keep_a <= la + t_s and t_s <= keep_b <= lb + t_s
    row = lambda i: (i, 0)
    cache = lambda l, rows, width: pl.BlockSpec((None, 1, rows, width), lambda i: (l, i, 0, 0))
    new = lambda width: pl.BlockSpec((t_s, width), row)

    def kv_specs(l):
        return [cache(l, la, WIDTH_A), new(WIDTH_A), cache(l, la, WIDTH_A), new(WIDTH_A),
                cache(l, lb, KV_WIDTH_B), new(KV_WIDTH_B), cache(l, lb, KV_WIDTH_B), new(KV_WIDTH_B)]

    own = kv_specs(layer)
    in_specs = ([pl.BlockSpec(memory_space=pltpu.SMEM), new(WIDTH_A)] + own[:4] + [new(WIDTH_B)] + own[4:]
                + [_resident((N_HEADS_A, BIAS_PERIOD)), _resident((1, WIDTH_A)), _resident((1, WIDTH_B))])
    operands = [sinks, qa, ka_cache, ka, va_cache, va, qb, kb_cache, kb, vb_cache, vb, rbv, goa, gob]
    out_specs = [pl.BlockSpec((t_s, MIX_WIDTH), row)]
    out_shape = [jax.ShapeDtypeStruct((n_tok, MIX_WIDTH), BF16)]
    n_prev = 0
    if emit:
        n_prev = len(prev_new)
        for l, (ka_l, va_l, kb_l, vb_l) in enumerate(prev_new):
            in_specs += kv_specs(l)
            operands += [ka_cache, ka_l, va_cache, va_l, kb_cache, kb_l, vb_cache, vb_l]
        n_layers = n_prev + 1
        for keep, width in ((keep_a, WIDTH_A), (keep_a, WIDTH_A), (keep_b, KV_WIDTH_B), (keep_b, KV_WIDTH_B)):
            out_specs.append(pl.BlockSpec((n_layers, 1, keep, width), lambda i: (0, i, 0, 0)))
            out_shape.append(jax.ShapeDtypeStruct((n_layers, n_seq, keep, width), F32))
    return pl.pallas_call(
        functools.partial(_attn_sample_kernel, n_prev=n_prev, emit=emit),
        grid=(n_seq,),
        in_specs=in_specs,
        out_specs=out_specs,
        out_shape=out_shape,
        scratch_shapes=[pltpu.VMEM((N_HEADS_A, t_s, BIAS_PERIOD), F32)],
        compiler_params=pltpu.CompilerParams(
            dimension_semantics=("arbitrary",), vmem_limit_bytes=V7X_VMEM_LIMIT_BYTES),
        name="attn_sample",
    )(*operands)


def _tail_kernel(h_ref, o_ref, p_ref, wout_ref, gffn_ref, wgu_ref, wdown_ref, wgate_ref,
                 wproj_ref, gfin_ref, out_ref, *, final, n_sub):
    rows = h_ref.shape[0] // n_sub
    subs = [pl.ds(i * rows, rows) for i in range(n_sub)]
    h = [h_ref[sl, :] + _dot(o_ref[sl, :], wout_ref[...]) for sl in subs]
    pp = [_dot(p_ref[sl, :].astype(BF16), wproj_ref[...]) for sl in subs]
    gu = [_dot(_rms(x, gffn_ref[...]).astype(BF16), wgu_ref[...]) for x in h]
    act = [(jax.nn.silu(x[:, :D_FF]) * x[:, D_FF:]).astype(BF16) for x in gu]
    h = [x + _dot(a, wdown_ref[...]) for x, a in zip(h, act)]
    gate = [jax.nn.sigmoid(_dot(x.astype(BF16), wgate_ref[...])) for x in h]
    for sl, x, g, e in zip(subs, h, gate, pp):
        x = x + g * e
        if final:
            x = _rms(x, gfin_ref[...])
        out_ref[sl, :] = x


def _tail(h, o, p, wout, gffn, wgu, wdown, wgate, wproj, gfin, *, tm, final, layer):
    n_tok = h.shape[0]
    row = lambda i: (i, 0)
    return pl.pallas_call(
        functools.partial(_tail_kernel, final=final, n_sub=2),
        grid=(n_tok // tm,),
        in_specs=[
            pl.BlockSpec((tm, D_MODEL), row),
            pl.BlockSpec((tm, MIX_WIDTH), row),
            pl.BlockSpec((None, tm, D_PLE), lambda i: (layer, i, 0)),
            _resident((MIX_WIDTH, D_MODEL), layer),
            _resident((1, D_MODEL)),
            _resident((D_MODEL, 2 * D_FF), layer),
            _resident((D_FF, D_MODEL), layer),
            _resident((D_MODEL, D_MODEL), layer),
            _resident((D_PLE, D_MODEL), layer),
            _resident((1, D_MODEL)),
        ],
        out_specs=pl.BlockSpec((tm, D_MODEL), row),
        out_shape=jax.ShapeDtypeStruct((n_tok, D_MODEL), F32),
        compiler_params=pltpu.CompilerParams(
            dimension_semantics=("arbitrary",), vmem_limit_bytes=V7X_VMEM_LIMIT_BYTES),
        name="layer_tail",
    )(h, o, p, wout, gffn, wgu, wdown, wgate, wproj, gfin)


CAST_ROWS = 256


def _cast_kernel(w_ref, o_ref):
    o_ref[...] = w_ref[...].astype(o_ref.dtype)


def _to_bf16(w):
    depth, rows, cols = w.shape
    blk = pl.BlockSpec((1, CAST_ROWS, cols), lambda d, r: (d, r, 0))
    return pl.pallas_call(
        _cast_kernel,
        grid=(depth, rows // CAST_ROWS),
        in_specs=[blk],
        out_specs=blk,
        out_shape=jax.ShapeDtypeStruct(w.shape, BF16),
        compiler_params=pltpu.CompilerParams(dimension_semantics=("arbitrary", "arbitrary")),
        name="cast_bf16",
    )(w)


def _rope_lane_tables(pos):
    half = ROT_DIM // 2
    d = np.arange(PAIR) % HEAD_DIM
    inv = ROPE_THETA ** (-(2.0 * (d % half)) / ROT_DIM)
    ang = np.where(d < ROT_DIM, np.asarray(pos, np.float64)[:, None] * inv[None, :], 0.0)
    lower = (d < half).astype(np.float64)[None, :]
    upper = ((d >= half) & (d < ROT_DIM)).astype(np.float64)[None, :]
    return np.cos(ang), np.sin(ang), lower, upper


def _rope_tables_direct(pos):
    cos, sin, lower, upper = _rope_lane_tables(pos)
    return jnp.asarray(np.stack([cos, -sin * lower, sin * upper]), F32)


def _rope_tables_split(n_tiles, tile):
    cr, sr, lower, upper = _rope_lane_tables(np.arange(tile))
    ct, st, _, _ = _rope_lane_tables(np.arange(n_tiles) * tile)
    rows = np.stack([cr, sr, -cr * lower, -sr * lower, cr * upper, sr * upper])
    return jnp.asarray(rows, F32), jnp.asarray(ct, F32), jnp.asarray(st, F32)


def _rel_bias_row(rel_bias):
    u = np.arange(BIAS_PERIOD)
    diff = np.where(u < BIAS_PERIOD // 2 + A_ROWS // 2, u, u - BIAS_PERIOD)
    idx = np.clip(A_ROWS - diff, -REL_CLIP, REL_CLIP) + REL_CLIP
    return rel_bias.astype(F32)[:, idx]


def kernel(x_prompt, x_sample, p_prompt, p_sample, cache_a_k, cache_a_v, cache_b_k, cache_b_v,
           g_mix_norm, w_in, rel_bias_a, sinks_b, g_out_a, g_out_b, w_out, g_ffn_norm,
           w_gate_up, w_down, w_ple_proj, w_ple_gate, g_final):
    b_p, s_p, _ = x_prompt.shape
    b_s, t_s, _ = x_sample.shape
    depth = w_in.shape[0]
    la_c, lb_c = cache_a_k.shape[2], cache_b_k.shape[2]
    keep_a_s = min(A_ROWS, la_c + t_s)
    keep_b_s = min(WINDOW_B, lb_c + t_s)
    tm_s = 512
    tm_tail = 512
    assert s_p % TQ == 0 and s_p >= A_ROWS and (b_s * t_s) % tm_s == 0 and tm_s % t_s == 0
    assert la_c == A_ROWS and lb_c == WINDOW_B and t_s == CHUNK

    rope_rows, rope_ct, rope_st = _rope_tables_split(s_p // TQ, TQ)
    rope_s = _rope_tables_direct(PAST_LEN + np.arange(tm_s) % t_s)
    row2 = lambda a: a.reshape(1, -1).astype(F32)
    g_fin = row2(g_final)

    hp = x_prompt
    hs = x_sample.reshape(b_s * t_s, D_MODEL)
    outs = [[] for _ in range(4)]
    new_rows = []
    w_in_b, w_out_b, w_gu_b, w_down_b, w_gate_b, w_proj_b = (
        _to_bf16(w) for w in (w_in, w_out, w_gate_up, w_down, w_ple_gate, w_ple_proj))
    p_prompt_f = p_prompt.reshape(depth, b_p * s_p, D_PLE)
    p_sample_f = p_sample.reshape(depth, b_s * t_s, D_PLE)
    cak_s = cache_a_k.reshape(depth, b_s, la_c, WIDTH_A)
    cav_s = cache_a_v.reshape(depth, b_s, la_c, WIDTH_A)
    cbk_s = cache_b_k.reshape(depth, b_s, lb_c, KV_WIDTH_B)
    cbv_s = cache_b_v.reshape(depth, b_s, lb_c, KV_WIDTH_B)
    for i in range(depth):
        tail_w = (w_out_b, row2(g_ffn_norm[i]), w_gu_b, w_down_b, w_gate_b, w_proj_b, g_fin)
        g_mix = row2(g_mix_norm[i])
        goa, gob = row2(g_out_a[i]), row2(g_out_b[i])
        sinks = sinks_b[i].astype(F32)
        rbv = _rel_bias_row(rel_bias_a[i])
        final = i == depth - 1

        o, cak, cav, cbk, cbv = _mix_prompt(sinks, hp, g_mix, w_in_b, rope_rows, rope_ct, rope_st,
                                            rbv, goa, gob, layer=i)
        hp = _tail(hp.reshape(b_p * s_p, D_MODEL), o.reshape(b_p * s_p, MIX_WIDTH),
                   p_prompt_f, *tail_w, tm=tm_tail, final=final, layer=i)
        hp = hp.reshape(b_p, s_p, D_MODEL)
        outs[0].append(cak.reshape(b_p, A_ROWS, N_HEADS_A, HEAD_DIM))
        outs[1].append(cav.reshape(b_p, A_ROWS, N_HEADS_A, HEAD_DIM))
        outs[2].append(cbk.reshape(b_p, WINDOW_B, N_KV_B, HEAD_DIM))
        outs[3].append(cbv.reshape(b_p, WINDOW_B, N_KV_B, HEAD_DIM))

        qa, ka, va, qb, kb, vb = _in_proj(hs, g_mix, w_in_b, rope_s, tm=tm_s, layer=i)
        o, *rolled = _attn_sample(sinks, qa, cak_s, ka, cav_s, va, qb, cbk_s, kb, cbv_s, vb,
                                  rbv, goa, gob, t_s=t_s, keep_a=keep_a_s, keep_b=keep_b_s,
                                  layer=i, prev_new=new_rows, emit=final)
        new_rows.append((ka, va, kb, vb))
        hs = _tail(hs, o, p_sample_f, *tail_w, tm=tm_tail, final=final, layer=i)

    nak, nav, nbk, nbv = rolled
    y_sample = hs.reshape(b_s, t_s, D_MODEL)
    return ((hp, y_sample) + tuple(jnp.stack(o) for o in outs)
            + (nak.reshape(depth, b_s, keep_a_s, N_HEADS_A, HEAD_DIM),
               nav.reshape(depth, b_s, keep_a_s, N_HEADS_A, HEAD_DIM),
               nbk.reshape(depth, b_s, keep_b_s, N_KV_B, HEAD_DIM),
               nbv.reshape(depth, b_s, keep_b_s, N_KV_B, HEAD_DIM)))
```

```python
import functools
import math

import numpy as np
import jax
import jax.numpy as jnp
from jax import lax
from jax.experimental import pallas as pl
from jax.experimental.pallas import tpu as pltpu

D_MODEL = 1024
CHUNK = 64
HEAD_DIM = 64
N_HEADS_A = 8
N_HEADS_B = 8
N_KV_B = 2
GROUP_B = N_HEADS_B // N_KV_B
WIDTH_A = N_HEADS_A * HEAD_DIM
WIDTH_B = N_HEADS_B * HEAD_DIM
KV_WIDTH_B = N_KV_B * HEAD_DIM
MIX_WIDTH = WIDTH_A + WIDTH_B
IN_WIDTH = 3 * WIDTH_A + WIDTH_B + 2 * KV_WIDTH_B
PREV_CHUNKS_A = 8
A_ROWS = PREV_CHUNKS_A * CHUNK
REL_CLIP = 256
WINDOW_B = 128
PREV_CHUNKS_B = WINDOW_B // CHUNK
ROT_DIM = HEAD_DIM // 4
ROPE_THETA = 500000.0
D_FF = 2816
D_PLE = 256
PAST_LEN = 2048
RMS_EPS = 1e-6
LOG2E = math.log2(math.e)
QSCALE = HEAD_DIM ** -0.5 * LOG2E
PAIR = 2 * HEAD_DIM
N_PAIRS_A = WIDTH_A // PAIR
N_PAIRS_B = WIDTH_B // PAIR
PAIRS_PER_KV = GROUP_B // 2
NEG_INF = float("-inf")
BIAS_PERIOD = 1024

V7X_VMEM_LIMIT_BYTES = 56 * 1024 * 1024

BF16 = jnp.bfloat16
F32 = jnp.float32

O_KA, O_VA, O_QB = WIDTH_A, 2 * WIDTH_A, 3 * WIDTH_A
O_KB = O_QB + WIDTH_B
O_VB = O_KB + KV_WIDTH_B


def _rms(x, g):
    return x * lax.rsqrt(jnp.mean(x * x, axis=-1, keepdims=True) + RMS_EPS) * g


def _dot(a, b):
    return jnp.dot(a, b, preferred_element_type=F32)


def _dot_nt(a, b):
    return lax.dot_general(a, b, (((1,), (1,)), ((), ())), preferred_element_type=F32)


def _resident(shape, layer=None):
    nd = len(shape)
    if layer is None:
        return pl.BlockSpec(shape, lambda *_: (0,) * nd, pipeline_mode=pl.Buffered(1))
    return pl.BlockSpec((None,) + tuple(shape), lambda *_: (layer,) + (0,) * nd,
                        pipeline_mode=pl.Buffered(1))


def _rope(x, cos, sdn, sup):
    half = ROT_DIM // 2
    return (x * cos + pltpu.roll(x, PAIR - half, axis=1) * sdn
            + pltpu.roll(x, half, axis=1) * sup)


def _low_lanes(shape):
    return lax.broadcasted_iota(jnp.int32, shape, 1) < HEAD_DIM


def _keep_half(x, half, fill):
    keep = _low_lanes(x.shape) if half == 0 else ~_low_lanes(x.shape)
    return jnp.where(keep, x, jnp.full_like(x, fill))


def _exp_scores(s_blocks, sink=None):
    if len({s.shape for s in s_blocks}) == 1:
        m = jnp.max(functools.reduce(jnp.maximum, s_blocks), axis=-1, keepdims=True)
    else:
        m = functools.reduce(jnp.maximum, [jnp.max(s, axis=-1, keepdims=True) for s in s_blocks])
    if sink is not None:
        m = jnp.maximum(m, sink)
    p = jnp.concatenate([jnp.exp2(s - m).astype(BF16) for s in s_blocks], axis=-1)
    return p, (None if sink is None else jnp.exp2(sink - m))


def _pv_stacked(probs, v_ext, extras=None):
    m = probs[0].shape[0]
    x = _dot(jnp.concatenate(probs, axis=0), v_ext)
    outs = []
    for i in range(len(probs)):
        den = x[i * m:(i + 1) * m, PAIR:]
        if extras is not None:
            den = den + extras[i]
        outs.append(x[i * m:(i + 1) * m, :PAIR] / den)
    return outs


def _rolled_bias_rows(rbv_ref, h, rows):
    x = jnp.broadcast_to(rbv_ref[h:h + 1, :] * LOG2E, (rows, BIAS_PERIOD))
    return pltpu.roll(x, 0, axis=1, stride=1, stride_axis=0)


def _band_valid(rows, col0, cols, prev_chunks):
    r = lax.broadcasted_iota(jnp.int32, (rows, cols), 0) // CHUNK
    c = (lax.broadcasted_iota(jnp.int32, (rows, cols), 1) + col0) // CHUNK
    return (c >= r) & (c <= r + prev_chunks)


TQ = 256
NBLK_A = A_ROWS // TQ + 1
WIN_A = NBLK_A * TQ
WIN_B = WINDOW_B + TQ
NEG_KIND = NBLK_A


def _mix_prompt_kernel(sinks_ref, x_ref, g_ref, w_ref, rope_ref, ct_ref, st_ref, rbv_ref,
                       goa_ref, gob_ref,
                       o_ref, cak_ref, cav_ref, cbk_ref, cbv_ref,
                       ka_scr, va_scr, kbp_scr, vbp_scr, bias_scr, maskb_scr):
    b, t = pl.program_id(0), pl.program_id(1)

    @pl.when((b == 0) & (t == 0))
    def _init():
        ka_scr[...] = jnp.zeros_like(ka_scr)
        kbp_scr[...] = jnp.zeros_like(kbp_scr)
        for v_scr in (va_scr, vbp_scr):
            v_scr[..., :PAIR] = jnp.zeros(v_scr.shape[:-1] + (PAIR,), BF16)
            v_scr[..., PAIR:] = jnp.ones(v_scr.shape[:-1] + (PAIR,), BF16)
        for h in range(N_HEADS_A):
            rows = _rolled_bias_rows(rbv_ref, h, TQ)
            for j in range(NBLK_A):
                valid = _band_valid(TQ, j * TQ, TQ, PREV_CHUNKS_A)
                bias_scr[j, h] = jnp.where(valid, rows[:, j * TQ:(j + 1) * TQ], NEG_INF)
            bias_scr[NEG_KIND, h] = jnp.full((TQ, TQ), NEG_INF, F32)
        band = jnp.where(_band_valid(TQ, 0, WIN_B, PREV_CHUNKS_B), jnp.zeros((TQ, WIN_B), F32), NEG_INF)
        col = lax.broadcasted_iota(jnp.int32, (TQ, WIN_B), 1)
        maskb_scr[0] = band
        maskb_scr[1] = jnp.where(col >= WINDOW_B, band, NEG_INF)

    n = _rms(x_ref[0], g_ref[...]).astype(BF16)
    slot = lax.rem(t, NBLK_A)
    row0 = pl.multiple_of(slot * TQ, TQ)
    ka = _dot(n, w_ref[:, O_KA:O_VA])
    for p in range(N_PAIRS_A):
        ka_scr[p, pl.ds(row0, TQ), :] = ka[:, p * PAIR:(p + 1) * PAIR].astype(BF16)
    za = _dot(n, w_ref[:, :O_KA])
    qa = [[_keep_half(za[:, j * PAIR:(j + 1) * PAIR] * QSCALE, half, 0.0).astype(BF16)
           for half in range(2)] for j in range(N_PAIRS_A)]
    va = _dot(n, w_ref[:, O_VA:O_QB])
    for p in range(N_PAIRS_A):
        va_scr[p, pl.ds(row0, TQ), :PAIR] = va[:, p * PAIR:(p + 1) * PAIR].astype(BF16)
    zb = _dot(n, w_ref[:, O_QB:])
    ct, st = ct_ref[pl.ds(t, 1), :], st_ref[pl.ds(t, 1), :]
    cos = ct * rope_ref[0] - st * rope_ref[1]
    sdn = st * rope_ref[2] + ct * rope_ref[3]
    sup = st * rope_ref[4] + ct * rope_ref[5]
    kb = _rope(zb[:, WIDTH_B:WIDTH_B + KV_WIDTH_B], cos, sdn, sup)
    vb = zb[:, WIDTH_B + KV_WIDTH_B:]
    qb = []
    for j in range(N_PAIRS_B):
        q = _rope(zb[:, j * PAIR:(j + 1) * PAIR], cos, sdn, sup) * QSCALE
        qb.append([_keep_half(q, half, 0.0).astype(BF16) for half in range(2)])

    cak_ref[0] = ka
    cav_ref[0] = va
    cbk_ref[0] = kb[TQ - WINDOW_B:, :]
    cbv_ref[0] = vb[TQ - WINDOW_B:, :]

    kinds = []
    for ps in range(NBLK_A):
        d = lax.rem(t - ps + NBLK_A, NBLK_A)
        kinds.append(jnp.where(t >= d, NBLK_A - 1 - d, NEG_KIND))
    low = _low_lanes((TQ, PAIR))
    oa = []
    for p in range(N_PAIRS_A):
        s = _dot_nt(jnp.concatenate(qa[p], axis=0), ka_scr[p])
        probs = []
        for half in range(2):
            s_blocks = [s[half * TQ:(half + 1) * TQ, ps * TQ:(ps + 1) * TQ]
                        + bias_scr[kinds[ps], 2 * p + half] for ps in range(NBLK_A)]
            probs.append(_exp_scores(s_blocks)[0])
        o = _pv_stacked(probs, va_scr[p])
        oa.append(jnp.where(low, o[0], o[1]))
    oa = jnp.concatenate(oa, axis=-1)

    kb_sw, vb_sw = pltpu.roll(kb, HEAD_DIM, axis=1), pltpu.roll(vb, HEAD_DIM, axis=1)
    ones = jnp.ones((TQ, PAIR), BF16)
    slot_b = lax.rem(t, 2)
    maskb = maskb_scr[jnp.where(t >= 1, 0, 1)]
    ob_half = [[None, None] for _ in range(N_PAIRS_B)]
    for c, (k_c, v_c) in enumerate(((kb, vb), (kb_sw, vb_sw))):
        k_c, v_c = k_c.astype(BF16), v_c.astype(BF16)
        kbp_scr[c, slot_b] = k_c[TQ - WINDOW_B:, :]
        vbp_scr[c, slot_b, :, :PAIR] = v_c[TQ - WINDOW_B:, :]
        k_win = jnp.concatenate([kbp_scr[c, 1 - slot_b], k_c], axis=0)
        v_win = jnp.concatenate([vbp_scr[c, 1 - slot_b], jnp.concatenate([v_c, ones], axis=1)], axis=0)
        heads = [(p, half) for p in range(N_PAIRS_B) for half in range(2)
                 if (half == p // PAIRS_PER_KV) == (c == 0)]
        s = _dot_nt(jnp.concatenate([qb[p][half] for p, half in heads], axis=0), k_win)
        probs, extras = [], []
        for i, (p, half) in enumerate(heads):
            pr, ex = _exp_scores([s[i * TQ:(i + 1) * TQ, :] + maskb], sinks_ref[2 * p + half] * LOG2E)
            probs.append(pr)
            extras.append(ex)
        for (p, half), o in zip(heads, _pv_stacked(probs, v_win, extras)):
            ob_half[p][half] = o
    ob = jnp.concatenate([jnp.where(low, o[0], o[1]) for o in ob_half], axis=-1)

    o_ref[0, :, :WIDTH_A] = _rms(oa, goa_ref[...]).astype(BF16)
    o_ref[0, :, WIDTH_A:] = _rms(ob, gob_ref[...]).astype(BF16)


def _mix_prompt(sinks, x, g, w_bf16, rope_rows, ct, st, rbv, goa, gob, *, layer):
    b, s, _ = x.shape
    nt = s // TQ
    cur = lambda bi, t: (bi, t, 0)
    keep_a = lambda bi, t: (bi, jnp.maximum(t - (nt - A_ROWS // TQ), 0), 0)
    seq = lambda bi, t: (bi, 0, 0)
    return pl.pallas_call(
        _mix_prompt_kernel,
        grid=(b, nt),
        in_specs=[
            pl.BlockSpec(memory_space=pltpu.SMEM),
            pl.BlockSpec((1, TQ, D_MODEL), cur),
            _resident((1, D_MODEL)),
            _resident((D_MODEL, IN_WIDTH), layer),
            _resident(rope_rows.shape),
            _resident(ct.shape),
            _resident(st.shape),
            _resident((N_HEADS_A, BIAS_PERIOD)),
            _resident((1, WIDTH_A)),
            _resident((1, WIDTH_B)),
        ],
        out_specs=[
            pl.BlockSpec((1, TQ, MIX_WIDTH), cur),
            pl.BlockSpec((1, TQ, WIDTH_A), keep_a),
            pl.BlockSpec((1, TQ, WIDTH_A), keep_a),
            pl.BlockSpec((1, WINDOW_B, KV_WIDTH_B), seq),
            pl.BlockSpec((1, WINDOW_B, KV_WIDTH_B), seq),
        ],
        out_shape=[
            jax.ShapeDtypeStruct((b, s, MIX_WIDTH), BF16),
            jax.ShapeDtypeStruct((b, A_ROWS, WIDTH_A), F32),
            jax.ShapeDtypeStruct((b, A_ROWS, WIDTH_A), F32),
            jax.ShapeDtypeStruct((b, WINDOW_B, KV_WIDTH_B), F32),
            jax.ShapeDtypeStruct((b, WINDOW_B, KV_WIDTH_B), F32),
        ],
        scratch_shapes=[
            pltpu.VMEM((N_PAIRS_A, WIN_A, PAIR), BF16),
            pltpu.VMEM((N_PAIRS_A, WIN_A, 2 * PAIR), BF16),
            pltpu.VMEM((2, 2, WINDOW_B, PAIR), BF16),
            pltpu.VMEM((2, 2, WINDOW_B, 2 * PAIR), BF16),
            pltpu.VMEM((NBLK_A + 1, N_HEADS_A, TQ, TQ), F32),
            pltpu.VMEM((2, TQ, WIN_B), F32),
        ],
        compiler_params=pltpu.CompilerParams(
            dimension_semantics=("arbitrary", "arbitrary"),
            vmem_limit_bytes=V7X_VMEM_LIMIT_BYTES),
        name="mix_prompt",
    )(sinks, x, g, w_bf16, rope_rows, ct, st, rbv, goa, gob)


def _in_proj_kernel(x_ref, g_ref, w_ref, rope_ref, qa_ref, ka_ref, va_ref, qb_ref, kb_ref, vb_ref):
    n = _rms(x_ref[...], g_ref[...]).astype(BF16)
    z = _dot(n, w_ref[...])
    cos, sdn, sup = rope_ref[0], rope_ref[1], rope_ref[2]
    qa_ref[...] = (z[:, :O_KA] * QSCALE).astype(BF16)
    ka_ref[...] = z[:, O_KA:O_VA]
    va_ref[...] = z[:, O_VA:O_QB]
    for j in range(N_PAIRS_B):
        sl = slice(O_QB + j * PAIR, O_QB + (j + 1) * PAIR)
        qb_ref[:, j * PAIR:(j + 1) * PAIR] = (_rope(z[:, sl], cos, sdn, sup) * QSCALE).astype(BF16)
    kb_ref[...] = _rope(z[:, O_KB:O_VB], cos, sdn, sup)
    vb_ref[...] = z[:, O_VB:]


def _in_proj(x, g, w_bf16, rope_tabs, *, tm, layer):
    n_tok = x.shape[0]
    row = lambda i: (i, 0)
    widths = (WIDTH_A, WIDTH_A, WIDTH_A, WIDTH_B, KV_WIDTH_B, KV_WIDTH_B)
    dtypes = (BF16, F32, F32, BF16, F32, F32)
    return pl.pallas_call(
        _in_proj_kernel,
        grid=(n_tok // tm,),
        in_specs=[
            pl.BlockSpec((tm, D_MODEL), row),
            _resident((1, D_MODEL)),
            _resident((D_MODEL, IN_WIDTH), layer),
            _resident(rope_tabs.shape),
        ],
        out_specs=[pl.BlockSpec((tm, w), row) for w in widths],
        out_shape=[jax.ShapeDtypeStruct((n_tok, w), d) for w, d in zip(widths, dtypes)],
        compiler_params=pltpu.CompilerParams(
            dimension_semantics=("arbitrary",), vmem_limit_bytes=V7X_VMEM_LIMIT_BYTES),
        name="in_proj",
    )(x, g, w_bf16, rope_tabs)


N_SAMPLE_IN = 14
N_ROLL_IN = 8


def _attn_sample_kernel(*refs, n_prev, emit):
    (sinks_ref, qa_ref, kac_ref, kan_ref, vac_ref, van_ref, qb_ref, kbc_ref, kbn_ref, vbc_ref,
     vbn_ref, rbv_ref, goa_ref, gob_ref) = refs[:N_SAMPLE_IN]
    n_in = N_SAMPLE_IN + N_ROLL_IN * n_prev
    o_ref, bias_scr = refs[n_in], refs[-1]
    t_s = qa_ref.shape[0]
    la = kac_ref.shape[1]

    @pl.when(pl.program_id(0) == 0)
    def _init():
        for h in range(N_HEADS_A):
            bias_scr[h] = _rolled_bias_rows(rbv_ref, h, t_s)

    if emit:
        own = (kac_ref, kan_ref, vac_ref, van_ref, kbc_ref, kbn_ref, vbc_ref, vbn_ref)
        layers = [refs[N_SAMPLE_IN + N_ROLL_IN * l:N_SAMPLE_IN + N_ROLL_IN * (l + 1)]
                  for l in range(n_prev)] + [own]
        for l, lr in enumerate(layers):
            for j, new_ref in enumerate(refs[n_in + 1:n_in + 5]):
                cache_ref, fresh_ref = lr[2 * j], lr[2 * j + 1]
                kept = new_ref.shape[2] - t_s
                new_ref[l, 0, :kept, :] = cache_ref[0, cache_ref.shape[1] - kept:, :]
                new_ref[l, 0, kept:, :] = fresh_ref[...]

    low = _low_lanes((t_s, PAIR))
    ones_a = jnp.ones((la + t_s, PAIR), BF16)
    oa = []
    for p in range(N_PAIRS_A):
        sl = slice(p * PAIR, (p + 1) * PAIR)
        qp = qa_ref[:, sl]
        k = jnp.concatenate([kac_ref[0, :, sl], kan_ref[:, sl]], axis=0).astype(BF16)
        v = jnp.concatenate([vac_ref[0, :, sl], van_ref[:, sl]], axis=0).astype(BF16)
        s = _dot_nt(jnp.concatenate([_keep_half(qp, half, 0.0) for half in range(2)], axis=0), k)
        probs = [_exp_scores([s[half * t_s:(half + 1) * t_s, :] + bias_scr[2 * p + half, :, :la + t_s]])[0]
                 for half in range(2)]
        o = _pv_stacked(probs, jnp.concatenate([v, ones_a], axis=1))
        oa.append(jnp.where(low, o[0], o[1]))
    oa = jnp.concatenate(oa, axis=-1)

    kb = jnp.concatenate([kbc_ref[0], kbn_ref[...]], axis=0)
    vb = jnp.concatenate([vbc_ref[0], vbn_ref[...]], axis=0)
    ones_b = jnp.ones((kb.shape[0], PAIR), BF16)
    ob_half = [[None, None] for _ in range(N_PAIRS_B)]
    for c in range(2):
        k_c = (kb if c == 0 else pltpu.roll(kb, HEAD_DIM, axis=1)).astype(BF16)
        v_c = (vb if c == 0 else pltpu.roll(vb, HEAD_DIM, axis=1)).astype(BF16)
        heads = [(p, half) for p in range(N_PAIRS_B) for half in range(2)
                 if (half == p // PAIRS_PER_KV) == (c == 0)]
        s = _dot_nt(jnp.concatenate([_keep_half(qb_ref[:, p * PAIR:(p + 1) * PAIR], half, 0.0)
                                     for p, half in heads], axis=0), k_c)
        probs, extras = [], []
        for i, (p, half) in enumerate(heads):
            pr, ex = _exp_scores([s[i * t_s:(i + 1) * t_s, :]], sinks_ref[2 * p + half] * LOG2E)
            probs.append(pr)
            extras.append(ex)
        for (p, half), o in zip(heads, _pv_stacked(probs, jnp.concatenate([v_c, ones_b], axis=1), extras)):
            ob_half[p][half] = o
    ob = jnp.concatenate([jnp.where(low, o[0], o[1]) for o in ob_half], axis=-1)
    o_ref[:, :WIDTH_A] = _rms(oa, goa_ref[...]).astype(BF16)
    o_ref[:, WIDTH_A:] = _rms(ob, gob_ref[...]).astype(BF16)


def _attn_sample(sinks, qa, ka_cache, ka, va_cache, va, qb, kb_cache, kb, vb_cache, vb,
                 rbv, goa, gob, *, t_s, keep_a, keep_b, layer, prev_new, emit):
    n_tok = qa.shape[0]
    n_seq = n_tok // t_s
    la, lb = ka_cache.shape[2], kb_cache.shape[2]
    assert t_s <= keep_a <= la + t_s and t_s <= keep_b <= lb + t_s
    row = lambda i: (i, 0)
    cache = lambda l, rows, width: pl.BlockSpec((None, 1, rows, width), lambda i: (l, i, 0, 0))
    new = lambda width: pl.BlockSpec((t_s, width), row)

    def kv_specs(l):
        return [cache(l, la, WIDTH_A), new(WIDTH_A), cache(l, la, WIDTH_A), new(WIDTH_A),
                cache(l, lb, KV_WIDTH_B), new(KV_WIDTH_B), cache(l, lb, KV_WIDTH_B), new(KV_WIDTH_B)]

    own = kv_specs(layer)
    in_specs = ([pl.BlockSpec(memory_space=pltpu.SMEM), new(WIDTH_A)] + own[:4] + [new(WIDTH_B)] + own[4:]
                + [_resident((N_HEADS_A, BIAS_PERIOD)), _resident((1, WIDTH_A)), _resident((1, WIDTH_B))])
    operands = [sinks, qa, ka_cache, ka, va_cache, va, qb, kb_cache, kb, vb_cache, vb, rbv, goa, gob]
    out_specs = [pl.BlockSpec((t_s, MIX_WIDTH), row)]
    out_shape = [jax.ShapeDtypeStruct((n_tok, MIX_WIDTH), BF16)]
    n_prev = 0
    if emit:
        n_prev = len(prev_new)
        for l, (ka_l, va_l, kb_l, vb_l) in enumerate(prev_new):
            in_specs += kv_specs(l)
            operands += [ka_cache, ka_l, va_cache, va_l, kb_cache, kb_l, vb_cache, vb_l]
        n_layers = n_prev + 1
        for keep, width in ((keep_a, WIDTH_A), (keep_a, WIDTH_A), (keep_b, KV_WIDTH_B), (keep_b, KV_WIDTH_B)):
            out_specs.append(pl.BlockSpec((n_layers, 1, keep, width), lambda i: (0, i, 0, 0)))
            out_shape.append(jax.ShapeDtypeStruct((n_layers, n_seq, keep, width), F32))
    return pl.pallas_call(
        functools.partial(_attn_sample_kernel, n_prev=n_prev, emit=emit),
        grid=(n_seq,),
        in_specs=in_specs,
        out_specs=out_specs,
        out_shape=out_shape,
        scratch_shapes=[pltpu.VMEM((N_HEADS_A, t_s, BIAS_PERIOD), F32)],
        compiler_params=pltpu.CompilerParams(
            dimension_semantics=("arbitrary",), vmem_limit_bytes=V7X_VMEM_LIMIT_BYTES),
        name="attn_sample",
    )(*operands)


def _tail_kernel(h_ref, o_ref, p_ref, wout_ref, gffn_ref, wgu_ref, wdown_ref, wgate_ref,
                 wproj_ref, gfin_ref, out_ref, *, final, n_sub):
    rows = h_ref.shape[0] // n_sub
    subs = [pl.ds(i * rows, rows) for i in range(n_sub)]
    h = [h_ref[sl, :] + _dot(o_ref[sl, :], wout_ref[...]) for sl in subs]
    pp = [_dot(p_ref[sl, :].astype(BF16), wproj_ref[...]) for sl in subs]
    gu = [_dot(_rms(x, gffn_ref[...]).astype(BF16), wgu_ref[...]) for x in h]
    act = [(jax.nn.silu(x[:, :D_FF]) * x[:, D_FF:]).astype(BF16) for x in gu]
    h = [x + _dot(a, wdown_ref[...]) for x, a in zip(h, act)]
    gate = [jax.nn.sigmoid(_dot(x.astype(BF16), wgate_ref[...])) for x in h]
    for sl, x, g, e in zip(subs, h, gate, pp):
        x = x + g * e
        if final:
            x = _rms(x, gfin_ref[...])
        out_ref[sl, :] = x


def _tail(h, o, p, wout, gffn, wgu, wdown, wgate, wproj, gfin, *, tm, final, layer):
    n_tok = h.shape[0]
    row = lambda i: (i, 0)
    return pl.pallas_call(
        functools.partial(_tail_kernel, final=final, n_sub=2),
        grid=(n_tok // tm,),
        in_specs=[
            pl.BlockSpec((tm, D_MODEL), row),
            pl.BlockSpec((tm, MIX_WIDTH), row),
            pl.BlockSpec((None, tm, D_PLE), lambda i: (layer, i, 0)),
            _resident((MIX_WIDTH, D_MODEL), layer),
            _resident((1, D_MODEL)),
            _resident((D_MODEL, 2 * D_FF), layer),
            _resident((D_FF, D_MODEL), layer),
            _resident((D_MODEL, D_MODEL), layer),
            _resident((D_PLE, D_MODEL), layer),
            _resident((1, D_MODEL)),
        ],
        out_specs=pl.BlockSpec((tm, D_MODEL), row),
        out_shape=jax.ShapeDtypeStruct((n_tok, D_MODEL), F32),
        compiler_params=pltpu.CompilerParams(
            dimension_semantics=("arbitrary",), vmem_limit_bytes=V7X_VMEM_LIMIT_BYTES),
        name="layer_tail",
    )(h, o, p, wout, gffn, wgu, wdown, wgate, wproj, gfin)


CAST_ROWS = 256


def _cast_kernel(w_ref, o_ref):
    o_ref[...] = w_ref[...].astype(o_ref.dtype)


def _to_bf16(w):
    depth, rows, cols = w.shape
    blk = pl.BlockSpec((1, CAST_ROWS, cols), lambda d, r: (d, r, 0))
    return pl.pallas_call(
        _cast_kernel,
        grid=(depth, rows // CAST_ROWS),
        in_specs=[blk],
        out_specs=blk,
        out_shape=jax.ShapeDtypeStruct(w.shape, BF16),
        compiler_params=pltpu.CompilerParams(dimension_semantics=("arbitrary", "arbitrary")),
        name="cast_bf16",
    )(w)


def _rope_lane_tables(pos):
    half = ROT_DIM // 2
    d = np.arange(PAIR) % HEAD_DIM
    inv = ROPE_THETA ** (-(2.0 * (d % half)) / ROT_DIM)
    ang = np.where(d < ROT_DIM, np.asarray(pos, np.float64)[:, None] * inv[None, :], 0.0)
    lower = (d < half).astype(np.float64)[None, :]
    upper = ((d >= half) & (d < ROT_DIM)).astype(np.float64)[None, :]
    return np.cos(ang), np.sin(ang), lower, upper


def _rope_tables_direct(pos):
    cos, sin, lower, upper = _rope_lane_tables(pos)
    return jnp.asarray(np.stack([cos, -sin * lower, sin * upper]), F32)


def _rope_tables_split(n_tiles, tile):
    cr, sr, lower, upper = _rope_lane_tables(np.arange(tile))
    ct, st, _, _ = _rope_lane_tables(np.arange(n_tiles) * tile)
    rows = np.stack([cr, sr, -cr * lower, -sr * lower, cr * upper, sr * upper])
    return jnp.asarray(rows, F32), jnp.asarray(ct, F32), jnp.asarray(st, F32)


def _rel_bias_row(rel_bias):
    u = np.arange(BIAS_PERIOD)
    diff = np.where(u < BIAS_PERIOD // 2 + A_ROWS // 2, u, u - BIAS_PERIOD)
    idx = np.clip(A_ROWS - diff, -REL_CLIP, REL_CLIP) + REL_CLIP
    return rel_bias.astype(F32)[:, idx]


def kernel(x_prompt, x_sample, p_prompt, p_sample, cache_a_k, cache_a_v, cache_b_k, cache_b_v,
           g_mix_norm, w_in, rel_bias_a, sinks_b, g_out_a, g_out_b, w_out, g_ffn_norm,
           w_gate_up, w_down, w_ple_proj, w_ple_gate, g_final):
    b_p, s_p, _ = x_prompt.shape
    b_s, t_s, _ = x_sample.shape
    depth = w_in.shape[0]
    la_c, lb_c = cache_a_k.shape[2], cache_b_k.shape[2]
    keep_a_s = min(A_ROWS, la_c + t_s)
    keep_b_s = min(WINDOW_B, lb_c + t_s)
    tm_s = 512
    tm_tail = 512
    assert s_p % TQ == 0 and s_p >= A_ROWS and (b_s * t_s) % tm_s == 0 and tm_s % t_s == 0
    assert la_c == A_ROWS and lb_c == WINDOW_B and t_s == CHUNK

    rope_rows, rope_ct, rope_st = _rope_tables_split(s_p // TQ, TQ)
    rope_s = _rope_tables_direct(PAST_LEN + np.arange(tm_s) % t_s)
    row2 = lambda a: a.reshape(1, -1).astype(F32)
    g_fin = row2(g_final)

    hp = x_prompt
    hs = x_sample.reshape(b_s * t_s, D_MODEL)
    outs = [[] for _ in range(4)]
    new_rows = []
    w_in_b, w_out_b, w_gu_b, w_down_b, w_gate_b, w_proj_b = (
        _to_bf16(w) for w in (w_in, w_out, w_gate_up, w_down, w_ple_gate, w_ple_proj))
    p_prompt_f = p_prompt.reshape(depth, b_p * s_p, D_PLE)
    p_sample_f = p_sample.reshape(depth, b_s * t_s, D_PLE)
    cak_s = cache_a_k.reshape(depth, b_s, la_c, WIDTH_A)
    cav_s = cache_a_v.reshape(depth, b_s, la_c, WIDTH_A)
    cbk_s = cache_b_k.reshape(depth, b_s, lb_c, KV_WIDTH_B)
    cbv_s = cache_b_v.reshape(depth, b_s, lb_c, KV_WIDTH_B)
    for i in range(depth):
        tail_w = (w_out_b, row2(g_ffn_norm[i]), w_gu_b, w_down_b, w_gate_b, w_proj_b, g_fin)
        g_mix = row2(g_mix_norm[i])
        goa, gob = row2(g_out_a[i]), row2(g_out_b[i])
        sinks = sinks_b[i].astype(F32)
        rbv = _rel_bias_row(rel_bias_a[i])
        final = i == depth - 1

        o, cak, cav, cbk, cbv = _mix_prompt(sinks, hp, g_mix, w_in_b, rope_rows, rope_ct, rope_st,
                                            rbv, goa, gob, layer=i)
        hp = _tail(hp.reshape(b_p * s_p, D_MODEL), o.reshape(b_p * s_p, MIX_WIDTH),
                   p_prompt_f, *tail_w, tm=tm_tail, final=final, layer=i)
        hp = hp.reshape(b_p, s_p, D_MODEL)
        outs[0].append(cak.reshape(b_p, A_ROWS, N_HEADS_A, HEAD_DIM))
        outs[1].append(cav.reshape(b_p, A_ROWS, N_HEADS_A, HEAD_DIM))
        outs[2].append(cbk.reshape(b_p, WINDOW_B, N_KV_B, HEAD_DIM))
        outs[3].append(cbv.reshape(b_p, WINDOW_B, N_KV_B, HEAD_DIM))

        qa, ka, va, qb, kb, vb = _in_proj(hs, g_mix, w_in_b, rope_s, tm=tm_s, layer=i)
        o, *rolled = _attn_sample(sinks, qa, cak_s, ka, cav_s, va, qb, cbk_s, kb, cbv_s, vb,
                                  rbv, goa, gob, t_s=t_s, keep_a=keep_a_s, keep_b=keep_b_s,
                                  layer=i, prev_new=new_rows, emit=final)
        new_rows.append((ka, va, kb, vb))
        hs = _tail(hs, o, p_sample_f, *tail_w, tm=tm_tail, final=final, layer=i)

    nak, nav, nbk, nbv = rolled
    y_sample = hs.reshape(b_s, t_s, D_MODEL)
    return ((hp, y_sample) + tuple(jnp.stack(o) for o in outs)
            + (nak.reshape(depth, b_s, keep_a_s, N_HEADS_A, HEAD_DIM),
               nav.reshape(depth, b_s, keep_a_s, N_HEADS_A, HEAD_DIM),
               nbk.reshape(depth, b_s, keep_b_s, N_KV_B, HEAD_DIM),
               nbv.reshape(depth, b_s, keep_b_s, N_KV_B, HEAD_DIM)))
```

```python
import functools
import math

import numpy as np
import jax
import jax.numpy as jnp
from jax import lax
from jax.experimental import pallas as pl
from jax.experimental.pallas import tpu as pltpu

D_MODEL = 1024
CHUNK = 64
HEAD_DIM = 64
N_HEADS_A = 8
N_HEADS_B = 8
N_KV_B = 2
GROUP_B = N_HEADS_B // N_KV_B
WIDTH_A = N_HEADS_A * HEAD_DIM
WIDTH_B = N_HEADS_B * HEAD_DIM
KV_WIDTH_B = N_KV_B * HEAD_DIM
MIX_WIDTH = WIDTH_A + WIDTH_B
IN_WIDTH = 3 * WIDTH_A + WIDTH_B + 2 * KV_WIDTH_B
PREV_CHUNKS_A = 8
A_ROWS = PREV_CHUNKS_A * CHUNK
REL_CLIP = 256
WINDOW_B = 128
PREV_CHUNKS_B = WINDOW_B // CHUNK
ROT_DIM = HEAD_DIM // 4
ROPE_THETA = 500000.0
D_FF = 2816
D_PLE = 256
PAST_LEN = 2048
RMS_EPS = 1e-6
LOG2E = math.log2(math.e)
QSCALE = HEAD_DIM ** -0.5 * LOG2E
PAIR = 2 * HEAD_DIM
N_PAIRS_A = WIDTH_A // PAIR
N_PAIRS_B = WIDTH_B // PAIR
PAIRS_PER_KV = GROUP_B // 2
NEG_INF = float("-inf")
BIAS_PERIOD = 1024

V7X_VMEM_LIMIT_BYTES = 56 * 1024 * 1024

BF16 = jnp.bfloat16
F32 = jnp.float32

O_KA, O_VA, O_QB = WIDTH_A, 2 * WIDTH_A, 3 * WIDTH_A
O_KB = O_QB + WIDTH_B
O_VB = O_KB + KV_WIDTH_B


def _rms(x, g):
    return x * lax.rsqrt(jnp.mean(x * x, axis=-1, keepdims=True) + RMS_EPS) * g


def _dot(a, b):
    return jnp.dot(a, b, preferred_element_type=F32)


def _dot_nt(a, b):
    return lax.dot_general(a, b, (((1,), (1,)), ((), ())), preferred_element_type=F32)


def _resident(shape, layer=None):
    nd = len(shape)
    if layer is None:
        return pl.BlockSpec(shape, lambda *_: (0,) * nd, pipeline_mode=pl.Buffered(1))
    return pl.BlockSpec((None,) + tuple(shape), lambda *_: (layer,) + (0,) * nd,
                        pipeline_mode=pl.Buffered(1))


def _rope(x, cos, sdn, sup):
    half = ROT_DIM // 2
    return (x * cos + pltpu.roll(x, PAIR - half, axis=1) * sdn
            + pltpu.roll(x, half, axis=1) * sup)


def _low_lanes(shape):
    return lax.broadcasted_iota(jnp.int32, shape, 1) < HEAD_DIM


def _keep_half(x, half, fill):
    keep = _low_lanes(x.shape) if half == 0 else ~_low_lanes(x.shape)
    return jnp.where(keep, x, jnp.full_like(x, fill))


def _exp_scores(s_blocks, sink=None):
    if len({s.shape for s in s_blocks}) == 1:
        m = jnp.max(functools.reduce(jnp.maximum, s_blocks), axis=-1, keepdims=True)
    else:
        m = functools.reduce(jnp.maximum, [jnp.max(s, axis=-1, keepdims=True) for s in s_blocks])
    if sink is not None:
        m = jnp.maximum(m, sink)
    p = jnp.concatenate([jnp.exp2(s - m).astype(BF16) for s in s_blocks], axis=-1)
    return p, (None if sink is None else jnp.exp2(sink - m))


def _pv_stacked(probs, v_ext, extras=None):
    m = probs[0].shape[0]
    x = _dot(jnp.concatenate(probs, axis=0), v_ext)
    outs = []
    for i in range(len(probs)):
        den = x[i * m:(i + 1) * m, PAIR:]
        if extras is not None:
            den = den + extras[i]
        outs.append(x[i * m:(i + 1) * m, :PAIR] / den)
    return outs


def _rolled_bias_rows(rbv_ref, h, rows):
    x = jnp.broadcast_to(rbv_ref[h:h + 1, :] * LOG2E, (rows, BIAS_PERIOD))
    return pltpu.roll(x, 0, axis=1, stride=1, stride_axis=0)


def _band_valid(rows, col0, cols, prev_chunks):
    r = lax.broadcasted_iota(jnp.int32, (rows, cols), 0) // CHUNK
    c = (lax.broadcasted_iota(jnp.int32, (rows, cols), 1) + col0) // CHUNK
    return (c >= r) & (c <= r + prev_chunks)


TQ = 256
NBLK_A = A_ROWS // TQ + 1
WIN_A = NBLK_A * TQ
WIN_B = WINDOW_B + TQ
NEG_KIND = NBLK_A


def _mix_prompt_kernel(sinks_ref, x_ref, g_ref, w_ref, rope_ref, ct_ref, st_ref, rbv_ref,
                       goa_ref, gob_ref,
                       o_ref, cak_ref, cav_ref, cbk_ref, cbv_ref,
                       ka_scr, va_scr, kbp_scr, vbp_scr, bias_scr, maskb_scr):
    b, t = pl.program_id(0), pl.program_id(1)

    @pl.when((b == 0) & (t == 0))
    def _init():
        ka_scr[...] = jnp.zeros_like(ka_scr)
        kbp_scr[...] = jnp.zeros_like(kbp_scr)
        for v_scr in (va_scr, vbp_scr):
            v_scr[..., :PAIR] = jnp.zeros(v_scr.shape[:-1] + (PAIR,), BF16)
            v_scr[..., PAIR:] = jnp.ones(v_scr.shape[:-1] + (PAIR,), BF16)
        for h in range(N_HEADS_A):
            rows = _rolled_bias_rows(rbv_ref, h, TQ)
            for j in range(NBLK_A):
                valid = _band_valid(TQ, j * TQ, TQ, PREV_CHUNKS_A)
                bias_scr[j, h] = jnp.where(valid, rows[:, j * TQ:(j + 1) * TQ], NEG_INF)
            bias_scr[NEG_KIND, h] = jnp.full((TQ, TQ), NEG_INF, F32)
        band = jnp.where(_band_valid(TQ, 0, WIN_B, PREV_CHUNKS_B), jnp.zeros((TQ, WIN_B), F32), NEG_INF)
        col = lax.broadcasted_iota(jnp.int32, (TQ, WIN_B), 1)
        maskb_scr[0] = band
        maskb_scr[1] = jnp.where(col >= WINDOW_B, band, NEG_INF)

    n = _rms(x_ref[0], g_ref[...]).astype(BF16)
    slot = lax.rem(t, NBLK_A)
    row0 = pl.multiple_of(slot * TQ, TQ)
    ka = _dot(n, w_ref[:, O_KA:O_VA])
    for p in range(N_PAIRS_A):
        ka_scr[p, pl.ds(row0, TQ), :] = ka[:, p * PAIR:(p + 1) * PAIR].astype(BF16)
    za = _dot(n, w_ref[:, :O_KA])
    qa = [[_keep_half(za[:, j * PAIR:(j + 1) * PAIR] * QSCALE, half, 0.0).astype(BF16)
           for half in range(2)] for j in range(N_PAIRS_A)]
    va = _dot(n, w_ref[:, O_VA:O_QB])
    for p in range(N_PAIRS_A):
        va_scr[p, pl.ds(row0, TQ), :PAIR] = va[:, p * PAIR:(p + 1) * PAIR].astype(BF16)
    zb = _dot(n, w_ref[:, O_QB:])
    ct, st = ct_ref[pl.ds(t, 1), :], st_ref[pl.ds(t, 1), :]
    cos = ct * rope_ref[0] - st * rope_ref[1]
    sdn = st * rope_ref[2] + ct * rope_ref[3]
    sup = st * rope_ref[4] + ct * rope_ref[5]
    kb = _rope(zb[:, WIDTH_B:WIDTH_B + KV_WIDTH_B], cos, sdn, sup)
    vb = zb[:, WIDTH_B + KV_WIDTH_B:]
    qb = []
    for j in range(N_PAIRS_B):
        q = _rope(zb[:, j * PAIR:(j + 1) * PAIR], cos, sdn, sup) * QSCALE
        qb.append([_keep_half(q, half, 0.0).astype(BF16) for half in range(2)])

    cak_ref[0] = ka
    cav_ref[0] = va
    cbk_ref[0] = kb[TQ - WINDOW_B:, :]
    cbv_ref[0] = vb[TQ - WINDOW_B:, :]

    kinds = []
    for ps in range(NBLK_A):
        d = lax.rem(t - ps + NBLK_A, NBLK_A)
        kinds.append(jnp.where(t >= d, NBLK_A - 1 - d, NEG_KIND))
    low = _low_lanes((TQ, PAIR))
    oa = []
    for p in range(N_PAIRS_A):
        s = _dot_nt(jnp.concatenate(qa[p], axis=0), ka_scr[p])
        probs = []
        for half in range(2):
            s_blocks = [s[half * TQ:(half + 1) * TQ, ps * TQ:(ps + 1) * TQ]
                        + bias_scr[kinds[ps], 2 * p + half] for ps in range(NBLK_A)]
            probs.append(_exp_scores(s_blocks)[0])
        o = _pv_stacked(probs, va_scr[p])
        oa.append(jnp.where(low, o[0], o[1]))
    oa = jnp.concatenate(oa, axis=-1)

    kb_sw, vb_sw = pltpu.roll(kb, HEAD_DIM, axis=1), pltpu.roll(vb, HEAD_DIM, axis=1)
    ones = jnp.ones((TQ, PAIR), BF16)
    slot_b = lax.rem(t, 2)
    maskb = maskb_scr[jnp.where(t >= 1, 0, 1)]
    ob_half = [[None, None] for _ in range(N_PAIRS_B)]
    for c, (k_c, v_c) in enumerate(((kb, vb), (kb_sw, vb_sw))):
        k_c, v_c = k_c.astype(BF16), v_c.astype(BF16)
        kbp_scr[c, slot_b] = k_c[TQ - WINDOW_B:, :]
        vbp_scr[c, slot_b, :, :PAIR] = v_c[TQ - WINDOW_B:, :]
        k_win = jnp.concatenate([kbp_scr[c, 1 - slot_b], k_c], axis=0)
        v_win = jnp.concatenate([vbp_scr[c, 1 - slot_b], jnp.concatenate([v_c, ones], axis=1)], axis=0)
        heads = [(p, half) for p in range(N_PAIRS_B) for half in range(2)
                 if (half == p // PAIRS_PER_KV) == (c == 0)]
        s = _dot_nt(jnp.concatenate([qb[p][half] for p, half in heads], axis=0), k_win)
        probs, extras = [], []
        for i, (p, half) in enumerate(heads):
            pr, ex = _exp_scores([s[i * TQ:(i + 1) * TQ, :] + maskb], sinks_ref[2 * p + half] * LOG2E)
            probs.append(pr)
            extras.append(ex)
        for (p, half), o in zip(heads, _pv_stacked(probs, v_win, extras)):
            ob_half[p][half] = o
    ob = jnp.concatenate([jnp.where(low, o[0], o[1]) for o in ob_half], axis=-1)

    o_ref[0, :, :WIDTH_A] = _rms(oa, goa_ref[...]).astype(BF16)
    o_ref[0, :, WIDTH_A:] = _rms(ob, gob_ref[...]).astype(BF16)


def _mix_prompt(sinks, x, g, w_bf16, rope_rows, ct, st, rbv, goa, gob, *, layer):
    b, s, _ = x.shape
    nt = s // TQ
    cur = lambda bi, t: (bi, t, 0)
    keep_a = lambda bi, t: (bi, jnp.maximum(t - (nt - A_ROWS // TQ), 0), 0)
    seq = lambda bi, t: (bi, 0, 0)
    return pl.pallas_call(
        _mix_prompt_kernel,
        grid=(b, nt),
        in_specs=[
            pl.BlockSpec(memory_space=pltpu.SMEM),
            pl.BlockSpec((1, TQ, D_MODEL), cur),
            _resident((1, D_MODEL)),
            _resident((D_MODEL, IN_WIDTH), layer),
            _resident(rope_rows.shape),
            _resident(ct.shape),
            _resident(st.shape),
            _resident((N_HEADS_A, BIAS_PERIOD)),
            _resident((1, WIDTH_A)),
            _resident((1, WIDTH_B)),
        ],
        out_specs=[
            pl.BlockSpec((1, TQ, MIX_WIDTH), cur),
            pl.BlockSpec((1, TQ, WIDTH_A), keep_a),
            pl.BlockSpec((1, TQ, WIDTH_A), keep_a),
            pl.BlockSpec((1, WINDOW_B, KV_WIDTH_B), seq),
            pl.BlockSpec((1, WINDOW_B, KV_WIDTH_B), seq),
        ],
        out_shape=[
            jax.ShapeDtypeStruct((b, s, MIX_WIDTH), BF16),
            jax.ShapeDtypeStruct((b, A_ROWS, WIDTH_A), F32),
            jax.ShapeDtypeStruct((b, A_ROWS, WIDTH_A), F32),
            jax.ShapeDtypeStruct((b, WINDOW_B, KV_WIDTH_B), F32),
            jax.ShapeDtypeStruct((b, WINDOW_B, KV_WIDTH_B), F32),
        ],
        scratch_shapes=[
            pltpu.VMEM((N_PAIRS_A, WIN_A, PAIR), BF16),
            pltpu.VMEM((N_PAIRS_A, WIN_A, 2 * PAIR), BF16),
            pltpu.VMEM((2, 2, WINDOW_B, PAIR), BF16),
            pltpu.VMEM((2, 2, WINDOW_B, 2 * PAIR), BF16),
            pltpu.VMEM((NBLK_A + 1, N_HEADS_A, TQ, TQ), F32),
            pltpu.VMEM((2, TQ, WIN_B), F32),
        ],
        compiler_params=pltpu.CompilerParams(
            dimension_semantics=("arbitrary", "arbitrary"),
            vmem_limit_bytes=V7X_VMEM_LIMIT_BYTES),
        name="mix_prompt",
    )(sinks, x, g, w_bf16, rope_rows, ct, st, rbv, goa, gob)


def _in_proj_kernel(x_ref, g_ref, w_ref, rope_ref, qa_ref, ka_ref, va_ref, qb_ref, kb_ref, vb_ref):
    n = _rms(x_ref[...], g_ref[...]).astype(BF16)
    z = _dot(n, w_ref[...])
    cos, sdn, sup = rope_ref[0], rope_ref[1], rope_ref[2]
    qa_ref[...] = (z[:, :O_KA] * QSCALE).astype(BF16)
    ka_ref[...] = z[:, O_KA:O_VA]
    va_ref[...] = z[:, O_VA:O_QB]
    for j in range(N_PAIRS_B):
        sl = slice(O_QB + j * PAIR, O_QB + (j + 1) * PAIR)
        qb_ref[:, j * PAIR:(j + 1) * PAIR] = (_rope(z[:, sl], cos, sdn, sup) * QSCALE).astype(BF16)
    kb_ref[...] = _rope(z[:, O_KB:O_VB], cos, sdn, sup)
    vb_ref[...] = z[:, O_VB:]


def _in_proj(x, g, w_bf16, rope_tabs, *, tm, layer):
    n_tok = x.shape[0]
    row = lambda i: (i, 0)
    widths = (WIDTH_A, WIDTH_A, WIDTH_A, WIDTH_B, KV_WIDTH_B, KV_WIDTH_B)
    dtypes = (BF16, F32, F32, BF16, F32, F32)
    return pl.pallas_call(
        _in_proj_kernel,
        grid=(n_tok // tm,),
        in_specs=[
            pl.BlockSpec((tm, D_MODEL), row),
            _resident((1, D_MODEL)),
            _resident((D_MODEL, IN_WIDTH), layer),
            _resident(rope_tabs.shape),
        ],
        out_specs=[pl.BlockSpec((tm, w), row) for w in widths],
        out_shape=[jax.ShapeDtypeStruct((n_tok, w), d) for w, d in zip(widths, dtypes)],
        compiler_params=pltpu.CompilerParams(
            dimension_semantics=("arbitrary",), vmem_limit_bytes=V7X_VMEM_LIMIT_BYTES),
        name="in_proj",
    )(x, g, w_bf16, rope_tabs)


N_SAMPLE_IN = 14
N_ROLL_IN = 8


def _attn_sample_kernel(*refs, n_prev, emit):
    (sinks_ref, qa_ref, kac_ref, kan_ref, vac_ref, van_ref, qb_ref, kbc_ref, kbn_ref, vbc_ref,
     vbn_ref, rbv_ref, goa_ref, gob_ref) = refs[:N_SAMPLE_IN]
    n_in = N_SAMPLE_IN + N_ROLL_IN * n_prev
    o_ref, bias_scr = refs[n_in], refs[-1]
    t_s = qa_ref.shape[0]
    la = kac_ref.shape[1]

    @pl.when(pl.program_id(0) == 0)
    def _init():
        for h in range(N_HEADS_A):
            bias_scr[h] = _rolled_bias_rows(rbv_ref, h, t_s)

    if emit:
        own = (kac_ref, kan_ref, vac_ref, van_ref, kbc_ref, kbn_ref, vbc_ref, vbn_ref)
        layers = [refs[N_SAMPLE_IN + N_ROLL_IN * l:N_SAMPLE_IN + N_ROLL_IN * (l + 1)]
                  for l in range(n_prev)] + [own]
        for l, lr in enumerate(layers):
            for j, new_ref in enumerate(refs[n_in + 1:n_in + 5]):
                cache_ref, fresh_ref = lr[2 * j], lr[2 * j + 1]
                kept = new_ref.shape[2] - t_s
                new_ref[l, 0, :kept, :] = cache_ref[0, cache_ref.shape[1] - kept:, :]
                new_ref[l, 0, kept:, :] = fresh_ref[...]

    low = _low_lanes((t_s, PAIR))
    ones_a = jnp.ones((la + t_s, PAIR), BF16)
    oa = []
    for p in range(N_PAIRS_A):
        sl = slice(p * PAIR, (p + 1) * PAIR)
        qp = qa_ref[:, sl]
        k = jnp.concatenate([kac_ref[0, :, sl], kan_ref[:, sl]], axis=0).astype(BF16)
        v = jnp.concatenate([vac_ref[0, :, sl], van_ref[:, sl]], axis=0).astype(BF16)
        s = _dot_nt(jnp.concatenate([_keep_half(qp, half, 0.0) for half in range(2)], axis=0), k)
        probs = [_exp_scores([s[half * t_s:(half + 1) * t_s, :] + bias_scr[2 * p + half, :, :la + t_s]])[0]
                 for half in range(2)]
        o = _pv_stacked(probs, jnp.concatenate([v, ones_a], axis=1))
        oa.append(jnp.where(low, o[0], o[1]))
    oa = jnp.concatenate(oa, axis=-1)

    kb = jnp.concatenate([kbc_ref[0], kbn_ref[...]], axis=0)
    vb = jnp.concatenate([vbc_ref[0], vbn_ref[...]], axis=0)
    ones_b = jnp.ones((kb.shape[0], PAIR), BF16)
    ob_half = [[None, None] for _ in range(N_PAIRS_B)]
    for c in range(2):
        k_c = (kb if c == 0 else pltpu.roll(kb, HEAD_DIM, axis=1)).astype(BF16)
        v_c = (vb if c == 0 else pltpu.roll(vb, HEAD_DIM, axis=1)).astype(BF16)
        heads = [(p, half) for p in range(N_PAIRS_B) for half in range(2)
                 if (half == p // PAIRS_PER_KV) == (c == 0)]
        s = _dot_nt(jnp.concatenate([_keep_half(qb_ref[:, p * PAIR:(p + 1) * PAIR], half, 0.0)
                                     for p, half in heads], axis=0), k_c)
        probs, extras = [], []
        for i, (p, half) in enumerate(heads):
            pr, ex = _exp_scores([s[i * t_s:(i + 1) * t_s, :]], sinks_ref[2 * p + half] * LOG2E)
            probs.append(pr)
            extras.append(ex)
        for (p, half), o in zip(heads, _pv_stacked(probs, jnp.concatenate([v_c, ones_b], axis=1), extras)):
            ob_half[p][half] = o
    ob = jnp.concatenate([jnp.where(low, o[0], o[1]) for o in ob_half], axis=-1)
    o_ref[:, :WIDTH_A] = _rms(oa, goa_ref[...]).astype(BF16)
    o_ref[:, WIDTH_A:] = _rms(ob, gob_ref[...]).astype(BF16)


def _attn_sample(sinks, qa, ka_cache, ka, va_cache, va, qb, kb_cache, kb, vb_cache, vb,
                 rbv, goa, gob, *, t_s, keep_a, keep_b, layer, prev_new, emit):
    n_tok = qa.shape[0]
    n_seq = n_tok // t_s
    la, lb = ka_cache.shape[2], kb_cache.shape[2]
    assert t_s <= keep_a <= la + t_s and t_s <= keep_b <= lb + t_s
    row = lambda i: (i, 0)
    cache = lambda l, rows, width: pl.BlockSpec((None, 1, rows, width), lambda i: (l, i, 0, 0))
    new = lambda width: pl.BlockSpec((t_s, width), row)

    def kv_specs(l):
        return [cache(l, la, WIDTH_A), new(WIDTH_A), cache(l, la, WIDTH_A), new(WIDTH_A),
                cache(l, lb, KV_WIDTH_B), new(KV_WIDTH_B), cache(l, lb, KV_WIDTH_B), new(KV_WIDTH_B)]

    own = kv_specs(layer)
    in_specs = ([pl.BlockSpec(memory_space=pltpu.SMEM), new(WIDTH_A)] + own[:4] + [new(WIDTH_B)] + own[4:]
                + [_resident((N_HEADS_A, BIAS_PERIOD)), _resident((1, WIDTH_A)), _resident((1, WIDTH_B))])
    operands = [sinks, qa, ka_cache, ka, va_cache, va, qb, kb_cache, kb, vb_cache, vb, rbv, goa, gob]
    out_specs = [pl.BlockSpec((t_s, MIX_WIDTH), row)]
    out_shape = [jax.ShapeDtypeStruct((n_tok, MIX_WIDTH), BF16)]
    n_prev = 0
    if emit:
        n_prev = len(prev_new)
        for l, (ka_l, va_l, kb_l, vb_l) in enumerate(prev_new):
            in_specs += kv_specs(l)
            operands += [ka_cache, ka_l, va_cache, va_l, kb_cache, kb_l, vb_cache, vb_l]
        n_layers = n_prev + 1
        for keep, width in ((keep_a, WIDTH_A), (keep_a, WIDTH_A), (keep_b, KV_WIDTH_B), (keep_b, KV_WIDTH_B)):
            out_specs.append(pl.BlockSpec((n_layers, 1, keep, width), lambda i: (0, i, 0, 0)))
            out_shape.append(jax.ShapeDtypeStruct((n_layers, n_seq, keep, width), F32))
    return pl.pallas_call(
        functools.partial(_attn_sample_kernel, n_prev=n_prev, emit=emit),
        grid=(n_seq,),
        in_specs=in_specs,
        out_specs=out_specs,
        out_shape=out_shape,
        scratch_shapes=[pltpu.VMEM((N_HEADS_A, t_s, BIAS_PERIOD), F32)],
        compiler_params=pltpu.CompilerParams(
            dimension_semantics=("arbitrary",), vmem_limit_bytes=V7X_VMEM_LIMIT_BYTES),
        name="attn_sample",
    )(*operands)


TAIL_SUB_ROWS = 256


def _tail_kernel(h_ref, o_ref, p_ref, wout_ref, gffn_ref, wgu_ref, wdown_ref, wgate_ref,
                 wproj_ref, gfin_ref, out_ref, *, final, n_sub):
    rows = h_ref.shape[0] // n_sub
    for first in range(0, n_sub, 2):
        subs = [pl.ds(i * rows, rows) for i in range(first, first + 2)]
        h = [h_ref[sl, :] + _dot(o_ref[sl, :], wout_ref[...]) for sl in subs]
        pp = [_dot(p_ref[sl, :].astype(BF16), wproj_ref[...]) for sl in subs]
        gu = [_dot(_rms(x, gffn_ref[...]).astype(BF16), wgu_ref[...]) for x in h]
        act = [(jax.nn.silu(x[:, :D_FF]) * x[:, D_FF:]).astype(BF16) for x in gu]
        h = [x + _dot(a, wdown_ref[...]) for x, a in zip(h, act)]
        gate = [jax.nn.sigmoid(_dot(x.astype(BF16), wgate_ref[...])) for x in h]
        for sl, x, g, e in zip(subs, h, gate, pp):
            x = x + g * e
            if final:
                x = _rms(x, gfin_ref[...])
            out_ref[sl, :] = x


def _tail(h, o, p, wout, gffn, wgu, wdown, wgate, wproj, gfin, *, tm, final, layer):
    n_tok = h.shape[0]
    row = lambda i: (i, 0)
    return pl.pallas_call(
        functools.partial(_tail_kernel, final=final, n_sub=tm // TAIL_SUB_ROWS),
        grid=(n_tok // tm,),
        in_specs=[
            pl.BlockSpec((tm, D_MODEL), row),
            pl.BlockSpec((tm, MIX_WIDTH), row),
            pl.BlockSpec((None, tm, D_PLE), lambda i: (layer, i, 0)),
            _resident((MIX_WIDTH, D_MODEL), layer),
            _resident((1, D_MODEL)),
            _resident((D_MODEL, 2 * D_FF), layer),
            _resident((D_FF, D_MODEL), layer),
            _resident((D_MODEL, D_MODEL), layer),
            _resident((D_PLE, D_MODEL), layer),
            _resident((1, D_MODEL)),
        ],
        out_specs=pl.BlockSpec((tm, D_MODEL), row),
        out_shape=jax.ShapeDtypeStruct((n_tok, D_MODEL), F32),
        compiler_params=pltpu.CompilerParams(
            dimension_semantics=("arbitrary",), vmem_limit_bytes=V7X_VMEM_LIMIT_BYTES),
        name="layer_tail",
    )(h, o, p, wout, gffn, wgu, wdown, wgate, wproj, gfin)


CAST_ROWS = 256


def _cast_kernel(w_ref, o_ref):
    o_ref[...] = w_ref[...].astype(o_ref.dtype)


def _to_bf16(w):
    depth, rows, cols = w.shape
    blk = pl.BlockSpec((1, CAST_ROWS, cols), lambda d, r: (d, r, 0))
    return pl.pallas_call(
        _cast_kernel,
        grid=(depth, rows // CAST_ROWS),
        in_specs=[blk],
        out_specs=blk,
        out_shape=jax.ShapeDtypeStruct(w.shape, BF16),
        compiler_params=pltpu.CompilerParams(dimension_semantics=("arbitrary", "arbitrary")),
        name="cast_bf16",
    )(w)


def _rope_lane_tables(pos):
    half = ROT_DIM // 2
    d = np.arange(PAIR) % HEAD_DIM
    inv = ROPE_THETA ** (-(2.0 * (d % half)) / ROT_DIM)
    ang = np.where(d < ROT_DIM, np.asarray(pos, np.float64)[:, None] * inv[None, :], 0.0)
    lower = (d < half).astype(np.float64)[None, :]
    upper = ((d >= half) & (d < ROT_DIM)).astype(np.float64)[None, :]
    return np.cos(ang), np.sin(ang), lower, upper


def _rope_tables_direct(pos):
    cos, sin, lower, upper = _rope_lane_tables(pos)
    return jnp.asarray(np.stack([cos, -sin * lower, sin * upper]), F32)


def _rope_tables_split(n_tiles, tile):
    cr, sr, lower, upper = _rope_lane_tables(np.arange(tile))
    ct, st, _, _ = _rope_lane_tables(np.arange(n_tiles) * tile)
    rows = np.stack([cr, sr, -cr * lower, -sr * lower, cr * upper, sr * upper])
    return jnp.asarray(rows, F32), jnp.asarray(ct, F32), jnp.asarray(st, F32)


def _rel_bias_row(rel_bias):
    u = np.arange(BIAS_PERIOD)
    diff = np.where(u < BIAS_PERIOD // 2 + A_ROWS // 2, u, u - BIAS_PERIOD)
    idx = np.clip(A_ROWS - diff, -REL_CLIP, REL_CLIP) + REL_CLIP
    return rel_bias.astype(F32)[:, idx]


def kernel(x_prompt, x_sample, p_prompt, p_sample, cache_a_k, cache_a_v, cache_b_k, cache_b_v,
           g_mix_norm, w_in, rel_bias_a, sinks_b, g_out_a, g_out_b, w_out, g_ffn_norm,
           w_gate_up, w_down, w_ple_proj, w_ple_gate, g_final):
    b_p, s_p, _ = x_prompt.shape
    b_s, t_s, _ = x_sample.shape
    depth = w_in.shape[0]
    la_c, lb_c = cache_a_k.shape[2], cache_b_k.shape[2]
    keep_a_s = min(A_ROWS, la_c + t_s)
    keep_b_s = min(WINDOW_B, lb_c + t_s)
    tm_s = 512
    tm_tail = 1024
    assert s_p % TQ == 0 and s_p >= A_ROWS and (b_s * t_s) % tm_s == 0 and tm_s % t_s == 0
    assert la_c == A_ROWS and lb_c == WINDOW_B and t_s == CHUNK

    rope_rows, rope_ct, rope_st = _rope_tables_split(s_p // TQ, TQ)
    rope_s = _rope_tables_direct(PAST_LEN + np.arange(tm_s) % t_s)
    row2 = lambda a: a.reshape(1, -1).astype(F32)
    g_fin = row2(g_final)

    hp = x_prompt
    hs = x_sample.reshape(b_s * t_s, D_MODEL)
    outs = [[] for _ in range(4)]
    new_rows = []
    w_in_b, w_out_b, w_gu_b, w_down_b, w_gate_b, w_proj_b = (
        _to_bf16(w) for w in (w_in, w_out, w_gate_up, w_down, w_ple_gate, w_ple_proj))
    p_prompt_f = p_prompt.reshape(depth, b_p * s_p, D_PLE)
    p_sample_f = p_sample.reshape(depth, b_s * t_s, D_PLE)
    cak_s = cache_a_k.reshape(depth, b_s, la_c, WIDTH_A)
    cav_s = cache_a_v.reshape(depth, b_s, la_c, WIDTH_A)
    cbk_s = cache_b_k.reshape(depth, b_s, lb_c, KV_WIDTH_B)
    cbv_s = cache_b_v.reshape(depth, b_s, lb_c, KV_WIDTH_B)
    for i in range(depth):
        tail_w = (w_out_b, row2(g_ffn_norm[i]), w_gu_b, w_down_b, w_gate_b, w_proj_b, g_fin)
        g_mix = row2(g_mix_norm[i])
        goa, gob = row2(g_out_a[i]), row2(g_out_b[i])
        sinks = sinks_b[i].astype(F32)
        rbv = _rel_bias_row(rel_bias_a[i])
        final = i == depth - 1

        o, cak, cav, cbk, cbv = _mix_prompt(sinks, hp, g_mix, w_in_b, rope_rows, rope_ct, rope_st,
                                            rbv, goa, gob, layer=i)
        hp = _tail(hp.reshape(b_p * s_p, D_MODEL), o.reshape(b_p * s_p, MIX_WIDTH),
                   p_prompt_f, *tail_w, tm=tm_tail, final=final, layer=i)
        hp = hp.reshape(b_p, s_p, D_MODEL)
        outs[0].append(cak.reshape(b_p, A_ROWS, N_HEADS_A, HEAD_DIM))
        outs[1].append(cav.reshape(b_p, A_ROWS, N_HEADS_A, HEAD_DIM))
        outs[2].append(cbk.reshape(b_p, WINDOW_B, N_KV_B, HEAD_DIM))
        outs[3].append(cbv.reshape(b_p, WINDOW_B, N_KV_B, HEAD_DIM))

        qa, ka, va, qb, kb, vb = _in_proj(hs, g_mix, w_in_b, rope_s, tm=tm_s, layer=i)
        o, *rolled = _attn_sample(sinks, qa, cak_s, ka, cav_s, va, qb, cbk_s, kb, cbv_s, vb,
                                  rbv, goa, gob, t_s=t_s, keep_a=keep_a_s, keep_b=keep_b_s,
                                  layer=i, prev_new=new_rows, emit=final)
        new_rows.append((ka, va, kb, vb))
        hs = _tail(hs, o, p_sample_f, *tail_w, tm=tm_tail, final=final, layer=i)

    nak, nav, nbk, nbv = rolled
    y_sample = hs.reshape(b_s, t_s, D_MODEL)
    return ((hp, y_sample) + tuple(jnp.stack(o) for o in outs)
            + (nak.reshape(depth, b_s, keep_a_s, N_HEADS_A, HEAD_DIM),
               nav.reshape(depth, b_s, keep_a_s, N_HEADS_A, HEAD_DIM),
               nbk.reshape(depth, b_s, keep_b_s, N_KV_B, HEAD_DIM),
               nbv.reshape(depth, b_s, keep_b_s, N_KV_B, HEAD_DIM)))
```

```python
import functools
import math

import numpy as np
import jax
import jax.numpy as jnp
from jax import lax
from jax.experimental import pallas as pl
from jax.experimental.pallas import tpu as pltpu

D_MODEL = 1024
CHUNK = 64
HEAD_DIM = 64
N_HEADS_A = 8
N_HEADS_B = 8
N_KV_B = 2
GROUP_B = N_HEADS_B // N_KV_B
WIDTH_A = N_HEADS_A * HEAD_DIM
WIDTH_B = N_HEADS_B * HEAD_DIM
KV_WIDTH_B = N_KV_B * HEAD_DIM
MIX_WIDTH = WIDTH_A + WIDTH_B
IN_WIDTH = 3 * WIDTH_A + WIDTH_B + 2 * KV_WIDTH_B
PREV_CHUNKS_A = 8
A_ROWS = PREV_CHUNKS_A * CHUNK
REL_CLIP = 256
WINDOW_B = 128
PREV_CHUNKS_B = WINDOW_B // CHUNK
ROT_DIM = HEAD_DIM // 4
ROPE_THETA = 500000.0
D_FF = 2816
D_PLE = 256
PAST_LEN = 2048
RMS_EPS = 1e-6
LOG2E = math.log2(math.e)
QSCALE = HEAD_DIM ** -0.5 * LOG2E
PAIR = 2 * HEAD_DIM
N_PAIRS_A = WIDTH_A // PAIR
N_PAIRS_B = WIDTH_B // PAIR
PAIRS_PER_KV = GROUP_B // 2
NEG_INF = float("-inf")
BIAS_PERIOD = 1024

V7X_VMEM_LIMIT_BYTES = 56 * 1024 * 1024

BF16 = jnp.bfloat16
F32 = jnp.float32

O_KA, O_VA, O_QB = WIDTH_A, 2 * WIDTH_A, 3 * WIDTH_A
O_KB = O_QB + WIDTH_B
O_VB = O_KB + KV_WIDTH_B


def _rms(x, g):
    return x * lax.rsqrt(jnp.mean(x * x, axis=-1, keepdims=True) + RMS_EPS) * g


def _dot(a, b):
    return jnp.dot(a, b, preferred_element_type=F32)


def _dot_nt(a, b):
    return lax.dot_general(a, b, (((1,), (1,)), ((), ())), preferred_element_type=F32)


def _resident(shape, layer=None):
    nd = len(shape)
    if layer is None:
        return pl.BlockSpec(shape, lambda *_: (0,) * nd, pipeline_mode=pl.Buffered(1))
    return pl.BlockSpec((None,) + tuple(shape), lambda *_: (layer,) + (0,) * nd,
                        pipeline_mode=pl.Buffered(1))


def _rope(x, cos, sdn, sup):
    half = ROT_DIM // 2
    return (x * cos + pltpu.roll(x, PAIR - half, axis=1) * sdn
            + pltpu.roll(x, half, axis=1) * sup)


def _low_lanes(shape):
    return lax.broadcasted_iota(jnp.int32, shape, 1) < HEAD_DIM


def _keep_half(x, half, fill):
    keep = _low_lanes(x.shape) if half == 0 else ~_low_lanes(x.shape)
    return jnp.where(keep, x, jnp.full_like(x, fill))


def _exp_scores(s_blocks, sink=None):
    if len({s.shape for s in s_blocks}) == 1:
        m = jnp.max(functools.reduce(jnp.maximum, s_blocks), axis=-1, keepdims=True)
    else:
        m = functools.reduce(jnp.maximum, [jnp.max(s, axis=-1, keepdims=True) for s in s_blocks])
    if sink is not None:
        m = jnp.maximum(m, sink)
    p = jnp.concatenate([jnp.exp2(s - m).astype(BF16) for s in s_blocks], axis=-1)
    return p, (None if sink is None else jnp.exp2(sink - m))


def _pv_stacked(probs, v_ext, extras=None):
    m = probs[0].shape[0]
    x = _dot(jnp.concatenate(probs, axis=0), v_ext)
    outs = []
    for i in range(len(probs)):
        den = x[i * m:(i + 1) * m, PAIR:]
        if extras is not None:
            den = den + extras[i]
        outs.append(x[i * m:(i + 1) * m, :PAIR] / den)
    return outs


def _rolled_bias_rows(rbv_ref, h, rows):
    x = jnp.broadcast_to(rbv_ref[h:h + 1, :] * LOG2E, (rows, BIAS_PERIOD))
    return pltpu.roll(x, 0, axis=1, stride=1, stride_axis=0)


def _band_valid(rows, col0, cols, prev_chunks):
    r = lax.broadcasted_iota(jnp.int32, (rows, cols), 0) // CHUNK
    c = (lax.broadcasted_iota(jnp.int32, (rows, cols), 1) + col0) // CHUNK
    return (c >= r) & (c <= r + prev_chunks)


TQ = 256
NBLK_A = A_ROWS // TQ + 1
WIN_A = NBLK_A * TQ
WIN_B = WINDOW_B + TQ
NEG_KIND = NBLK_A
HALF_B = WINDOW_B
KEYS_B = HALF_B + WINDOW_B


def _mix_prompt_kernel(sinks_ref, x_ref, g_ref, w_ref, rope_ref, ct_ref, st_ref, rbv_ref,
                       goa_ref, gob_ref,
                       o_ref, cak_ref, cav_ref, cbk_ref, cbv_ref,
                       ka_scr, va_scr, kbp_scr, vbp_scr, bias_scr, maskb_scr):
    b, t = pl.program_id(0), pl.program_id(1)

    @pl.when((b == 0) & (t == 0))
    def _init():
        ka_scr[...] = jnp.zeros_like(ka_scr)
        kbp_scr[...] = jnp.zeros_like(kbp_scr)
        for v_scr in (va_scr, vbp_scr):
            v_scr[..., :PAIR] = jnp.zeros(v_scr.shape[:-1] + (PAIR,), BF16)
            v_scr[..., PAIR:] = jnp.ones(v_scr.shape[:-1] + (PAIR,), BF16)
        for h in range(N_HEADS_A):
            rows = _rolled_bias_rows(rbv_ref, h, TQ)
            for j in range(NBLK_A):
                valid = _band_valid(TQ, j * TQ, TQ, PREV_CHUNKS_A)
                bias_scr[j, h] = jnp.where(valid, rows[:, j * TQ:(j + 1) * TQ], NEG_INF)
            bias_scr[NEG_KIND, h] = jnp.full((TQ, TQ), NEG_INF, F32)
        band = jnp.where(_band_valid(HALF_B, 0, KEYS_B, PREV_CHUNKS_B),
                         jnp.zeros((HALF_B, KEYS_B), F32), NEG_INF)
        col = lax.broadcasted_iota(jnp.int32, (HALF_B, KEYS_B), 1)
        maskb_scr[0] = band
        maskb_scr[1] = jnp.where(col >= WINDOW_B, band, NEG_INF)

    n = _rms(x_ref[0], g_ref[...]).astype(BF16)
    slot = lax.rem(t, NBLK_A)
    row0 = pl.multiple_of(slot * TQ, TQ)
    ka = _dot(n, w_ref[:, O_KA:O_VA])
    for p in range(N_PAIRS_A):
        ka_scr[p, pl.ds(row0, TQ), :] = ka[:, p * PAIR:(p + 1) * PAIR].astype(BF16)
    za = _dot(n, w_ref[:, :O_KA])
    qa = [[_keep_half(za[:, j * PAIR:(j + 1) * PAIR] * QSCALE, half, 0.0).astype(BF16)
           for half in range(2)] for j in range(N_PAIRS_A)]
    va = _dot(n, w_ref[:, O_VA:O_QB])
    for p in range(N_PAIRS_A):
        va_scr[p, pl.ds(row0, TQ), :PAIR] = va[:, p * PAIR:(p + 1) * PAIR].astype(BF16)
    zb = _dot(n, w_ref[:, O_QB:])
    ct, st = ct_ref[pl.ds(t, 1), :], st_ref[pl.ds(t, 1), :]
    cos = ct * rope_ref[0] - st * rope_ref[1]
    sdn = st * rope_ref[2] + ct * rope_ref[3]
    sup = st * rope_ref[4] + ct * rope_ref[5]
    kb = _rope(zb[:, WIDTH_B:WIDTH_B + KV_WIDTH_B], cos, sdn, sup)
    vb = zb[:, WIDTH_B + KV_WIDTH_B:]
    qb = []
    for j in range(N_PAIRS_B):
        q = _rope(zb[:, j * PAIR:(j + 1) * PAIR], cos, sdn, sup) * QSCALE
        qb.append([_keep_half(q, half, 0.0).astype(BF16) for half in range(2)])

    cak_ref[0] = ka
    cav_ref[0] = va
    cbk_ref[0] = kb[TQ - WINDOW_B:, :]
    cbv_ref[0] = vb[TQ - WINDOW_B:, :]

    kinds = []
    for ps in range(NBLK_A):
        d = lax.rem(t - ps + NBLK_A, NBLK_A)
        kinds.append(jnp.where(t >= d, NBLK_A - 1 - d, NEG_KIND))
    low = _low_lanes((TQ, PAIR))
    oa = []
    for p in range(N_PAIRS_A):
        s = _dot_nt(jnp.concatenate(qa[p], axis=0), ka_scr[p])
        probs = []
        for half in range(2):
            s_blocks = [s[half * TQ:(half + 1) * TQ, ps * TQ:(ps + 1) * TQ]
                        + bias_scr[kinds[ps], 2 * p + half] for ps in range(NBLK_A)]
            probs.append(_exp_scores(s_blocks)[0])
        o = _pv_stacked(probs, va_scr[p])
        oa.append(jnp.where(low, o[0], o[1]))
    oa = jnp.concatenate(oa, axis=-1)

    kb_sw, vb_sw = pltpu.roll(kb, HEAD_DIM, axis=1), pltpu.roll(vb, HEAD_DIM, axis=1)
    ones = jnp.ones((TQ, PAIR), BF16)
    slot_b = lax.rem(t, 2)
    mask_first = maskb_scr[jnp.where(t >= 1, 0, 1)]
    ob_rows = [[[None] * (TQ // HALF_B) for _ in range(2)] for _ in range(N_PAIRS_B)]
    for c, (k_c, v_c) in enumerate(((kb, vb), (kb_sw, vb_sw))):
        k_c, v_c = k_c.astype(BF16), v_c.astype(BF16)
        kbp_scr[c, slot_b] = k_c[TQ - WINDOW_B:, :]
        vbp_scr[c, slot_b, :, :PAIR] = v_c[TQ - WINDOW_B:, :]
        k_win = jnp.concatenate([kbp_scr[c, 1 - slot_b], k_c], axis=0)
        v_win = jnp.concatenate([vbp_scr[c, 1 - slot_b], jnp.concatenate([v_c, ones], axis=1)], axis=0)
        heads = [(p, half) for p in range(N_PAIRS_B) for half in range(2)
                 if (half == p // PAIRS_PER_KV) == (c == 0)]
        for r in range(TQ // HALF_B):
            rows, keys = slice(r * HALF_B, (r + 1) * HALF_B), slice(r * HALF_B, r * HALF_B + KEYS_B)
            mask = mask_first if r == 0 else maskb_scr[0]
            s = _dot_nt(jnp.concatenate([qb[p][half][rows] for p, half in heads], axis=0), k_win[keys])
            probs, extras = [], []
            for i, (p, half) in enumerate(heads):
                pr, ex = _exp_scores([s[i * HALF_B:(i + 1) * HALF_B, :] + mask],
                                     sinks_ref[2 * p + half] * LOG2E)
                probs.append(pr)
                extras.append(ex)
            for (p, half), o in zip(heads, _pv_stacked(probs, v_win[keys], extras)):
                ob_rows[p][half][r] = o
    ob = jnp.concatenate([jnp.where(low, jnp.concatenate(o[0], axis=0), jnp.concatenate(o[1], axis=0))
                          for o in ob_rows], axis=-1)

    o_ref[0, :, :WIDTH_A] = _rms(oa, goa_ref[...]).astype(BF16)
    o_ref[0, :, WIDTH_A:] = _rms(ob, gob_ref[...]).astype(BF16)


def _mix_prompt(sinks, x, g, w_bf16, rope_rows, ct, st, rbv, goa, gob, *, layer):
    b, s, _ = x.shape
    nt = s // TQ
    cur = lambda bi, t: (bi, t, 0)
    keep_a = lambda bi, t: (bi, jnp.maximum(t - (nt - A_ROWS // TQ), 0), 0)
    seq = lambda bi, t: (bi, 0, 0)
    return pl.pallas_call(
        _mix_prompt_kernel,
        grid=(b, nt),
        in_specs=[
            pl.BlockSpec(memory_space=pltpu.SMEM),
            pl.BlockSpec((1, TQ, D_MODEL), cur),
            _resident((1, D_MODEL)),
            _resident((D_MODEL, IN_WIDTH), layer),
            _resident(rope_rows.shape),
            _resident(ct.shape),
            _resident(st.shape),
            _resident((N_HEADS_A, BIAS_PERIOD)),
            _resident((1, WIDTH_A)),
            _resident((1, WIDTH_B)),
        ],
        out_specs=[
            pl.BlockSpec((1, TQ, MIX_WIDTH), cur),
            pl.BlockSpec((1, TQ, WIDTH_A), keep_a),
            pl.BlockSpec((1, TQ, WIDTH_A), keep_a),
            pl.BlockSpec((1, WINDOW_B, KV_WIDTH_B), seq),
            pl.BlockSpec((1, WINDOW_B, KV_WIDTH_B), seq),
        ],
        out_shape=[
            jax.ShapeDtypeStruct((b, s, MIX_WIDTH), BF16),
            jax.ShapeDtypeStruct((b, A_ROWS, WIDTH_A), F32),
            jax.ShapeDtypeStruct((b, A_ROWS, WIDTH_A), F32),
            jax.ShapeDtypeStruct((b, WINDOW_B, KV_WIDTH_B), F32),
            jax.ShapeDtypeStruct((b, WINDOW_B, KV_WIDTH_B), F32),
        ],
        scratch_shapes=[
            pltpu.VMEM((N_PAIRS_A, WIN_A, PAIR), BF16),
            pltpu.VMEM((N_PAIRS_A, WIN_A, 2 * PAIR), BF16),
            pltpu.VMEM((2, 2, WINDOW_B, PAIR), BF16),
            pltpu.VMEM((2, 2, WINDOW_B, 2 * PAIR), BF16),
            pltpu.VMEM((NBLK_A + 1, N_HEADS_A, TQ, TQ), F32),
            pltpu.VMEM((2, HALF_B, KEYS_B), F32),
        ],
        compiler_params=pltpu.CompilerParams(
            dimension_semantics=("arbitrary", "arbitrary"),
            vmem_limit_bytes=V7X_VMEM_LIMIT_BYTES),
        name="mix_prompt",
    )(sinks, x, g, w_bf16, rope_rows, ct, st, rbv, goa, gob)


def _in_proj_kernel(x_ref, g_ref, w_ref, rope_ref, qa_ref, ka_ref, va_ref, qb_ref, kb_ref, vb_ref):
    n = _rms(x_ref[...], g_ref[...]).astype(BF16)
    z = _dot(n, w_ref[...])
    cos, sdn, sup = rope_ref[0], rope_ref[1], rope_ref[2]
    qa_ref[...] = (z[:, :O_KA] * QSCALE).astype(BF16)
    ka_ref[...] = z[:, O_KA:O_VA]
    va_ref[...] = z[:, O_VA:O_QB]
    for j in range(N_PAIRS_B):
        sl = slice(O_QB + j * PAIR, O_QB + (j + 1) * PAIR)
        qb_ref[:, j * PAIR:(j + 1) * PAIR] = (_rope(z[:, sl], cos, sdn, sup) * QSCALE).astype(BF16)
    kb_ref[...] = _rope(z[:, O_KB:O_VB], cos, sdn, sup)
    vb_ref[...] = z[:, O_VB:]


def _in_proj(x, g, w_bf16, rope_tabs, *, tm, layer):
    n_tok = x.shape[0]
    row = lambda i: (i, 0)
    widths = (WIDTH_A, WIDTH_A, WIDTH_A, WIDTH_B, KV_WIDTH_B, KV_WIDTH_B)
    dtypes = (BF16, F32, F32, BF16, F32, F32)
    return pl.pallas_call(
        _in_proj_kernel,
        grid=(n_tok // tm,),
        in_specs=[
            pl.BlockSpec((tm, D_MODEL), row),
            _resident((1, D_MODEL)),
            _resident((D_MODEL, IN_WIDTH), layer),
            _resident(rope_tabs.shape),
        ],
        out_specs=[pl.BlockSpec((tm, w), row) for w in widths],
        out_shape=[jax.ShapeDtypeStruct((n_tok, w), d) for w, d in zip(widths, dtypes)],
        compiler_params=pltpu.CompilerParams(
            dimension_semantics=("arbitrary",), vmem_limit_bytes=V7X_VMEM_LIMIT_BYTES),
        name="in_proj",
    )(x, g, w_bf16, rope_tabs)


N_SAMPLE_IN = 14
N_ROLL_IN = 8


def _attn_sample_kernel(*refs, n_prev, emit):
    (sinks_ref, qa_ref, kac_ref, kan_ref, vac_ref, van_ref, qb_ref, kbc_ref, kbn_ref, vbc_ref,
     vbn_ref, rbv_ref, goa_ref, gob_ref) = refs[:N_SAMPLE_IN]
    n_in = N_SAMPLE_IN + N_ROLL_IN * n_prev
    o_ref, bias_scr = refs[n_in], refs[-1]
    t_s = qa_ref.shape[0]
    la = kac_ref.shape[1]

    @pl.when(pl.program_id(0) == 0)
    def _init():
        for h in range(N_HEADS_A):
            bias_scr[h] = _rolled_bias_rows(rbv_ref, h, t_s)

    if emit:
        own = (kac_ref, kan_ref, vac_ref, van_ref, kbc_ref, kbn_ref, vbc_ref, vbn_ref)
        layers = [refs[N_SAMPLE_IN + N_ROLL_IN * l:N_SAMPLE_IN + N_ROLL_IN * (l + 1)]
                  for l in range(n_prev)] + [own]
        for l, lr in enumerate(layers):
            for j, new_ref in enumerate(refs[n_in + 1:n_in + 5]):
                cache_ref, fresh_ref = lr[2 * j], lr[2 * j + 1]
                kept = new_ref.shape[2] - t_s
                new_ref[l, 0, :kept, :] = cache_ref[0, cache_ref.shape[1] - kept:, :]
                new_ref[l, 0, kept:, :] = fresh_ref[...]

    low = _low_lanes((t_s, PAIR))
    ones_a = jnp.ones((la + t_s, PAIR), BF16)
    oa = []
    for p in range(N_PAIRS_A):
        sl = slice(p * PAIR, (p + 1) * PAIR)
        qp = qa_ref[:, sl]
        k = jnp.concatenate([kac_ref[0, :, sl], kan_ref[:, sl]], axis=0).astype(BF16)
        v = jnp.concatenate([vac_ref[0, :, sl], van_ref[:, sl]], axis=0).astype(BF16)
        s = _dot_nt(jnp.concatenate([_keep_half(qp, half, 0.0) for half in range(2)], axis=0), k)
        probs = [_exp_scores([s[half * t_s:(half + 1) * t_s, :] + bias_scr[2 * p + half, :, :la + t_s]])[0]
                 for half in range(2)]
        o = _pv_stacked(probs, jnp.concatenate([v, ones_a], axis=1))
        oa.append(jnp.where(low, o[0], o[1]))
    oa = jnp.concatenate(oa, axis=-1)

    kb = jnp.concatenate([kbc_ref[0], kbn_ref[...]], axis=0)
    vb = jnp.concatenate([vbc_ref[0], vbn_ref[...]], axis=0)
    ones_b = jnp.ones((kb.shape[0], PAIR), BF16)
    ob_half = [[None, None] for _ in range(N_PAIRS_B)]
    for c in range(2):
        k_c = (kb if c == 0 else pltpu.roll(kb, HEAD_DIM, axis=1)).astype(BF16)
        v_c = (vb if c == 0 else pltpu.roll(vb, HEAD_DIM, axis=1)).astype(BF16)
        heads = [(p, half) for p in range(N_PAIRS_B) for half in range(2)
                 if (half == p // PAIRS_PER_KV) == (c == 0)]
        s = _dot_nt(jnp.concatenate([_keep_half(qb_ref[:, p * PAIR:(p + 1) * PAIR], half, 0.0)
                                     for p, half in heads], axis=0), k_c)
        probs, extras = [], []
        for i, (p, half) in enumerate(heads):
            pr, ex = _exp_scores([s[i * t_s:(i + 1) * t_s, :]], sinks_ref[2 * p + half] * LOG2E)
            probs.append(pr)
            extras.append(ex)
        for (p, half), o in zip(heads, _pv_stacked(probs, jnp.concatenate([v_c, ones_b], axis=1), extras)):
            ob_half[p][half] = o
    ob = jnp.concatenate([jnp.where(low, o[0], o[1]) for o in ob_half], axis=-1)
    o_ref[:, :WIDTH_A] = _rms(oa, goa_ref[...]).astype(BF16)
    o_ref[:, WIDTH_A:] = _rms(ob, gob_ref[...]).astype(BF16)


def _attn_sample(sinks, qa, ka_cache, ka, va_cache, va, qb, kb_cache, kb, vb_cache, vb,
                 rbv, goa, gob, *, t_s, keep_a, keep_b, layer, prev_new, emit):
    n_tok = qa.shape[0]
    n_seq = n_tok // t_s
    la, lb = ka_cache.shape[2], kb_cache.shape[2]
    assert t_s <= keep_a <= la + t_s and t_s <= keep_b <= lb + t_s
    row = lambda i: (i, 0)
    cache = lambda l, rows, width: pl.BlockSpec((None, 1, rows, width), lambda i: (l, i, 0, 0))
    new = lambda width: pl.BlockSpec((t_s, width), row)

    def kv_specs(l):
        return [cache(l, la, WIDTH_A), new(WIDTH_A), cache(l, la, WIDTH_A), new(WIDTH_A),
                cache(l, lb, KV_WIDTH_B), new(KV_WIDTH_B), cache(l, lb, KV_WIDTH_B), new(KV_WIDTH_B)]

    own = kv_specs(layer)
    in_specs = ([pl.BlockSpec(memory_space=pltpu.SMEM), new(WIDTH_A)] + own[:4] + [new(WIDTH_B)] + own[4:]
                + [_resident((N_HEADS_A, BIAS_PERIOD)), _resident((1, WIDTH_A)), _resident((1, WIDTH_B))])
    operands = [sinks, qa, ka_cache, ka, va_cache, va, qb, kb_cache, kb, vb_cache, vb, rbv, goa, gob]
    out_specs = [pl.BlockSpec((t_s, MIX_WIDTH), row)]
    out_shape = [jax.ShapeDtypeStruct((n_tok, MIX_WIDTH), BF16)]
    n_prev = 0
    if emit:
        n_prev = len(prev_new)
        for l, (ka_l, va_l, kb_l, vb_l) in enumerate(prev_new):
            in_specs += kv_specs(l)
            operands += [ka_cache, ka_l, va_cache, va_l, kb_cache, kb_l, vb_cache, vb_l]
        n_layers = n_prev + 1
        for keep, width in ((keep_a, WIDTH_A), (keep_a, WIDTH_A), (keep_b, KV_WIDTH_B), (keep_b, KV_WIDTH_B)):
            out_specs.append(pl.BlockSpec((n_layers, 1, keep, width), lambda i: (0, i, 0, 0)))
            out_shape.append(jax.ShapeDtypeStruct((n_layers, n_seq, keep, width), F32))
    return pl.pallas_call(
        functools.partial(_attn_sample_kernel, n_prev=n_prev, emit=emit),
        grid=(n_seq,),
        in_specs=in_specs,
        out_specs=out_specs,
        out_shape=out_shape,
        scratch_shapes=[pltpu.VMEM((N_HEADS_A, t_s, BIAS_PERIOD), F32)],
        compiler_params=pltpu.CompilerParams(
            dimension_semantics=("arbitrary",), vmem_limit_bytes=V7X_VMEM_LIMIT_BYTES),
        name="attn_sample",
    )(*operands)


TAIL_SUB_ROWS = 256


def _tail_kernel(h_ref, o_ref, p_ref, wout_ref, gffn_ref, wgu_ref, wdown_ref, wgate_ref,
                 wproj_ref, gfin_ref, out_ref, *, final, n_sub):
    rows = h_ref.shape[0] // n_sub
    for first in range(0, n_sub, 2):
        subs = [pl.ds(i * rows, rows) for i in range(first, first + 2)]
        h = [h_ref[sl, :] + _dot(o_ref[sl, :], wout_ref[...]) for sl in subs]
        pp = [_dot(p_ref[sl, :].astype(BF16), wproj_ref[...]) for sl in subs]
        gu = [_dot(_rms(x, gffn_ref[...]).astype(BF16), wgu_ref[...]) for x in h]
        act = [(jax.nn.silu(x[:, :D_FF]) * x[:, D_FF:]).astype(BF16) for x in gu]
        h = [x + _dot(a, wdown_ref[...]) for x, a in zip(h, act)]
        gate = [jax.nn.sigmoid(_dot(x.astype(BF16), wgate_ref[...])) for x in h]
        for sl, x, g, e in zip(subs, h, gate, pp):
            x = x + g * e
            if final:
                x = _rms(x, gfin_ref[...])
            out_ref[sl, :] = x


def _tail(h, o, p, wout, gffn, wgu, wdown, wgate, wproj, gfin, *, tm, final, layer):
    n_tok = h.shape[0]
    row = lambda i: (i, 0)
    return pl.pallas_call(
        functools.partial(_tail_kernel, final=final, n_sub=tm // TAIL_SUB_ROWS),
        grid=(n_tok // tm,),
        in_specs=[
            pl.BlockSpec((tm, D_MODEL), row),
            pl.BlockSpec((tm, MIX_WIDTH), row),
            pl.BlockSpec((None, tm, D_PLE), lambda i: (layer, i, 0)),
            _resident((MIX_WIDTH, D_MODEL), layer),
            _resident((1, D_MODEL)),
            _resident((D_MODEL, 2 * D_FF), layer),
            _resident((D_FF, D_MODEL), layer),
            _resident((D_MODEL, D_MODEL), layer),
            _resident((D_PLE, D_MODEL), layer),
            _resident((1, D_MODEL)),
        ],
        out_specs=pl.BlockSpec((tm, D_MODEL), row),
        out_shape=jax.ShapeDtypeStruct((n_tok, D_MODEL), F32),
        compiler_params=pltpu.CompilerParams(
            dimension_semantics=("arbitrary",), vmem_limit_bytes=V7X_VMEM_LIMIT_BYTES),
        name="layer_tail",
    )(h, o, p, wout, gffn, wgu, wdown, wgate, wproj, gfin)


CAST_ROWS = 256


def _cast_kernel(w_ref, o_ref):
    o_ref[...] = w_ref[...].astype(o_ref.dtype)


def _to_bf16(w):
    depth, rows, cols = w.shape
    blk = pl.BlockSpec((1, CAST_ROWS, cols), lambda d, r: (d, r, 0))
    return pl.pallas_call(
        _cast_kernel,
        grid=(depth, rows // CAST_ROWS),
        in_specs=[blk],
        out_specs=blk,
        out_shape=jax.ShapeDtypeStruct(w.shape, BF16),
        compiler_params=pltpu.CompilerParams(dimension_semantics=("arbitrary", "arbitrary")),
        name="cast_bf16",
    )(w)


def _rope_lane_tables(pos):
    half = ROT_DIM // 2
    d = np.arange(PAIR) % HEAD_DIM
    inv = ROPE_THETA ** (-(2.0 * (d % half)) / ROT_DIM)
    ang = np.where(d < ROT_DIM, np.asarray(pos, np.float64)[:, None] * inv[None, :], 0.0)
    lower = (d < half).astype(np.float64)[None, :]
    upper = ((d >= half) & (d < ROT_DIM)).astype(np.float64)[None, :]
    return np.cos(ang), np.sin(ang), lower, upper


def _rope_tables_direct(pos):
    cos, sin, lower, upper = _rope_lane_tables(pos)
    return jnp.asarray(np.stack([cos, -sin * lower, sin * upper]), F32)


def _rope_tables_split(n_tiles, tile):
    cr, sr, lower, upper = _rope_lane_tables(np.arange(tile))
    ct, st, _, _ = _rope_lane_tables(np.arange(n_tiles) * tile)
    rows = np.stack([cr, sr, -cr * lower, -sr * lower, cr * upper, sr * upper])
    return jnp.asarray(rows, F32), jnp.asarray(ct, F32), jnp.asarray(st, F32)


def _rel_bias_row(rel_bias):
    u = np.arange(BIAS_PERIOD)
    diff = np.where(u < BIAS_PERIOD // 2 + A_ROWS // 2, u, u - BIAS_PERIOD)
    idx = np.clip(A_ROWS - diff, -REL_CLIP, REL_CLIP) + REL_CLIP
    return rel_bias.astype(F32)[:, idx]


def kernel(x_prompt, x_sample, p_prompt, p_sample, cache_a_k, cache_a_v, cache_b_k, cache_b_v,
           g_mix_norm, w_in, rel_bias_a, sinks_b, g_out_a, g_out_b, w_out, g_ffn_norm,
           w_gate_up, w_down, w_ple_proj, w_ple_gate, g_final):
    b_p, s_p, _ = x_prompt.shape
    b_s, t_s, _ = x_sample.shape
    depth = w_in.shape[0]
    la_c, lb_c = cache_a_k.shape[2], cache_b_k.shape[2]
    keep_a_s = min(A_ROWS, la_c + t_s)
    keep_b_s = min(WINDOW_B, lb_c + t_s)
    tm_s = 512
    tm_tail = 512
    assert s_p % TQ == 0 and s_p >= A_ROWS and (b_s * t_s) % tm_s == 0 and tm_s % t_s == 0
    assert la_c == A_ROWS and lb_c == WINDOW_B and t_s == CHUNK

    rope_rows, rope_ct, rope_st = _rope_tables_split(s_p // TQ, TQ)
    rope_s = _rope_tables_direct(PAST_LEN + np.arange(tm_s) % t_s)
    row2 = lambda a: a.reshape(1, -1).astype(F32)
    g_fin = row2(g_final)

    hp = x_prompt
    hs = x_sample.reshape(b_s * t_s, D_MODEL)
    outs = [[] for _ in range(4)]
    new_rows = []
    w_in_b, w_out_b, w_gu_b, w_down_b, w_gate_b, w_proj_b = (
        _to_bf16(w) for w in (w_in, w_out, w_gate_up, w_down, w_ple_gate, w_ple_proj))
    p_prompt_f = p_prompt.reshape(depth, b_p * s_p, D_PLE)
    p_sample_f = p_sample.reshape(depth, b_s * t_s, D_PLE)
    cak_s = cache_a_k.reshape(depth, b_s, la_c, WIDTH_A)
    cav_s = cache_a_v.reshape(depth, b_s, la_c, WIDTH_A)
    cbk_s = cache_b_k.reshape(depth, b_s, lb_c, KV_WIDTH_B)
    cbv_s = cache_b_v.reshape(depth, b_s, lb_c, KV_WIDTH_B)
    for i in range(depth):
        tail_w = (w_out_b, row2(g_ffn_norm[i]), w_gu_b, w_down_b, w_gate_b, w_proj_b, g_fin)
        g_mix = row2(g_mix_norm[i])
        goa, gob = row2(g_out_a[i]), row2(g_out_b[i])
        sinks = sinks_b[i].astype(F32)
        rbv = _rel_bias_row(rel_bias_a[i])
        final = i == depth - 1

        o, cak, cav, cbk, cbv = _mix_prompt(sinks, hp, g_mix, w_in_b, rope_rows, rope_ct, rope_st,
                                            rbv, goa, gob, layer=i)
        hp = _tail(hp.reshape(b_p * s_p, D_MODEL), o.reshape(b_p * s_p, MIX_WIDTH),
                   p_prompt_f, *tail_w, tm=tm_tail, final=final, layer=i)
        hp = hp.reshape(b_p, s_p, D_MODEL)
        outs[0].append(cak.reshape(b_p, A_ROWS, N_HEADS_A, HEAD_DIM))
        outs[1].append(cav.reshape(b_p, A_ROWS, N_HEADS_A, HEAD_DIM))
        outs[2].append(cbk.reshape(b_p, WINDOW_B, N_KV_B, HEAD_DIM))
        outs[3].append(cbv.reshape(b_p, WINDOW_B, N_KV_B, HEAD_DIM))

        qa, ka, va, qb, kb, vb = _in_proj(hs, g_mix, w_in_b, rope_s, tm=tm_s, layer=i)
        o, *rolled = _attn_sample(sinks, qa, cak_s, ka, cav_s, va, qb, cbk_s, kb, cbv_s, vb,
                                  rbv, goa, gob, t_s=t_s, keep_a=keep_a_s, keep_b=keep_b_s,
                                  layer=i, prev_new=new_rows, emit=final)
        new_rows.append((ka, va, kb, vb))
        hs = _tail(hs, o, p_sample_f, *tail_w, tm=tm_tail, final=final, layer=i)

    nak, nav, nbk, nbv = rolled
    y_sample = hs.reshape(b_s, t_s, D_MODEL)
    return ((hp, y_sample) + tuple(jnp.stack(o) for o in outs)
            + (nak.reshape(depth, b_s, keep_a_s, N_HEADS_A, HEAD_DIM),
               nav.reshape(depth, b_s, keep_a_s, N_HEADS_A, HEAD_DIM),
               nbk.reshape(depth, b_s, keep_b_s, N_KV_B, HEAD_DIM),
               nbv.reshape(depth, b_s, keep_b_s, N_KV_B, HEAD_DIM)))
```

```python
import functools
import math

import numpy as np
import jax
import jax.numpy as jnp
from jax import lax
from jax.experimental import pallas as pl
from jax.experimental.pallas import tpu as pltpu

D_MODEL = 1024
CHUNK = 64
HEAD_DIM = 64
N_HEADS_A = 8
N_HEADS_B = 8
N_KV_B = 2
GROUP_B = N_HEADS_B // N_KV_B
WIDTH_A = N_HEADS_A * HEAD_DIM
WIDTH_B = N_HEADS_B * HEAD_DIM
KV_WIDTH_B = N_KV_B * HEAD_DIM
MIX_WIDTH = WIDTH_A + WIDTH_B
IN_WIDTH = 3 * WIDTH_A + WIDTH_B + 2 * KV_WIDTH_B
PREV_CHUNKS_A = 8
A_ROWS = PREV_CHUNKS_A * CHUNK
REL_CLIP = 256
WINDOW_B = 128
PREV_CHUNKS_B = WINDOW_B // CHUNK
ROT_DIM = HEAD_DIM // 4
ROPE_THETA = 500000.0
D_FF = 2816
D_PLE = 256
PAST_LEN = 2048
RMS_EPS = 1e-6
LOG2E = math.log2(math.e)
QSCALE = HEAD_DIM ** -0.5 * LOG2E
PAIR = 2 * HEAD_DIM
N_PAIRS_A = WIDTH_A // PAIR
N_PAIRS_B = WIDTH_B // PAIR
PAIRS_PER_KV = GROUP_B // 2
NEG_INF = float("-inf")
BIAS_PERIOD = 1024

V7X_VMEM_LIMIT_BYTES = 56 * 1024 * 1024

BF16 = jnp.bfloat16
F32 = jnp.float32

O_KA, O_VA, O_QB = WIDTH_A, 2 * WIDTH_A, 3 * WIDTH_A
O_KB = O_QB + WIDTH_B
O_VB = O_KB + KV_WIDTH_B


def _rms(x, g):
    return x * lax.rsqrt(jnp.mean(x * x, axis=-1, keepdims=True) + RMS_EPS) * g


def _dot(a, b):
    return jnp.dot(a, b, preferred_element_type=F32)


def _dot_nt(a, b):
    return lax.dot_general(a, b, (((1,), (1,)), ((), ())), preferred_element_type=F32)


def _resident(shape, layer=None):
    nd = len(shape)
    if layer is None:
        return pl.BlockSpec(shape, lambda *_: (0,) * nd, pipeline_mode=pl.Buffered(1))
    return pl.BlockSpec((None,) + tuple(shape), lambda *_: (layer,) + (0,) * nd,
                        pipeline_mode=pl.Buffered(1))


def _rope(x, cos, sdn, sup):
    half = ROT_DIM // 2
    return (x * cos + pltpu.roll(x, PAIR - half, axis=1) * sdn
            + pltpu.roll(x, half, axis=1) * sup)


def _low_lanes(shape):
    return lax.broadcasted_iota(jnp.int32, shape, 1) < HEAD_DIM


def _keep_half(x, half, fill):
    keep = _low_lanes(x.shape) if half == 0 else ~_low_lanes(x.shape)
    return jnp.where(keep, x, jnp.full_like(x, fill))


def _exp_scores(s_blocks, sink=None):
    m = functools.reduce(jnp.maximum, [jnp.max(s, axis=-1, keepdims=True) for s in s_blocks])
    if sink is not None:
        m = jnp.maximum(m, sink)
    p = jnp.concatenate([jnp.exp2(s - m).astype(BF16) for s in s_blocks], axis=-1)
    return p, (None if sink is None else jnp.exp2(sink - m))


def _pv_stacked(probs, v_ext, extras=None):
    m = probs[0].shape[0]
    x = _dot(jnp.concatenate(probs, axis=0), v_ext)
    outs = []
    for i in range(len(probs)):
        den = x[i * m:(i + 1) * m, PAIR:]
        if extras is not None:
            den = den + extras[i]
        outs.append(x[i * m:(i + 1) * m, :PAIR] / den)
    return outs


def _rolled_bias_rows(rbv_ref, h, rows):
    x = jnp.broadcast_to(rbv_ref[h:h + 1, :] * LOG2E, (rows, BIAS_PERIOD))
    return pltpu.roll(x, 0, axis=1, stride=1, stride_axis=0)


def _band_valid(rows, col0, cols, prev_chunks):
    r = lax.broadcasted_iota(jnp.int32, (rows, cols), 0) // CHUNK
    c = (lax.broadcasted_iota(jnp.int32, (rows, cols), 1) + col0) // CHUNK
    return (c >= r) & (c <= r + prev_chunks)


TQ = 256
NBLK_A = A_ROWS // TQ + 1
WIN_A = NBLK_A * TQ
WIN_B = WINDOW_B + TQ
NEG_KIND = NBLK_A
HALF_B = WINDOW_B
KEYS_B = HALF_B + WINDOW_B


def _mix_prompt_kernel(sinks_ref, x_ref, g_ref, w_ref, rope_ref, ct_ref, st_ref, rbv_ref,
                       goa_ref, gob_ref,
                       o_ref, cak_ref, cav_ref, cbk_ref, cbv_ref,
                       ka_scr, va_scr, kbp_scr, vbp_scr, bias_scr, maskb_scr):
    b, t = pl.program_id(0), pl.program_id(1)

    @pl.when((b == 0) & (t == 0))
    def _init():
        ka_scr[...] = jnp.zeros_like(ka_scr)
        kbp_scr[...] = jnp.zeros_like(kbp_scr)
        for v_scr in (va_scr, vbp_scr):
            v_scr[..., :PAIR] = jnp.zeros(v_scr.shape[:-1] + (PAIR,), BF16)
            v_scr[..., PAIR:] = jnp.ones(v_scr.shape[:-1] + (PAIR,), BF16)
        for h in range(N_HEADS_A):
            rows = _rolled_bias_rows(rbv_ref, h, TQ)
            for j in range(NBLK_A):
                valid = _band_valid(TQ, j * TQ, TQ, PREV_CHUNKS_A)
                bias_scr[j, h] = jnp.where(valid, rows[:, j * TQ:(j + 1) * TQ], NEG_INF)
            bias_scr[NEG_KIND, h] = jnp.full((TQ, TQ), NEG_INF, F32)
        band = jnp.where(_band_valid(HALF_B, 0, KEYS_B, PREV_CHUNKS_B),
                         jnp.zeros((HALF_B, KEYS_B), F32), NEG_INF)
        col = lax.broadcasted_iota(jnp.int32, (HALF_B, KEYS_B), 1)
        maskb_scr[0] = band
        maskb_scr[1] = jnp.where(col >= WINDOW_B, band, NEG_INF)

    n = _rms(x_ref[0], g_ref[...]).astype(BF16)
    slot = lax.rem(t, NBLK_A)
    row0 = pl.multiple_of(slot * TQ, TQ)
    ka = _dot(n, w_ref[:, O_KA:O_VA])
    for p in range(N_PAIRS_A):
        ka_scr[p, pl.ds(row0, TQ), :] = ka[:, p * PAIR:(p + 1) * PAIR].astype(BF16)
    za = _dot(n, w_ref[:, :O_KA])
    qa = [[_keep_half(za[:, j * PAIR:(j + 1) * PAIR] * QSCALE, half, 0.0).astype(BF16)
           for half in range(2)] for j in range(N_PAIRS_A)]
    va = _dot(n, w_ref[:, O_VA:O_QB])
    for p in range(N_PAIRS_A):
        va_scr[p, pl.ds(row0, TQ), :PAIR] = va[:, p * PAIR:(p + 1) * PAIR].astype(BF16)
    zb = _dot(n, w_ref[:, O_QB:])
    ct, st = ct_ref[pl.ds(t, 1), :], st_ref[pl.ds(t, 1), :]
    cos = ct * rope_ref[0] - st * rope_ref[1]
    sdn = st * rope_ref[2] + ct * rope_ref[3]
    sup = st * rope_ref[4] + ct * rope_ref[5]
    kb = _rope(zb[:, WIDTH_B:WIDTH_B + KV_WIDTH_B], cos, sdn, sup)
    vb = zb[:, WIDTH_B + KV_WIDTH_B:]
    qb = []
    for j in range(N_PAIRS_B):
        q = _rope(zb[:, j * PAIR:(j + 1) * PAIR], cos, sdn, sup) * QSCALE
        qb.append([_keep_half(q, half, 0.0).astype(BF16) for half in range(2)])

    cak_ref[0] = ka
    cav_ref[0] = va
    cbk_ref[0] = kb[TQ - WINDOW_B:, :]
    cbv_ref[0] = vb[TQ - WINDOW_B:, :]

    kinds = []
    for ps in range(NBLK_A):
        d = lax.rem(t - ps + NBLK_A, NBLK_A)
        kinds.append(jnp.where(t >= d, NBLK_A - 1 - d, NEG_KIND))
    low = _low_lanes((TQ, PAIR))
    oa = []
    for p in range(N_PAIRS_A):
        s = _dot_nt(jnp.concatenate(qa[p], axis=0), ka_scr[p])
        probs = []
        for half in range(2):
            s_blocks = [s[half * TQ:(half + 1) * TQ, ps * TQ:(ps + 1) * TQ]
                        + bias_scr[kinds[ps], 2 * p + half] for ps in range(NBLK_A)]
            probs.append(_exp_scores(s_blocks)[0])
        o = _pv_stacked(probs, va_scr[p])
        oa.append(jnp.where(low, o[0], o[1]))
    oa = jnp.concatenate(oa, axis=-1)

    kb_sw, vb_sw = pltpu.roll(kb, HEAD_DIM, axis=1), pltpu.roll(vb, HEAD_DIM, axis=1)
    ones = jnp.ones((TQ, PAIR), BF16)
    slot_b = lax.rem(t, 2)
    mask_first = maskb_scr[jnp.where(t >= 1, 0, 1)]
    ob_rows = [[[None] * (TQ // HALF_B) for _ in range(2)] for _ in range(N_PAIRS_B)]
    for c, (k_c, v_c) in enumerate(((kb, vb), (kb_sw, vb_sw))):
        k_c, v_c = k_c.astype(BF16), v_c.astype(BF16)
        kbp_scr[c, slot_b] = k_c[TQ - WINDOW_B:, :]
        vbp_scr[c, slot_b, :, :PAIR] = v_c[TQ - WINDOW_B:, :]
        k_win = jnp.concatenate([kbp_scr[c, 1 - slot_b], k_c], axis=0)
        v_win = jnp.concatenate([vbp_scr[c, 1 - slot_b], jnp.concatenate([v_c, ones], axis=1)], axis=0)
        heads = [(p, half) for p in range(N_PAIRS_B) for half in range(2)
                 if (half == p // PAIRS_PER_KV) == (c == 0)]
        for r in range(TQ // HALF_B):
            rows, keys = slice(r * HALF_B, (r + 1) * HALF_B), slice(r * HALF_B, r * HALF_B + KEYS_B)
            mask = mask_first if r == 0 else maskb_scr[0]
            s = _dot_nt(jnp.concatenate([qb[p][half][rows] for p, half in heads], axis=0), k_win[keys])
            probs, extras = [], []
            for i, (p, half) in enumerate(heads):
                pr, ex = _exp_scores([s[i * HALF_B:(i + 1) * HALF_B, :] + mask],
                                     sinks_ref[2 * p + half] * LOG2E)
                probs.append(pr)
                extras.append(ex)
            for (p, half), o in zip(heads, _pv_stacked(probs, v_win[keys], extras)):
                ob_rows[p][half][r] = o
    ob = jnp.concatenate([jnp.where(low, jnp.concatenate(o[0], axis=0), jnp.concatenate(o[1], axis=0))
                          for o in ob_rows], axis=-1)

    o_ref[0, :, :WIDTH_A] = _rms(oa, goa_ref[...]).astype(BF16)
    o_ref[0, :, WIDTH_A:] = _rms(ob, gob_ref[...]).astype(BF16)


def _mix_prompt(sinks, x, g, w_bf16, rope_rows, ct, st, rbv, goa, gob, *, layer):
    b, s, _ = x.shape
    nt = s // TQ
    cur = lambda bi, t: (bi, t, 0)
    keep_a = lambda bi, t: (bi, jnp.maximum(t - (nt - A_ROWS // TQ), 0), 0)
    seq = lambda bi, t: (bi, 0, 0)
    return pl.pallas_call(
        _mix_prompt_kernel,
        grid=(b, nt),
        in_specs=[
            pl.BlockSpec(memory_space=pltpu.SMEM),
            pl.BlockSpec((1, TQ, D_MODEL), cur),
            _resident((1, D_MODEL)),
            _resident((D_MODEL, IN_WIDTH), layer),
            _resident(rope_rows.shape),
            _resident(ct.shape),
            _resident(st.shape),
            _resident((N_HEADS_A, BIAS_PERIOD)),
            _resident((1, WIDTH_A)),
            _resident((1, WIDTH_B)),
        ],
        out_specs=[
            pl.BlockSpec((1, TQ, MIX_WIDTH), cur),
            pl.BlockSpec((1, TQ, WIDTH_A), keep_a),
            pl.BlockSpec((1, TQ, WIDTH_A), keep_a),
            pl.BlockSpec((1, WINDOW_B, KV_WIDTH_B), seq),
            pl.BlockSpec((1, WINDOW_B, KV_WIDTH_B), seq),
        ],
        out_shape=[
            jax.ShapeDtypeStruct((b, s, MIX_WIDTH), BF16),
            jax.ShapeDtypeStruct((b, A_ROWS, WIDTH_A), F32),
            jax.ShapeDtypeStruct((b, A_ROWS, WIDTH_A), F32),
            jax.ShapeDtypeStruct((b, WINDOW_B, KV_WIDTH_B), F32),
            jax.ShapeDtypeStruct((b, WINDOW_B, KV_WIDTH_B), F32),
        ],
        scratch_shapes=[
            pltpu.VMEM((N_PAIRS_A, WIN_A, PAIR), BF16),
            pltpu.VMEM((N_PAIRS_A, WIN_A, 2 * PAIR), BF16),
            pltpu.VMEM((2, 2, WINDOW_B, PAIR), BF16),
            pltpu.VMEM((2, 2, WINDOW_B, 2 * PAIR), BF16),
            pltpu.VMEM((NBLK_A + 1, N_HEADS_A, TQ, TQ), F32),
            pltpu.VMEM((2, HALF_B, KEYS_B), F32),
        ],
        compiler_params=pltpu.CompilerParams(
            dimension_semantics=("arbitrary", "arbitrary"),
            vmem_limit_bytes=V7X_VMEM_LIMIT_BYTES),
        name="mix_prompt",
    )(sinks, x, g, w_bf16, rope_rows, ct, st, rbv, goa, gob)


def _in_proj_kernel(x_ref, g_ref, w_ref, rope_ref, qa_ref, ka_ref, va_ref, qb_ref, kb_ref, vb_ref):
    n = _rms(x_ref[...], g_ref[...]).astype(BF16)
    z = _dot(n, w_ref[...])
    cos, sdn, sup = rope_ref[0], rope_ref[1], rope_ref[2]
    qa_ref[...] = (z[:, :O_KA] * QSCALE).astype(BF16)
    ka_ref[...] = z[:, O_KA:O_VA]
    va_ref[...] = z[:, O_VA:O_QB]
    for j in range(N_PAIRS_B):
        sl = slice(O_QB + j * PAIR, O_QB + (j + 1) * PAIR)
        qb_ref[:, j * PAIR:(j + 1) * PAIR] = (_rope(z[:, sl], cos, sdn, sup) * QSCALE).astype(BF16)
    kb_ref[...] = _rope(z[:, O_KB:O_VB], cos, sdn, sup)
    vb_ref[...] = z[:, O_VB:]


def _in_proj(x, g, w_bf16, rope_tabs, *, tm, layer):
    n_tok = x.shape[0]
    row = lambda i: (i, 0)
    widths = (WIDTH_A, WIDTH_A, WIDTH_A, WIDTH_B, KV_WIDTH_B, KV_WIDTH_B)
    dtypes = (BF16, F32, F32, BF16, F32, F32)
    return pl.pallas_call(
        _in_proj_kernel,
        grid=(n_tok // tm,),
        in_specs=[
            pl.BlockSpec((tm, D_MODEL), row),
            _resident((1, D_MODEL)),
            _resident((D_MODEL, IN_WIDTH), layer),
            _resident(rope_tabs.shape),
        ],
        out_specs=[pl.BlockSpec((tm, w), row) for w in widths],
        out_shape=[jax.ShapeDtypeStruct((n_tok, w), d) for w, d in zip(widths, dtypes)],
        compiler_params=pltpu.CompilerParams(
            dimension_semantics=("arbitrary",), vmem_limit_bytes=V7X_VMEM_LIMIT_BYTES),
        name="in_proj",
    )(x, g, w_bf16, rope_tabs)


N_SAMPLE_IN = 14
N_ROLL_IN = 8


def _attn_sample_kernel(*refs, n_prev, emit):
    (sinks_ref, qa_ref, kac_ref, kan_ref, vac_ref, van_ref, qb_ref, kbc_ref, kbn_ref, vbc_ref,
     vbn_ref, rbv_ref, goa_ref, gob_ref) = refs[:N_SAMPLE_IN]
    n_in = N_SAMPLE_IN + N_ROLL_IN * n_prev
    o_ref, bias_scr = refs[n_in], refs[-1]
    t_s = qa_ref.shape[0]
    la = kac_ref.shape[1]

    @pl.when(pl.program_id(0) == 0)
    def _init():
        for h in range(N_HEADS_A):
            bias_scr[h] = _rolled_bias_rows(rbv_ref, h, t_s)

    if emit:
        own = (kac_ref, kan_ref, vac_ref, van_ref, kbc_ref, kbn_ref, vbc_ref, vbn_ref)
        layers = [refs[N_SAMPLE_IN + N_ROLL_IN * l:N_SAMPLE_IN + N_ROLL_IN * (l + 1)]
                  for l in range(n_prev)] + [own]
        for l, lr in enumerate(layers):
            for j, new_ref in enumerate(refs[n_in + 1:n_in + 5]):
                cache_ref, fresh_ref = lr[2 * j], lr[2 * j + 1]
                kept = new_ref.shape[2] - t_s
                new_ref[l, 0, :kept, :] = cache_ref[0, cache_ref.shape[1] - kept:, :]
                new_ref[l, 0, kept:, :] = fresh_ref[...]

    low = _low_lanes((t_s, PAIR))
    ones_a = jnp.ones((la + t_s, PAIR), BF16)
    oa = []
    for p in range(N_PAIRS_A):
        sl = slice(p * PAIR, (p + 1) * PAIR)
        qp = qa_ref[:, sl]
        k = jnp.concatenate([kac_ref[0, :, sl], kan_ref[:, sl]], axis=0).astype(BF16)
        v = jnp.concatenate([vac_ref[0, :, sl], van_ref[:, sl]], axis=0).astype(BF16)
        s = _dot_nt(jnp.concatenate([_keep_half(qp, half, 0.0) for half in range(2)], axis=0), k)
        probs = [_exp_scores([s[half * t_s:(half + 1) * t_s, :] + bias_scr[2 * p + half, :, :la + t_s]])[0]
                 for half in range(2)]
        o = _pv_stacked(probs, jnp.concatenate([v, ones_a], axis=1))
        oa.append(jnp.where(low, o[0], o[1]))
    oa = jnp.concatenate(oa, axis=-1)

    kb = jnp.concatenate([kbc_ref[0], kbn_ref[...]], axis=0)
    vb = jnp.concatenate([vbc_ref[0], vbn_ref[...]], axis=0)
    ones_b = jnp.ones((kb.shape[0], PAIR), BF16)
    ob_half = [[None, None] for _ in range(N_PAIRS_B)]
    for c in range(2):
        k_c = (kb if c == 0 else pltpu.roll(kb, HEAD_DIM, axis=1)).astype(BF16)
        v_c = (vb if c == 0 else pltpu.roll(vb, HEAD_DIM, axis=1)).astype(BF16)
        heads = [(p, half) for p in range(N_PAIRS_B) for half in range(2)
                 if (half == p // PAIRS_PER_KV) == (c == 0)]
        s = _dot_nt(jnp.concatenate([_keep_half(qb_ref[:, p * PAIR:(p + 1) * PAIR], half, 0.0)
                                     for p, half in heads], axis=0), k_c)
        probs, extras = [], []
        for i, (p, half) in enumerate(heads):
            pr, ex = _exp_scores([s[i * t_s:(i + 1) * t_s, :]], sinks_ref[2 * p + half] * LOG2E)
            probs.append(pr)
            extras.append(ex)
        for (p, half), o in zip(heads, _pv_stacked(probs, jnp.concatenate([v_c, ones_b], axis=1), extras)):
            ob_half[p][half] = o
    ob = jnp.concatenate([jnp.where(low, o[0], o[1]) for o in ob_half], axis=-1)
    o_ref[:, :WIDTH_A] = _rms(oa, goa_ref[...]).astype(BF16)
    o_ref[:, WIDTH_A:] = _rms(ob, gob_ref[...]).astype(BF16)


def _attn_sample(sinks, qa, ka_cache, ka, va_cache, va, qb, kb_cache, kb, vb_cache, vb,
                 rbv, goa, gob, *, t_s, keep_a, keep_b, layer, prev_new, emit):
    n_tok = qa.shape[0]
    n_seq = n_tok // t_s
    la, lb = ka_cache.shape[2], kb_cache.shape[2]
    assert t_s <= keep_a <= la + t_s and t_s <= keep_b <= lb + t_s
    row = lambda i: (i, 0)
    cache = lambda l, rows, width: pl.BlockSpec((None, 1, rows, width), lambda i: (l, i, 0, 0))
    new = lambda width: pl.BlockSpec((t_s, width), row)

    def kv_specs(l):
        return [cache(l, la, WIDTH_A), new(WIDTH_A), cache(l, la, WIDTH_A), new(WIDTH_A),
                cache(l, lb, KV_WIDTH_B), new(KV_WIDTH_B), cache(l, lb, KV_WIDTH_B), new(KV_WIDTH_B)]

    own = kv_specs(layer)
    in_specs = ([pl.BlockSpec(memory_space=pltpu.SMEM), new(WIDTH_A)] + own[:4] + [new(WIDTH_B)] + own[4:]
                + [_resident((N_HEADS_A, BIAS_PERIOD)), _resident((1, WIDTH_A)), _resident((1, WIDTH_B))])
    operands = [sinks, qa, ka_cache, ka, va_cache, va, qb, kb_cache, kb, vb_cache, vb, rbv, goa, gob]
    out_specs = [pl.BlockSpec((t_s, MIX_WIDTH), row)]
    out_shape = [jax.ShapeDtypeStruct((n_tok, MIX_WIDTH), BF16)]
    n_prev = 0
    if emit:
        n_prev = len(prev_new)
        for l, (ka_l, va_l, kb_l, vb_l) in enumerate(prev_new):
            in_specs += kv_specs(l)
            operands += [ka_cache, ka_l, va_cache, va_l, kb_cache, kb_l, vb_cache, vb_l]
        n_layers = n_prev + 1
        for keep, width in ((keep_a, WIDTH_A), (keep_a, WIDTH_A), (keep_b, KV_WIDTH_B), (keep_b, KV_WIDTH_B)):
            out_specs.append(pl.BlockSpec((n_layers, 1, keep, width), lambda i: (0, i, 0, 0)))
            out_shape.append(jax.ShapeDtypeStruct((n_layers, n_seq, keep, width), F32))
    return pl.pallas_call(
        functools.partial(_attn_sample_kernel, n_prev=n_prev, emit=emit),
        grid=(n_seq,),
        in_specs=in_specs,
        out_specs=out_specs,
        out_shape=out_shape,
        scratch_shapes=[pltpu.VMEM((N_HEADS_A, t_s, BIAS_PERIOD), F32)],
        compiler_params=pltpu.CompilerParams(
            dimension_semantics=("arbitrary",), vmem_limit_bytes=V7X_VMEM_LIMIT_BYTES),
        name="attn_sample",
    )(*operands)


TAIL_SUB_ROWS = 256


def _tail_kernel(h_ref, o_ref, p_ref, wout_ref, gffn_ref, wgu_ref, wdown_ref, wgate_ref,
                 wproj_ref, gfin_ref, out_ref, *, final, n_sub):
    rows = h_ref.shape[0] // n_sub
    for first in range(0, n_sub, 2):
        subs = [pl.ds(i * rows, rows) for i in range(first, first + 2)]
        h = [h_ref[sl, :] + _dot(o_ref[sl, :], wout_ref[...]) for sl in subs]
        pp = [_dot(p_ref[sl, :].astype(BF16), wproj_ref[...]) for sl in subs]
        gu = [_dot(_rms(x, gffn_ref[...]).astype(BF16), wgu_ref[...]) for x in h]
        act = [(jax.nn.silu(x[:, :D_FF]) * x[:, D_FF:]).astype(BF16) for x in gu]
        h = [x + _dot(a, wdown_ref[...]) for x, a in zip(h, act)]
        gate = [jax.nn.sigmoid(_dot(x.astype(BF16), wgate_ref[...])) for x in h]
        for sl, x, g, e in zip(subs, h, gate, pp):
            x = x + g * e
            if final:
                x = _rms(x, gfin_ref[...])
            out_ref[sl, :] = x


def _tail(h, o, p, wout, gffn, wgu, wdown, wgate, wproj, gfin, *, tm, final, layer):
    n_tok = h.shape[0]
    row = lambda i: (i, 0)
    return pl.pallas_call(
        functools.partial(_tail_kernel, final=final, n_sub=tm // TAIL_SUB_ROWS),
        grid=(n_tok // tm,),
        in_specs=[
            pl.BlockSpec((tm, D_MODEL), row),
            pl.BlockSpec((tm, MIX_WIDTH), row),
            pl.BlockSpec((None, tm, D_PLE), lambda i: (layer, i, 0)),
            _resident((MIX_WIDTH, D_MODEL), layer),
            _resident((1, D_MODEL)),
            _resident((D_MODEL, 2 * D_FF), layer),
            _resident((D_FF, D_MODEL), layer),
            _resident((D_MODEL, D_MODEL), layer),
            _resident((D_PLE, D_MODEL), layer),
            _resident((1, D_MODEL)),
        ],
        out_specs=pl.BlockSpec((tm, D_MODEL), row),
        out_shape=jax.ShapeDtypeStruct((n_tok, D_MODEL), F32),
        compiler_params=pltpu.CompilerParams(
            dimension_semantics=("arbitrary",), vmem_limit_bytes=V7X_VMEM_LIMIT_BYTES),
        name="layer_tail",
    )(h, o, p, wout, gffn, wgu, wdown, wgate, wproj, gfin)


CAST_ROWS = 256


def _cast_kernel(w_ref, o_ref):
    o_ref[...] = w_ref[...].astype(o_ref.dtype)


def _to_bf16(w):
    depth, rows, cols = w.shape
    blk = pl.BlockSpec((1, CAST_ROWS, cols), lambda d, r: (d, r, 0))
    return pl.pallas_call(
        _cast_kernel,
        grid=(depth, rows // CAST_ROWS),
        in_specs=[blk],
        out_specs=blk,
        out_shape=jax.ShapeDtypeStruct(w.shape, BF16),
        compiler_params=pltpu.CompilerParams(dimension_semantics=("arbitrary", "arbitrary")),
        name="cast_bf16",
    )(w)


def _rope_lane_tables(pos):
    half = ROT_DIM // 2
    d = np.arange(PAIR) % HEAD_DIM
    inv = ROPE_THETA ** (-(2.0 * (d % half)) / ROT_DIM)
    ang = np.where(d < ROT_DIM, np.asarray(pos, np.float64)[:, None] * inv[None, :], 0.0)
    lower = (d < half).astype(np.float64)[None, :]
    upper = ((d >= half) & (d < ROT_DIM)).astype(np.float64)[None, :]
    return np.cos(ang), np.sin(ang), lower, upper


def _rope_tables_direct(pos):
    cos, sin, lower, upper = _rope_lane_tables(pos)
    return jnp.asarray(np.stack([cos, -sin * lower, sin * upper]), F32)


def _rope_tables_split(n_tiles, tile):
    cr, sr, lower, upper = _rope_lane_tables(np.arange(tile))
    ct, st, _, _ = _rope_lane_tables(np.arange(n_tiles) * tile)
    rows = np.stack([cr, sr, -cr * lower, -sr * lower, cr * upper, sr * upper])
    return jnp.asarray(rows, F32), jnp.asarray(ct, F32), jnp.asarray(st, F32)


def _rel_bias_row(rel_bias):
    u = np.arange(BIAS_PERIOD)
    diff = np.where(u < BIAS_PERIOD // 2 + A_ROWS // 2, u, u - BIAS_PERIOD)
    idx = np.clip(A_ROWS - diff, -REL_CLIP, REL_CLIP) + REL_CLIP
    return rel_bias.astype(F32)[:, idx]


def kernel(x_prompt, x_sample, p_prompt, p_sample, cache_a_k, cache_a_v, cache_b_k, cache_b_v,
           g_mix_norm, w_in, rel_bias_a, sinks_b, g_out_a, g_out_b, w_out, g_ffn_norm,
           w_gate_up, w_down, w_ple_proj, w_ple_gate, g_final):
    b_p, s_p, _ = x_prompt.shape
    b_s, t_s, _ = x_sample.shape
    depth = w_in.shape[0]
    la_c, lb_c = cache_a_k.shape[2], cache_b_k.shape[2]
    keep_a_s = min(A_ROWS, la_c + t_s)
    keep_b_s = min(WINDOW_B, lb_c + t_s)
    tm_s = 512
    tm_tail = 512
    assert s_p % TQ == 0 and s_p >= A_ROWS and (b_s * t_s) % tm_s == 0 and tm_s % t_s == 0
    assert la_c == A_ROWS and lb_c == WINDOW_B and t_s == CHUNK

    rope_rows, rope_ct, rope_st = _rope_tables_split(s_p // TQ, TQ)
    rope_s = _rope_tables_direct(PAST_LEN + np.arange(tm_s) % t_s)
    row2 = lambda a: a.reshape(1, -1).astype(F32)
    g_fin = row2(g_final)

    hp = x_prompt
    hs = x_sample.reshape(b_s * t_s, D_MODEL)
    outs = [[] for _ in range(4)]
    new_rows = []
    w_in_b, w_out_b, w_gu_b, w_down_b, w_gate_b, w_proj_b = (
        _to_bf16(w) for w in (w_in, w_out, w_gate_up, w_down, w_ple_gate, w_ple_proj))
    p_prompt_f = p_prompt.reshape(depth, b_p * s_p, D_PLE)
    p_sample_f = p_sample.reshape(depth, b_s * t_s, D_PLE)
    cak_s = cache_a_k.reshape(depth, b_s, la_c, WIDTH_A)
    cav_s = cache_a_v.reshape(depth, b_s, la_c, WIDTH_A)
    cbk_s = cache_b_k.reshape(depth, b_s, lb_c, KV_WIDTH_B)
    cbv_s = cache_b_v.reshape(depth, b_s, lb_c, KV_WIDTH_B)
    for i in range(depth):
        tail_w = (w_out_b, row2(g_ffn_norm[i]), w_gu_b, w_down_b, w_gate_b, w_proj_b, g_fin)
        g_mix = row2(g_mix_norm[i])
        goa, gob = row2(g_out_a[i]), row2(g_out_b[i])
        sinks = sinks_b[i].astype(F32)
        rbv = _rel_bias_row(rel_bias_a[i])
        final = i == depth - 1

        o, cak, cav, cbk, cbv = _mix_prompt(sinks, hp, g_mix, w_in_b, rope_rows, rope_ct, rope_st,
                                            rbv, goa, gob, layer=i)
        hp = _tail(hp.reshape(b_p * s_p, D_MODEL), o.reshape(b_p * s_p, MIX_WIDTH),
                   p_prompt_f, *tail_w, tm=tm_tail, final=final, layer=i)
        hp = hp.reshape(b_p, s_p, D_MODEL)
        outs[0].append(cak.reshape(b_p, A_ROWS, N_HEADS_A, HEAD_DIM))
        outs[1].append(cav.reshape(b_p, A_ROWS, N_HEADS_A, HEAD_DIM))
        outs[2].append(cbk.reshape(b_p, WINDOW_B, N_KV_B, HEAD_DIM))
        outs[3].append(cbv.reshape(b_p, WINDOW_B, N_KV_B, HEAD_DIM))

        qa, ka, va, qb, kb, vb = _in_proj(hs, g_mix, w_in_b, rope_s, tm=tm_s, layer=i)
        o, *rolled = _attn_sample(sinks, qa, cak_s, ka, cav_s, va, qb, cbk_s, kb, cbv_s, vb,
                                  rbv, goa, gob, t_s=t_s, keep_a=keep_a_s, keep_b=keep_b_s,
                                  layer=i, prev_new=new_rows, emit=final)
        new_rows.append((ka, va, kb, vb))
        hs = _tail(hs, o, p_sample_f, *tail_w, tm=tm_tail, final=final, layer=i)

    nak, nav, nbk, nbv = rolled
    y_sample = hs.reshape(b_s, t_s, D_MODEL)
    return ((hp, y_sample) + tuple(jnp.stack(o) for o in outs)
            + (nak.reshape(depth, b_s, keep_a_s, N_HEADS_A, HEAD_DIM),
               nav.reshape(depth, b_s, keep_a_s, N_HEADS_A, HEAD_DIM),
               nbk.reshape(depth, b_s, keep_b_s, N_KV_B, HEAD_DIM),
               nbv.reshape(depth, b_s, keep_b_s, N_KV_B, HEAD_DIM)))
```

```python
import functools
import math

import numpy as np
import jax
import jax.numpy as jnp
from jax import lax
from jax.experimental import pallas as pl
from jax.experimental.pallas import tpu as pltpu

D_MODEL = 1024
CHUNK = 64
HEAD_DIM = 64
N_HEADS_A = 8
N_HEADS_B = 8
N_KV_B = 2
GROUP_B = N_HEADS_B // N_KV_B
WIDTH_A = N_HEADS_A * HEAD_DIM
WIDTH_B = N_HEADS_B * HEAD_DIM
KV_WIDTH_B = N_KV_B * HEAD_DIM
MIX_WIDTH = WIDTH_A + WIDTH_B
IN_WIDTH = 3 * WIDTH_A + WIDTH_B + 2 * KV_WIDTH_B
PREV_CHUNKS_A = 8
A_ROWS = PREV_CHUNKS_A * CHUNK
REL_CLIP = 256
WINDOW_B = 128
PREV_CHUNKS_B = WINDOW_B // CHUNK
ROT_DIM = HEAD_DIM // 4
ROPE_THETA = 500000.0
D_FF = 2816
D_PLE = 256
PAST_LEN = 2048
RMS_EPS = 1e-6
LOG2E = math.log2(math.e)
QSCALE = HEAD_DIM ** -0.5 * LOG2E
PAIR = 2 * HEAD_DIM
N_PAIRS_A = WIDTH_A // PAIR
N_PAIRS_B = WIDTH_B // PAIR
PAIRS_PER_KV = GROUP_B // 2
NEG_INF = float("-inf")
BIAS_PERIOD = 1024

V7X_VMEM_LIMIT_BYTES = 56 * 1024 * 1024

BF16 = jnp.bfloat16
F32 = jnp.float32

O_KA, O_VA, O_QB = WIDTH_A, 2 * WIDTH_A, 3 * WIDTH_A
O_KB = O_QB + WIDTH_B
O_VB = O_KB + KV_WIDTH_B


def _rms(x, g):
    return x * lax.rsqrt(jnp.mean(x * x, axis=-1, keepdims=True) + RMS_EPS) * g


def _dot(a, b):
    return jnp.dot(a, b, preferred_element_type=F32)


def _dot_nt(a, b):
    return lax.dot_general(a, b, (((1,), (1,)), ((), ())), preferred_element_type=F32)


def _resident(shape, layer=None):
    nd = len(shape)
    if layer is None:
        return pl.BlockSpec(shape, lambda *_: (0,) * nd, pipeline_mode=pl.Buffered(1))
    return pl.BlockSpec((None,) + tuple(shape), lambda *_: (layer,) + (0,) * nd,
                        pipeline_mode=pl.Buffered(1))


def _rope(x, cos, sdn, sup):
    half = ROT_DIM // 2
    return (x * cos + pltpu.roll(x, PAIR - half, axis=1) * sdn
            + pltpu.roll(x, half, axis=1) * sup)


def _low_lanes(shape):
    return lax.broadcasted_iota(jnp.int32, shape, 1) < HEAD_DIM


def _keep_half(x, half, fill):
    keep = _low_lanes(x.shape) if half == 0 else ~_low_lanes(x.shape)
    return jnp.where(keep, x, jnp.full_like(x, fill))


def _exp_scores(s_blocks, sink=None):
    m = functools.reduce(jnp.maximum, [jnp.max(s, axis=-1, keepdims=True) for s in s_blocks])
    if sink is not None:
        m = jnp.maximum(m, sink)
    p = jnp.concatenate([jnp.exp2(s - m).astype(BF16) for s in s_blocks], axis=-1)
    return p, (None if sink is None else jnp.exp2(sink - m))


def _pv_stacked(probs, v_ext, extras=None):
    m = probs[0].shape[0]
    x = _dot(jnp.concatenate(probs, axis=0), v_ext)
    outs = []
    for i in range(len(probs)):
        den = x[i * m:(i + 1) * m, PAIR:]
        if extras is not None:
            den = den + extras[i]
        outs.append(x[i * m:(i + 1) * m, :PAIR] / den)
    return outs


def _rolled_bias_rows(rbv_ref, h, rows):
    x = jnp.broadcast_to(rbv_ref[h:h + 1, :] * LOG2E, (rows, BIAS_PERIOD))
    return pltpu.roll(x, 0, axis=1, stride=1, stride_axis=0)


def _band_valid(rows, col0, cols, prev_chunks):
    r = lax.broadcasted_iota(jnp.int32, (rows, cols), 0) // CHUNK
    c = (lax.broadcasted_iota(jnp.int32, (rows, cols), 1) + col0) // CHUNK
    return (c >= r) & (c <= r + prev_chunks)


TQ = 256
NBLK_A = A_ROWS // TQ + 1
WIN_A = NBLK_A * TQ
WIN_B = WINDOW_B + TQ
NEG_KIND = NBLK_A
HALF_B = WINDOW_B
KEYS_B = HALF_B + WINDOW_B


def _mix_prompt_kernel(sinks_ref, x_ref, g_ref, w_ref, rope_ref, ct_ref, st_ref, rbv_ref,
                       goa_ref, gob_ref,
                       o_ref, cak_ref, cav_ref, cbk_ref, cbv_ref,
                       ka_scr, va_scr, kbp_scr, vbp_scr, bias_scr, maskb_scr):
    b, t = pl.program_id(0), pl.program_id(1)

    @pl.when((b == 0) & (t == 0))
    def _init():
        ka_scr[...] = jnp.zeros_like(ka_scr)
        kbp_scr[...] = jnp.zeros_like(kbp_scr)
        for v_scr in (va_scr, vbp_scr):
            v_scr[..., :PAIR] = jnp.zeros(v_scr.shape[:-1] + (PAIR,), BF16)
            v_scr[..., PAIR:] = jnp.ones(v_scr.shape[:-1] + (PAIR,), BF16)
        for h in range(N_HEADS_A):
            rows = _rolled_bias_rows(rbv_ref, h, TQ)
            for j in range(NBLK_A):
                valid = _band_valid(TQ, j * TQ, TQ, PREV_CHUNKS_A)
                bias_scr[j, h] = jnp.where(valid, rows[:, j * TQ:(j + 1) * TQ], NEG_INF)
            bias_scr[NEG_KIND, h] = jnp.full((TQ, TQ), NEG_INF, F32)
        band = jnp.where(_band_valid(HALF_B, 0, KEYS_B, PREV_CHUNKS_B),
                         jnp.zeros((HALF_B, KEYS_B), F32), NEG_INF)
        col = lax.broadcasted_iota(jnp.int32, (HALF_B, KEYS_B), 1)
        maskb_scr[0] = band
        maskb_scr[1] = jnp.where(col >= WINDOW_B, band, NEG_INF)

    n = _rms(x_ref[0], g_ref[...]).astype(BF16)
    slot = lax.rem(t, NBLK_A)
    row0 = pl.multiple_of(slot * TQ, TQ)
    ka = _dot(n, w_ref[:, O_KA:O_VA])
    for p in range(N_PAIRS_A):
        ka_scr[p, pl.ds(row0, TQ), :] = ka[:, p * PAIR:(p + 1) * PAIR].astype(BF16)
    za = _dot(n, w_ref[:, :O_KA])
    qa = [[_keep_half(za[:, j * PAIR:(j + 1) * PAIR] * QSCALE, half, 0.0).astype(BF16)
           for half in range(2)] for j in range(N_PAIRS_A)]
    va = _dot(n, w_ref[:, O_VA:O_QB])
    for p in range(N_PAIRS_A):
        va_scr[p, pl.ds(row0, TQ), :PAIR] = va[:, p * PAIR:(p + 1) * PAIR].astype(BF16)
    zb = _dot(n, w_ref[:, O_QB:])
    ct, st = ct_ref[pl.ds(t, 1), :], st_ref[pl.ds(t, 1), :]
    cos = ct * rope_ref[0] - st * rope_ref[1]
    sdn = st * rope_ref[2] + ct * rope_ref[3]
    sup = st * rope_ref[4] + ct * rope_ref[5]
    kb = _rope(zb[:, WIDTH_B:WIDTH_B + KV_WIDTH_B], cos, sdn, sup)
    vb = zb[:, WIDTH_B + KV_WIDTH_B:]
    qb = []
    for j in range(N_PAIRS_B):
        q = _rope(zb[:, j * PAIR:(j + 1) * PAIR], cos, sdn, sup) * QSCALE
        qb.append([_keep_half(q, half, 0.0).astype(BF16) for half in range(2)])

    cak_ref[0] = ka
    cav_ref[0] = va
    cbk_ref[0] = kb[TQ - WINDOW_B:, :]
    cbv_ref[0] = vb[TQ - WINDOW_B:, :]

    kinds = []
    for ps in range(NBLK_A):
        d = lax.rem(t - ps + NBLK_A, NBLK_A)
        kinds.append(jnp.where(t >= d, NBLK_A - 1 - d, NEG_KIND))
    low = _low_lanes((TQ, PAIR))
    oa = []
    for p in range(N_PAIRS_A):
        s = _dot_nt(jnp.concatenate(qa[p], axis=0), ka_scr[p])
        probs = []
        for half in range(2):
            s_blocks = [s[half * TQ:(half + 1) * TQ, ps * TQ:(ps + 1) * TQ]
                        + bias_scr[kinds[ps], 2 * p + half] for ps in range(NBLK_A)]
            probs.append(_exp_scores(s_blocks)[0])
        o = _pv_stacked(probs, va_scr[p])
        oa.append(jnp.where(low, o[0], o[1]))
    oa = jnp.concatenate(oa, axis=-1)

    kb_sw, vb_sw = pltpu.roll(kb, HEAD_DIM, axis=1), pltpu.roll(vb, HEAD_DIM, axis=1)
    ones = jnp.ones((TQ, PAIR), BF16)
    slot_b = lax.rem(t, 2)
    mask_first = maskb_scr[jnp.where(t >= 1, 0, 1)]
    ob_rows = [[[None] * (TQ // HALF_B) for _ in range(2)] for _ in range(N_PAIRS_B)]
    for c, (k_c, v_c) in enumerate(((kb, vb), (kb_sw, vb_sw))):
        k_c, v_c = k_c.astype(BF16), v_c.astype(BF16)
        kbp_scr[c, slot_b] = k_c[TQ - WINDOW_B:, :]
        vbp_scr[c, slot_b, :, :PAIR] = v_c[TQ - WINDOW_B:, :]
        k_win = jnp.concatenate([kbp_scr[c, 1 - slot_b], k_c], axis=0)
        v_win = jnp.concatenate([vbp_scr[c, 1 - slot_b], jnp.concatenate([v_c, ones], axis=1)], axis=0)
        heads = [(p, half) for p in range(N_PAIRS_B) for half in range(2)
                 if (half == p // PAIRS_PER_KV) == (c == 0)]
        for r in range(TQ // HALF_B):
            rows, keys = slice(r * HALF_B, (r + 1) * HALF_B), slice(r * HALF_B, r * HALF_B + KEYS_B)
            mask = mask_first if r == 0 else maskb_scr[0]
            s = _dot_nt(jnp.concatenate([qb[p][half][rows] for p, half in heads], axis=0), k_win[keys])
            probs, extras = [], []
            for i, (p, half) in enumerate(heads):
                pr, ex = _exp_scores([s[i * HALF_B:(i + 1) * HALF_B, :] + mask],
                                     sinks_ref[2 * p + half] * LOG2E)
                probs.append(pr)
                extras.append(ex)
            for (p, half), o in zip(heads, _pv_stacked(probs, v_win[keys], extras)):
                ob_rows[p][half][r] = o
    ob = jnp.concatenate([jnp.where(low, jnp.concatenate(o[0], axis=0), jnp.concatenate(o[1], axis=0))
                          for o in ob_rows], axis=-1)

    o_ref[0, :, :WIDTH_A] = _rms(oa, goa_ref[...]).astype(BF16)
    o_ref[0, :, WIDTH_A:] = _rms(ob, gob_ref[...]).astype(BF16)


def _mix_prompt(sinks, x, g, w_bf16, rope_rows, ct, st, rbv, goa, gob, *, layer):
    b, s, _ = x.shape
    nt = s // TQ
    cur = lambda bi, t: (bi, t, 0)
    keep_a = lambda bi, t: (bi, jnp.maximum(t - (nt - A_ROWS // TQ), 0), 0)
    seq = lambda bi, t: (bi, 0, 0)
    return pl.pallas_call(
        _mix_prompt_kernel,
        grid=(b, nt),
        in_specs=[
            pl.BlockSpec(memory_space=pltpu.SMEM),
            pl.BlockSpec((1, TQ, D_MODEL), cur),
            _resident((1, D_MODEL)),
            _resident((D_MODEL, IN_WIDTH), layer),
            _resident(rope_rows.shape),
            _resident(ct.shape),
            _resident(st.shape),
            _resident((N_HEADS_A, BIAS_PERIOD)),
            _resident((1, WIDTH_A)),
            _resident((1, WIDTH_B)),
        ],
        out_specs=[
            pl.BlockSpec((1, TQ, MIX_WIDTH), cur),
            pl.BlockSpec((1, TQ, WIDTH_A), keep_a),
            pl.BlockSpec((1, TQ, WIDTH_A), keep_a),
            pl.BlockSpec((1, WINDOW_B, KV_WIDTH_B), seq),
            pl.BlockSpec((1, WINDOW_B, KV_WIDTH_B), seq),
        ],
        out_shape=[
            jax.ShapeDtypeStruct((b, s, MIX_WIDTH), BF16),
            jax.ShapeDtypeStruct((b, A_ROWS, WIDTH_A), F32),
            jax.ShapeDtypeStruct((b, A_ROWS, WIDTH_A), F32),
            jax.ShapeDtypeStruct((b, WINDOW_B, KV_WIDTH_B), F32),
            jax.ShapeDtypeStruct((b, WINDOW_B, KV_WIDTH_B), F32),
        ],
        scratch_shapes=[
            pltpu.VMEM((N_PAIRS_A, WIN_A, PAIR), BF16),
            pltpu.VMEM((N_PAIRS_A, WIN_A, 2 * PAIR), BF16),
            pltpu.VMEM((2, 2, WINDOW_B, PAIR), BF16),
            pltpu.VMEM((2, 2, WINDOW_B, 2 * PAIR), BF16),
            pltpu.VMEM((NBLK_A + 1, N_HEADS_A, TQ, TQ), F32),
            pltpu.VMEM((2, HALF_B, KEYS_B), F32),
        ],
        compiler_params=pltpu.CompilerParams(
            dimension_semantics=("arbitrary", "arbitrary"),
            vmem_limit_bytes=V7X_VMEM_LIMIT_BYTES),
        name="mix_prompt",
    )(sinks, x, g, w_bf16, rope_rows, ct, st, rbv, goa, gob)


def _in_proj_kernel(x_ref, g_ref, w_ref, rope_ref, qa_ref, ka_ref, va_ref, qb_ref, kb_ref, vb_ref):
    n = _rms(x_ref[...], g_ref[...]).astype(BF16)
    z = _dot(n, w_ref[...])
    cos, sdn, sup = rope_ref[0], rope_ref[1], rope_ref[2]
    qa_ref[...] = (z[:, :O_KA] * QSCALE).astype(BF16)
    ka_ref[...] = z[:, O_KA:O_VA]
    va_ref[...] = z[:, O_VA:O_QB]
    for j in range(N_PAIRS_B):
        sl = slice(O_QB + j * PAIR, O_QB + (j + 1) * PAIR)
        qb_ref[:, j * PAIR:(j + 1) * PAIR] = (_rope(z[:, sl], cos, sdn, sup) * QSCALE).astype(BF16)
    kb_ref[...] = _rope(z[:, O_KB:O_VB], cos, sdn, sup)
    vb_ref[...] = z[:, O_VB:]


def _in_proj(x, g, w_bf16, rope_tabs, *, tm, layer):
    n_tok = x.shape[0]
    row = lambda i: (i, 0)
    widths = (WIDTH_A, WIDTH_A, WIDTH_A, WIDTH_B, KV_WIDTH_B, KV_WIDTH_B)
    dtypes = (BF16, F32, F32, BF16, F32, F32)
    return pl.pallas_call(
        _in_proj_kernel,
        grid=(n_tok // tm,),
        in_specs=[
            pl.BlockSpec((tm, D_MODEL), row),
            _resident((1, D_MODEL)),
            _resident((D_MODEL, IN_WIDTH), layer),
            _resident(rope_tabs.shape),
        ],
        out_specs=[pl.BlockSpec((tm, w), row) for w in widths],
        out_shape=[jax.ShapeDtypeStruct((n_tok, w), d) for w, d in zip(widths, dtypes)],
        compiler_params=pltpu.CompilerParams(
            dimension_semantics=("arbitrary",), vmem_limit_bytes=V7X_VMEM_LIMIT_BYTES),
        name="in_proj",
    )(x, g, w_bf16, rope_tabs)


N_SAMPLE_IN = 14
N_ROLL_IN = 8


def _attn_sample_kernel(*refs, n_prev, emit):
    (sinks_ref, qa_ref, kac_ref, kan_ref, vac_ref, van_ref, qb_ref, kbc_ref, kbn_ref, vbc_ref,
     vbn_ref, rbv_ref, goa_ref, gob_ref) = refs[:N_SAMPLE_IN]
    n_in = N_SAMPLE_IN + N_ROLL_IN * n_prev
    o_ref, bias_scr = refs[n_in], refs[-1]
    t_s = qa_ref.shape[0]
    la = kac_ref.shape[1]

    @pl.when(pl.program_id(0) == 0)
    def _init():
        for h in range(N_HEADS_A):
            bias_scr[h] = _rolled_bias_rows(rbv_ref, h, t_s)

    if emit:
        own = (kac_ref, kan_ref, vac_ref, van_ref, kbc_ref, kbn_ref, vbc_ref, vbn_ref)
        layers = [refs[N_SAMPLE_IN + N_ROLL_IN * l:N_SAMPLE_IN + N_ROLL_IN * (l + 1)]
                  for l in range(n_prev)] + [own]
        for l, lr in enumerate(layers):
            for j, new_ref in enumerate(refs[n_in + 1:n_in + 5]):
                cache_ref, fresh_ref = lr[2 * j], lr[2 * j + 1]
                kept, hd = new_ref.shape[2] - t_s, new_ref.shape[3:]
                new_ref[l, 0, :kept] = cache_ref[0, cache_ref.shape[1] - kept:, :].reshape((kept,) + hd)
                new_ref[l, 0, kept:] = fresh_ref[...].reshape((t_s,) + hd)

    low = _low_lanes((t_s, PAIR))
    ones_a = jnp.ones((la + t_s, PAIR), BF16)
    oa = []
    for p in range(N_PAIRS_A):
        sl = slice(p * PAIR, (p + 1) * PAIR)
        qp = qa_ref[:, sl]
        k = jnp.concatenate([kac_ref[0, :, sl], kan_ref[:, sl]], axis=0).astype(BF16)
        v = jnp.concatenate([vac_ref[0, :, sl], van_ref[:, sl]], axis=0).astype(BF16)
        s = _dot_nt(jnp.concatenate([_keep_half(qp, half, 0.0) for half in range(2)], axis=0), k)
        probs = [_exp_scores([s[half * t_s:(half + 1) * t_s, :] + bias_scr[2 * p + half, :, :la + t_s]])[0]
                 for half in range(2)]
        o = _pv_stacked(probs, jnp.concatenate([v, ones_a], axis=1))
        oa.append(jnp.where(low, o[0], o[1]))
    oa = jnp.concatenate(oa, axis=-1)

    kb = jnp.concatenate([kbc_ref[0], kbn_ref[...]], axis=0)
    vb = jnp.concatenate([vbc_ref[0], vbn_ref[...]], axis=0)
    ones_b = jnp.ones((kb.shape[0], PAIR), BF16)
    ob_half = [[None, None] for _ in range(N_PAIRS_B)]
    for c in range(2):
        k_c = (kb if c == 0 else pltpu.roll(kb, HEAD_DIM, axis=1)).astype(BF16)
        v_c = (vb if c == 0 else pltpu.roll(vb, HEAD_DIM, axis=1)).astype(BF16)
        heads = [(p, half) for p in range(N_PAIRS_B) for half in range(2)
                 if (half == p // PAIRS_PER_KV) == (c == 0)]
        s = _dot_nt(jnp.concatenate([_keep_half(qb_ref[:, p * PAIR:(p + 1) * PAIR], half, 0.0)
                                     for p, half in heads], axis=0), k_c)
        probs, extras = [], []
        for i, (p, half) in enumerate(heads):
            pr, ex = _exp_scores([s[i * t_s:(i + 1) * t_s, :]], sinks_ref[2 * p + half] * LOG2E)
            probs.append(pr)
            extras.append(ex)
        for (p, half), o in zip(heads, _pv_stacked(probs, jnp.concatenate([v_c, ones_b], axis=1), extras)):
            ob_half[p][half] = o
    ob = jnp.concatenate([jnp.where(low, o[0], o[1]) for o in ob_half], axis=-1)
    o_ref[:, :WIDTH_A] = _rms(oa, goa_ref[...]).astype(BF16)
    o_ref[:, WIDTH_A:] = _rms(ob, gob_ref[...]).astype(BF16)


def _attn_sample(sinks, qa, ka_cache, ka, va_cache, va, qb, kb_cache, kb, vb_cache, vb,
                 rbv, goa, gob, *, t_s, keep_a, keep_b, layer, prev_new, emit):
    n_tok = qa.shape[0]
    n_seq = n_tok // t_s
    la, lb = ka_cache.shape[2], kb_cache.shape[2]
    assert t_s <= keep_a <= la + t_s and t_s <= keep_b <= lb + t_s
    row = lambda i: (i, 0)
    cache = lambda l, rows, width: pl.BlockSpec((None, 1, rows, width), lambda i: (l, i, 0, 0))
    new = lambda width: pl.BlockSpec((t_s, width), row)

    def kv_specs(l):
        return [cache(l, la, WIDTH_A), new(WIDTH_A), cache(l, la, WIDTH_A), new(WIDTH_A),
                cache(l, lb, KV_WIDTH_B), new(KV_WIDTH_B), cache(l, lb, KV_WIDTH_B), new(KV_WIDTH_B)]

    own = kv_specs(layer)
    in_specs = ([pl.BlockSpec(memory_space=pltpu.SMEM), new(WIDTH_A)] + own[:4] + [new(WIDTH_B)] + own[4:]
                + [_resident((N_HEADS_A, BIAS_PERIOD)), _resident((1, WIDTH_A)), _resident((1, WIDTH_B))])
    operands = [sinks, qa, ka_cache, ka, va_cache, va, qb, kb_cache, kb, vb_cache, vb, rbv, goa, gob]
    out_specs = [pl.BlockSpec((t_s, MIX_WIDTH), row)]
    out_shape = [jax.ShapeDtypeStruct((n_tok, MIX_WIDTH), BF16)]
    n_prev = 0
    if emit:
        n_prev = len(prev_new)
        for l, (ka_l, va_l, kb_l, vb_l) in enumerate(prev_new):
            in_specs += kv_specs(l)
            operands += [ka_cache, ka_l, va_cache, va_l, kb_cache, kb_l, vb_cache, vb_l]
        n_layers = n_prev + 1
        for keep, width in ((keep_a, WIDTH_A), (keep_a, WIDTH_A), (keep_b, KV_WIDTH_B), (keep_b, KV_WIDTH_B)):
            shape = (keep, width // HEAD_DIM, HEAD_DIM)
            out_specs.append(pl.BlockSpec((n_layers, 1) + shape, lambda i: (0, i, 0, 0, 0)))
            out_shape.append(jax.ShapeDtypeStruct((n_layers, n_seq) + shape, F32))
    return pl.pallas_call(
        functools.partial(_attn_sample_kernel, n_prev=n_prev, emit=emit),
        grid=(n_seq,),
        in_specs=in_specs,
        out_specs=out_specs,
        out_shape=out_shape,
        scratch_shapes=[pltpu.VMEM((N_HEADS_A, t_s, BIAS_PERIOD), F32)],
        compiler_params=pltpu.CompilerParams(
            dimension_semantics=("arbitrary",), vmem_limit_bytes=V7X_VMEM_LIMIT_BYTES),
        name="attn_sample",
    )(*operands)


TAIL_SUB_ROWS = 256


def _tail_kernel(h_ref, o_ref, p_ref, wout_ref, gffn_ref, wgu_ref, wdown_ref, wgate_ref,
                 wproj_ref, gfin_ref, out_ref, *, final, n_sub):
    rows = h_ref.shape[0] // n_sub
    for first in range(0, n_sub, 2):
        subs = [pl.ds(i * rows, rows) for i in range(first, first + 2)]
        h = [h_ref[sl, :] + _dot(o_ref[sl, :], wout_ref[...]) for sl in subs]
        pp = [_dot(p_ref[sl, :].astype(BF16), wproj_ref[...]) for sl in subs]
        gu = [_dot(_rms(x, gffn_ref[...]).astype(BF16), wgu_ref[...]) for x in h]
        act = [(jax.nn.silu(x[:, :D_FF]) * x[:, D_FF:]).astype(BF16) for x in gu]
        h = [x + _dot(a, wdown_ref[...]) for x, a in zip(h, act)]
        gate = [jax.nn.sigmoid(_dot(x.astype(BF16), wgate_ref[...])) for x in h]
        for sl, x, g, e in zip(subs, h, gate, pp):
            x = x + g * e
            if final:
                x = _rms(x, gfin_ref[...])
            out_ref[sl, :] = x


def _tail(h, o, p, wout, gffn, wgu, wdown, wgate, wproj, gfin, *, tm, final, layer):
    n_tok = h.shape[0]
    row = lambda i: (i, 0)
    return pl.pallas_call(
        functools.partial(_tail_kernel, final=final, n_sub=tm // TAIL_SUB_ROWS),
        grid=(n_tok // tm,),
        in_specs=[
            pl.BlockSpec((tm, D_MODEL), row),
            pl.BlockSpec((tm, MIX_WIDTH), row),
            pl.BlockSpec((None, tm, D_PLE), lambda i: (layer, i, 0)),
            _resident((MIX_WIDTH, D_MODEL), layer),
            _resident((1, D_MODEL)),
            _resident((D_MODEL, 2 * D_FF), layer),
            _resident((D_FF, D_MODEL), layer),
            _resident((D_MODEL, D_MODEL), layer),
            _resident((D_PLE, D_MODEL), layer),
            _resident((1, D_MODEL)),
        ],
        out_specs=pl.BlockSpec((tm, D_MODEL), row),
        out_shape=jax.ShapeDtypeStruct((n_tok, D_MODEL), F32),
        compiler_params=pltpu.CompilerParams(
            dimension_semantics=("arbitrary",), vmem_limit_bytes=V7X_VMEM_LIMIT_BYTES),
        name="layer_tail",
    )(h, o, p, wout, gffn, wgu, wdown, wgate, wproj, gfin)


CAST_ROWS = 256


def _cast_kernel(w_ref, o_ref):
    o_ref[...] = w_ref[...].astype(o_ref.dtype)


def _to_bf16(w):
    depth, rows, cols = w.shape
    blk = pl.BlockSpec((1, CAST_ROWS, cols), lambda d, r: (d, r, 0))
    return pl.pallas_call(
        _cast_kernel,
        grid=(depth, rows // CAST_ROWS),
        in_specs=[blk],
        out_specs=blk,
        out_shape=jax.ShapeDtypeStruct(w.shape, BF16),
        compiler_params=pltpu.CompilerParams(dimension_semantics=("arbitrary", "arbitrary")),
        name="cast_bf16",
    )(w)


def _rope_lane_tables(pos):
    half = ROT_DIM // 2
    d = np.arange(PAIR) % HEAD_DIM
    inv = ROPE_THETA ** (-(2.0 * (d % half)) / ROT_DIM)
    ang = np.where(d < ROT_DIM, np.asarray(pos, np.float64)[:, None] * inv[None, :], 0.0)
    lower = (d < half).astype(np.float64)[None, :]
    upper = ((d >= half) & (d < ROT_DIM)).astype(np.float64)[None, :]
    return np.cos(ang), np.sin(ang), lower, upper


def _rope_tables_direct(pos):
    cos, sin, lower, upper = _rope_lane_tables(pos)
    return jnp.asarray(np.stack([cos, -sin * lower, sin * upper]), F32)


def _rope_tables_split(n_tiles, tile):
    cr, sr, lower, upper = _rope_lane_tables(np.arange(tile))
    ct, st, _, _ = _rope_lane_tables(np.arange(n_tiles) * tile)
    rows = np.stack([cr, sr, -cr * lower, -sr * lower, cr * upper, sr * upper])
    return jnp.asarray(rows, F32), jnp.asarray(ct, F32), jnp.asarray(st, F32)


def _rel_bias_row(rel_bias):
    u = np.arange(BIAS_PERIOD)
    diff = np.where(u < BIAS_PERIOD // 2 + A_ROWS // 2, u, u - BIAS_PERIOD)
    idx = np.clip(A_ROWS - diff, -REL_CLIP, REL_CLIP) + REL_CLIP
    return rel_bias.astype(F32)[:, idx]


def kernel(x_prompt, x_sample, p_prompt, p_sample, cache_a_k, cache_a_v, cache_b_k, cache_b_v,
           g_mix_norm, w_in, rel_bias_a, sinks_b, g_out_a, g_out_b, w_out, g_ffn_norm,
           w_gate_up, w_down, w_ple_proj, w_ple_gate, g_final):
    b_p, s_p, _ = x_prompt.shape
    b_s, t_s, _ = x_sample.shape
    depth = w_in.shape[0]
    la_c, lb_c = cache_a_k.shape[2], cache_b_k.shape[2]
    keep_a_s = min(A_ROWS, la_c + t_s)
    keep_b_s = min(WINDOW_B, lb_c + t_s)
    tm_s = 512
    tm_tail = 512
    assert s_p % TQ == 0 and s_p >= A_ROWS and (b_s * t_s) % tm_s == 0 and tm_s % t_s == 0
    assert la_c == A_ROWS and lb_c == WINDOW_B and t_s == CHUNK

    rope_rows, rope_ct, rope_st = _rope_tables_split(s_p // TQ, TQ)
    rope_s = _rope_tables_direct(PAST_LEN + np.arange(tm_s) % t_s)
    row2 = lambda a: a.reshape(1, -1).astype(F32)
    g_fin = row2(g_final)

    hp = x_prompt
    hs = x_sample.reshape(b_s * t_s, D_MODEL)
    outs = [[] for _ in range(4)]
    new_rows = []
    w_in_b, w_out_b, w_gu_b, w_down_b, w_gate_b, w_proj_b = (
        _to_bf16(w) for w in (w_in, w_out, w_gate_up, w_down, w_ple_gate, w_ple_proj))
    p_prompt_f = p_prompt.reshape(depth, b_p * s_p, D_PLE)
    p_sample_f = p_sample.reshape(depth, b_s * t_s, D_PLE)
    cak_s = cache_a_k.reshape(depth, b_s, la_c, WIDTH_A)
    cav_s = cache_a_v.reshape(depth, b_s, la_c, WIDTH_A)
    cbk_s = cache_b_k.reshape(depth, b_s, lb_c, KV_WIDTH_B)
    cbv_s = cache_b_v.reshape(depth, b_s, lb_c, KV_WIDTH_B)
    for i in range(depth):
        tail_w = (w_out_b, row2(g_ffn_norm[i]), w_gu_b, w_down_b, w_gate_b, w_proj_b, g_fin)
        g_mix = row2(g_mix_norm[i])
        goa, gob = row2(g_out_a[i]), row2(g_out_b[i])
        sinks = sinks_b[i].astype(F32)
        rbv = _rel_bias_row(rel_bias_a[i])
        final = i == depth - 1

        o, cak, cav, cbk, cbv = _mix_prompt(sinks, hp, g_mix, w_in_b, rope_rows, rope_ct, rope_st,
                                            rbv, goa, gob, layer=i)
        hp = _tail(hp.reshape(b_p * s_p, D_MODEL), o.reshape(b_p * s_p, MIX_WIDTH),
                   p_prompt_f, *tail_w, tm=tm_tail, final=final, layer=i)
        hp = hp.reshape(b_p, s_p, D_MODEL)
        outs[0].append(cak.reshape(b_p, A_ROWS, N_HEADS_A, HEAD_DIM))
        outs[1].append(cav.reshape(b_p, A_ROWS, N_HEADS_A, HEAD_DIM))
        outs[2].append(cbk.reshape(b_p, WINDOW_B, N_KV_B, HEAD_DIM))
        outs[3].append(cbv.reshape(b_p, WINDOW_B, N_KV_B, HEAD_DIM))

        qa, ka, va, qb, kb, vb = _in_proj(hs, g_mix, w_in_b, rope_s, tm=tm_s, layer=i)
        o, *rolled = _attn_sample(sinks, qa, cak_s, ka, cav_s, va, qb, cbk_s, kb, cbv_s, vb,
                                  rbv, goa, gob, t_s=t_s, keep_a=keep_a_s, keep_b=keep_b_s,
                                  layer=i, prev_new=new_rows, emit=final)
        new_rows.append((ka, va, kb, vb))
        hs = _tail(hs, o, p_sample_f, *tail_w, tm=tm_tail, final=final, layer=i)

    y_sample = hs.reshape(b_s, t_s, D_MODEL)
    return (hp, y_sample) + tuple(jnp.stack(o) for o in outs) + tuple(rolled)
```

```python
import functools
import math

import numpy as np
import jax
import jax.numpy as jnp
from jax import lax
from jax.experimental import pallas as pl
from jax.experimental.pallas import tpu as pltpu

D_MODEL = 1024
CHUNK = 64
HEAD_DIM = 64
N_HEADS_A = 8
N_HEADS_B = 8
N_KV_B = 2
GROUP_B = N_HEADS_B // N_KV_B
WIDTH_A = N_HEADS_A * HEAD_DIM
WIDTH_B = N_HEADS_B * HEAD_DIM
KV_WIDTH_B = N_KV_B * HEAD_DIM
MIX_WIDTH = WIDTH_A + WIDTH_B
IN_WIDTH = 3 * WIDTH_A + WIDTH_B + 2 * KV_WIDTH_B
PREV_CHUNKS_A = 8
A_ROWS = PREV_CHUNKS_A * CHUNK
REL_CLIP = 256
WINDOW_B = 128
PREV_CHUNKS_B = WINDOW_B // CHUNK
ROT_DIM = HEAD_DIM // 4
ROPE_THETA = 500000.0
D_FF = 2816
D_PLE = 256
PAST_LEN = 2048
RMS_EPS = 1e-6
LOG2E = math.log2(math.e)
QSCALE = HEAD_DIM ** -0.5 * LOG2E
PAIR = 2 * HEAD_DIM
N_PAIRS_A = WIDTH_A // PAIR
N_PAIRS_B = WIDTH_B // PAIR
PAIRS_PER_KV = GROUP_B // 2
NEG_INF = float("-inf")
BIAS_PERIOD = 1024

V7X_VMEM_LIMIT_BYTES = 56 * 1024 * 1024

BF16 = jnp.bfloat16
F32 = jnp.float32

O_KA, O_VA, O_QB = WIDTH_A, 2 * WIDTH_A, 3 * WIDTH_A
O_KB = O_QB + WIDTH_B
O_VB = O_KB + KV_WIDTH_B


def _rms(x, g):
    return x * lax.rsqrt(jnp.mean(x * x, axis=-1, keepdims=True) + RMS_EPS) * g


def _dot(a, b):
    return jnp.dot(a, b, preferred_element_type=F32)


def _dot_nt(a, b):
    return lax.dot_general(a, b, (((1,), (1,)), ((), ())), preferred_element_type=F32)


def _resident(shape, layer=None):
    nd = len(shape)
    if layer is None:
        return pl.BlockSpec(shape, lambda *_: (0,) * nd, pipeline_mode=pl.Buffered(1))
    return pl.BlockSpec((None,) + tuple(shape), lambda *_: (layer,) + (0,) * nd,
                        pipeline_mode=pl.Buffered(1))


def _rope(x, cos, sdn, sup):
    half = ROT_DIM // 2
    return (x * cos + pltpu.roll(x, PAIR - half, axis=1) * sdn
            + pltpu.roll(x, half, axis=1) * sup)


def _low_lanes(shape):
    return lax.broadcasted_iota(jnp.int32, shape, 1) < HEAD_DIM


def _keep_half(x, half, fill):
    keep = _low_lanes(x.shape) if half == 0 else ~_low_lanes(x.shape)
    return jnp.where(keep, x, jnp.full_like(x, fill))


def _exp_scores(s_blocks, sink=None):
    m = functools.reduce(jnp.maximum, [jnp.max(s, axis=-1, keepdims=True) for s in s_blocks])
    if sink is not None:
        m = jnp.maximum(m, sink)
    p = jnp.concatenate([jnp.exp2(s - m).astype(BF16) for s in s_blocks], axis=-1)
    return p, (None if sink is None else jnp.exp2(sink - m))


def _pv_stacked(probs, v_ext, extras=None):
    m = probs[0].shape[0]
    x = _dot(jnp.concatenate(probs, axis=0), v_ext)
    outs = []
    for i in range(len(probs)):
        den = x[i * m:(i + 1) * m, PAIR:]
        if extras is not None:
            den = den + extras[i]
        outs.append(x[i * m:(i + 1) * m, :PAIR] / den)
    return outs


def _rolled_bias_rows(rbv_ref, h, rows):
    x = jnp.broadcast_to(rbv_ref[h:h + 1, :] * LOG2E, (rows, BIAS_PERIOD))
    return pltpu.roll(x, 0, axis=1, stride=1, stride_axis=0)


def _band_valid(rows, col0, cols, prev_chunks):
    r = lax.broadcasted_iota(jnp.int32, (rows, cols), 0) // CHUNK
    c = (lax.broadcasted_iota(jnp.int32, (rows, cols), 1) + col0) // CHUNK
    return (c >= r) & (c <= r + prev_chunks)


TQ = 256
NBLK_A = A_ROWS // TQ + 1
WIN_A = NBLK_A * TQ
WIN_B = WINDOW_B + TQ
NEG_KIND = NBLK_A
HALF_B = WINDOW_B
KEYS_B = HALF_B + WINDOW_B


def _mix_prompt_kernel(sinks_ref, x_ref, g_ref, w_ref, rope_ref, ct_ref, st_ref, rbv_ref,
                       goa_ref, gob_ref,
                       o_ref, cak_ref, cav_ref, cbk_ref, cbv_ref,
                       ka_scr, va_scr, kbp_scr, vbp_scr, bias_scr, maskb_scr):
    b, t = pl.program_id(0), pl.program_id(1)

    @pl.when((b == 0) & (t == 0))
    def _init():
        ka_scr[...] = jnp.zeros_like(ka_scr)
        kbp_scr[...] = jnp.zeros_like(kbp_scr)
        for v_scr in (va_scr, vbp_scr):
            v_scr[..., :PAIR] = jnp.zeros(v_scr.shape[:-1] + (PAIR,), BF16)
            v_scr[..., PAIR:] = jnp.ones(v_scr.shape[:-1] + (PAIR,), BF16)
        for h in range(N_HEADS_A):
            rows = _rolled_bias_rows(rbv_ref, h, TQ)
            for j in range(NBLK_A):
                valid = _band_valid(TQ, j * TQ, TQ, PREV_CHUNKS_A)
                bias_scr[j, h] = jnp.where(valid, rows[:, j * TQ:(j + 1) * TQ], NEG_INF)
            bias_scr[NEG_KIND, h] = jnp.full((TQ, TQ), NEG_INF, F32)
        band = jnp.where(_band_valid(HALF_B, 0, KEYS_B, PREV_CHUNKS_B),
                         jnp.zeros((HALF_B, KEYS_B), F32), NEG_INF)
        col = lax.broadcasted_iota(jnp.int32, (HALF_B, KEYS_B), 1)
        maskb_scr[0] = band
        maskb_scr[1] = jnp.where(col >= WINDOW_B, band, NEG_INF)

    n = _rms(x_ref[0], g_ref[...]).astype(BF16)
    slot = lax.rem(t, NBLK_A)
    row0 = pl.multiple_of(slot * TQ, TQ)
    ka = _dot(n, w_ref[:, O_KA:O_VA])
    for p in range(N_PAIRS_A):
        ka_scr[p, pl.ds(row0, TQ), :] = ka[:, p * PAIR:(p + 1) * PAIR].astype(BF16)
    za = _dot(n, w_ref[:, :O_KA])
    qa = [[_keep_half(za[:, j * PAIR:(j + 1) * PAIR] * QSCALE, half, 0.0).astype(BF16)
           for half in range(2)] for j in range(N_PAIRS_A)]
    va = _dot(n, w_ref[:, O_VA:O_QB])
    for p in range(N_PAIRS_A):
        va_scr[p, pl.ds(row0, TQ), :PAIR] = va[:, p * PAIR:(p + 1) * PAIR].astype(BF16)
    zb = _dot(n, w_ref[:, O_QB:])
    ct, st = ct_ref[pl.ds(t, 1), :], st_ref[pl.ds(t, 1), :]
    cos = ct * rope_ref[0] - st * rope_ref[1]
    sdn = st * rope_ref[2] + ct * rope_ref[3]
    sup = st * rope_ref[4] + ct * rope_ref[5]
    kb = _rope(zb[:, WIDTH_B:WIDTH_B + KV_WIDTH_B], cos, sdn, sup)
    vb = zb[:, WIDTH_B + KV_WIDTH_B:]
    qb = []
    for j in range(N_PAIRS_B):
        q = _rope(zb[:, j * PAIR:(j + 1) * PAIR], cos, sdn, sup) * QSCALE
        qb.append([_keep_half(q, half, 0.0).astype(BF16) for half in range(2)])

    cak_ref[0] = ka
    cav_ref[0] = va
    cbk_ref[0] = kb[TQ - WINDOW_B:, :]
    cbv_ref[0] = vb[TQ - WINDOW_B:, :]

    kinds = []
    for ps in range(NBLK_A):
        d = lax.rem(t - ps + NBLK_A, NBLK_A)
        kinds.append(jnp.where(t >= d, NBLK_A - 1 - d, NEG_KIND))
    low = _low_lanes((TQ, PAIR))
    oa = []
    for p in range(N_PAIRS_A):
        s = _dot_nt(jnp.concatenate(qa[p], axis=0), ka_scr[p])
        probs = []
        for half in range(2):
            s_blocks = [s[half * TQ:(half + 1) * TQ, ps * TQ:(ps + 1) * TQ]
                        + bias_scr[kinds[ps], 2 * p + half] for ps in range(NBLK_A)]
            probs.append(_exp_scores(s_blocks)[0])
        o = _pv_stacked(probs, va_scr[p])
        oa.append(jnp.where(low, o[0], o[1]))
    oa = jnp.concatenate(oa, axis=-1)

    kb_sw, vb_sw = pltpu.roll(kb, HEAD_DIM, axis=1), pltpu.roll(vb, HEAD_DIM, axis=1)
    ones = jnp.ones((TQ, PAIR), BF16)
    slot_b = lax.rem(t, 2)
    mask_first = maskb_scr[jnp.where(t >= 1, 0, 1)]
    ob_rows = [[[None] * (TQ // HALF_B) for _ in range(2)] for _ in range(N_PAIRS_B)]
    for c, (k_c, v_c) in enumerate(((kb, vb), (kb_sw, vb_sw))):
        k_c, v_c = k_c.astype(BF16), v_c.astype(BF16)
        kbp_scr[c, slot_b] = k_c[TQ - WINDOW_B:, :]
        vbp_scr[c, slot_b, :, :PAIR] = v_c[TQ - WINDOW_B:, :]
        k_win = jnp.concatenate([kbp_scr[c, 1 - slot_b], k_c], axis=0)
        v_win = jnp.concatenate([vbp_scr[c, 1 - slot_b], jnp.concatenate([v_c, ones], axis=1)], axis=0)
        heads = [(p, half) for p in range(N_PAIRS_B) for half in range(2)
                 if (half == p // PAIRS_PER_KV) == (c == 0)]
        for r in range(TQ // HALF_B):
            rows, keys = slice(r * HALF_B, (r + 1) * HALF_B), slice(r * HALF_B, r * HALF_B + KEYS_B)
            mask = mask_first if r == 0 else maskb_scr[0]
            s = _dot_nt(jnp.concatenate([qb[p][half][rows] for p, half in heads], axis=0), k_win[keys])
            probs, extras = [], []
            for i, (p, half) in enumerate(heads):
                pr, ex = _exp_scores([s[i * HALF_B:(i + 1) * HALF_B, :] + mask],
                                     sinks_ref[2 * p + half] * LOG2E)
                probs.append(pr)
                extras.append(ex)
            for (p, half), o in zip(heads, _pv_stacked(probs, v_win[keys], extras)):
                ob_rows[p][half][r] = o
    ob = jnp.concatenate([jnp.where(low, jnp.concatenate(o[0], axis=0), jnp.concatenate(o[1], axis=0))
                          for o in ob_rows], axis=-1)

    o_ref[0, :, :WIDTH_A] = _rms(oa, goa_ref[...]).astype(BF16)
    o_ref[0, :, WIDTH_A:] = _rms(ob, gob_ref[...]).astype(BF16)


def _mix_prompt(sinks, x, g, w_bf16, rope_rows, ct, st, rbv, goa, gob, *, layer):
    b, s, _ = x.shape
    nt = s // TQ
    cur = lambda bi, t: (bi, t, 0)
    keep_a = lambda bi, t: (bi, jnp.maximum(t - (nt - A_ROWS // TQ), 0), 0)
    seq = lambda bi, t: (bi, 0, 0)
    return pl.pallas_call(
        _mix_prompt_kernel,
        grid=(b, nt),
        in_specs=[
            pl.BlockSpec(memory_space=pltpu.SMEM),
            pl.BlockSpec((1, TQ, D_MODEL), cur),
            _resident((1, D_MODEL)),
            _resident((D_MODEL, IN_WIDTH), layer),
            _resident(rope_rows.shape),
            _resident(ct.shape),
            _resident(st.shape),
            _resident((N_HEADS_A, BIAS_PERIOD)),
            _resident((1, WIDTH_A)),
            _resident((1, WIDTH_B)),
        ],
        out_specs=[
            pl.BlockSpec((1, TQ, MIX_WIDTH), cur),
            pl.BlockSpec((1, TQ, WIDTH_A), keep_a),
            pl.BlockSpec((1, TQ, WIDTH_A), keep_a),
            pl.BlockSpec((1, WINDOW_B, KV_WIDTH_B), seq),
            pl.BlockSpec((1, WINDOW_B, KV_WIDTH_B), seq),
        ],
        out_shape=[
            jax.ShapeDtypeStruct((b, s, MIX_WIDTH), BF16),
            jax.ShapeDtypeStruct((b, A_ROWS, WIDTH_A), F32),
            jax.ShapeDtypeStruct((b, A_ROWS, WIDTH_A), F32),
            jax.ShapeDtypeStruct((b, WINDOW_B, KV_WIDTH_B), F32),
            jax.ShapeDtypeStruct((b, WINDOW_B, KV_WIDTH_B), F32),
        ],
        scratch_shapes=[
            pltpu.VMEM((N_PAIRS_A, WIN_A, PAIR), BF16),
            pltpu.VMEM((N_PAIRS_A, WIN_A, 2 * PAIR), BF16),
            pltpu.VMEM((2, 2, WINDOW_B, PAIR), BF16),
            pltpu.VMEM((2, 2, WINDOW_B, 2 * PAIR), BF16),
            pltpu.VMEM((NBLK_A + 1, N_HEADS_A, TQ, TQ), F32),
            pltpu.VMEM((2, HALF_B, KEYS_B), F32),
        ],
        compiler_params=pltpu.CompilerParams(
            dimension_semantics=("arbitrary", "arbitrary"),
            vmem_limit_bytes=V7X_VMEM_LIMIT_BYTES),
        name="mix_prompt",
    )(sinks, x, g, w_bf16, rope_rows, ct, st, rbv, goa, gob)


def _in_proj_kernel(x_ref, g_ref, w_ref, rope_ref, qa_ref, ka_ref, va_ref, qb_ref, kb_ref, vb_ref):
    n = _rms(x_ref[...], g_ref[...]).astype(BF16)
    z = _dot(n, w_ref[...])
    cos, sdn, sup = rope_ref[0], rope_ref[1], rope_ref[2]
    qa_ref[...] = (z[:, :O_KA] * QSCALE).astype(BF16)
    ka_ref[...] = z[:, O_KA:O_VA]
    va_ref[...] = z[:, O_VA:O_QB]
    for j in range(N_PAIRS_B):
        sl = slice(O_QB + j * PAIR, O_QB + (j + 1) * PAIR)
        qb_ref[:, j * PAIR:(j + 1) * PAIR] = (_rope(z[:, sl], cos, sdn, sup) * QSCALE).astype(BF16)
    kb_ref[...] = _rope(z[:, O_KB:O_VB], cos, sdn, sup)
    vb_ref[...] = z[:, O_VB:]


def _in_proj(x, g, w_bf16, rope_tabs, *, tm, layer):
    n_tok = x.shape[0]
    row = lambda i: (i, 0)
    widths = (WIDTH_A, WIDTH_A, WIDTH_A, WIDTH_B, KV_WIDTH_B, KV_WIDTH_B)
    dtypes = (BF16, F32, F32, BF16, F32, F32)
    return pl.pallas_call(
        _in_proj_kernel,
        grid=(n_tok // tm,),
        in_specs=[
            pl.BlockSpec((tm, D_MODEL), row),
            _resident((1, D_MODEL)),
            _resident((D_MODEL, IN_WIDTH), layer),
            _resident(rope_tabs.shape),
        ],
        out_specs=[pl.BlockSpec((tm, w), row) for w in widths],
        out_shape=[jax.ShapeDtypeStruct((n_tok, w), d) for w, d in zip(widths, dtypes)],
        compiler_params=pltpu.CompilerParams(
            dimension_semantics=("arbitrary",), vmem_limit_bytes=V7X_VMEM_LIMIT_BYTES),
        name="in_proj",
    )(x, g, w_bf16, rope_tabs)


N_SAMPLE_IN = 14
N_ROLL_IN = 8


def _attn_sample_kernel(*refs, n_prev, emit):
    (sinks_ref, qa_ref, kac_ref, kan_ref, vac_ref, van_ref, qb_ref, kbc_ref, kbn_ref, vbc_ref,
     vbn_ref, rbv_ref, goa_ref, gob_ref) = refs[:N_SAMPLE_IN]
    n_in = N_SAMPLE_IN + N_ROLL_IN * n_prev
    o_ref, bias_scr = refs[n_in], refs[-1]
    t_s = qa_ref.shape[0]
    la = kac_ref.shape[1]

    @pl.when(pl.program_id(0) == 0)
    def _init():
        for h in range(N_HEADS_A):
            bias_scr[h] = _rolled_bias_rows(rbv_ref, h, t_s)

    if emit:
        own = (kac_ref, kan_ref, vac_ref, van_ref, kbc_ref, kbn_ref, vbc_ref, vbn_ref)
        layers = [refs[N_SAMPLE_IN + N_ROLL_IN * l:N_SAMPLE_IN + N_ROLL_IN * (l + 1)]
                  for l in range(n_prev)] + [own]
        for l, lr in enumerate(layers):
            for j, new_ref in enumerate(refs[n_in + 1:n_in + 5]):
                cache_ref, fresh_ref = lr[2 * j], lr[2 * j + 1]
                kept, hd = new_ref.shape[2] - t_s, new_ref.shape[3:]
                new_ref[l, 0, :kept] = cache_ref[0, cache_ref.shape[1] - kept:]
                new_ref[l, 0, kept:] = fresh_ref[...].reshape((t_s,) + hd)

    kac2, vac2 = kac_ref[0].reshape(la, WIDTH_A), vac_ref[0].reshape(la, WIDTH_A)
    low = _low_lanes((t_s, PAIR))
    ones_a = jnp.ones((la + t_s, PAIR), BF16)
    oa = []
    for p in range(N_PAIRS_A):
        sl = slice(p * PAIR, (p + 1) * PAIR)
        qp = qa_ref[:, sl]
        k = jnp.concatenate([kac2[:, sl], kan_ref[:, sl]], axis=0).astype(BF16)
        v = jnp.concatenate([vac2[:, sl], van_ref[:, sl]], axis=0).astype(BF16)
        s = _dot_nt(jnp.concatenate([_keep_half(qp, half, 0.0) for half in range(2)], axis=0), k)
        probs = [_exp_scores([s[half * t_s:(half + 1) * t_s, :] + bias_scr[2 * p + half, :, :la + t_s]])[0]
                 for half in range(2)]
        o = _pv_stacked(probs, jnp.concatenate([v, ones_a], axis=1))
        oa.append(jnp.where(low, o[0], o[1]))
    oa = jnp.concatenate(oa, axis=-1)

    lb = kbc_ref.shape[1]
    kb = jnp.concatenate([kbc_ref[0].reshape(lb, KV_WIDTH_B), kbn_ref[...]], axis=0)
    vb = jnp.concatenate([vbc_ref[0].reshape(lb, KV_WIDTH_B), vbn_ref[...]], axis=0)
    ones_b = jnp.ones((kb.shape[0], PAIR), BF16)
    ob_half = [[None, None] for _ in range(N_PAIRS_B)]
    for c in range(2):
        k_c = (kb if c == 0 else pltpu.roll(kb, HEAD_DIM, axis=1)).astype(BF16)
        v_c = (vb if c == 0 else pltpu.roll(vb, HEAD_DIM, axis=1)).astype(BF16)
        heads = [(p, half) for p in range(N_PAIRS_B) for half in range(2)
                 if (half == p // PAIRS_PER_KV) == (c == 0)]
        s = _dot_nt(jnp.concatenate([_keep_half(qb_ref[:, p * PAIR:(p + 1) * PAIR], half, 0.0)
                                     for p, half in heads], axis=0), k_c)
        probs, extras = [], []
        for i, (p, half) in enumerate(heads):
            pr, ex = _exp_scores([s[i * t_s:(i + 1) * t_s, :]], sinks_ref[2 * p + half] * LOG2E)
            probs.append(pr)
            extras.append(ex)
        for (p, half), o in zip(heads, _pv_stacked(probs, jnp.concatenate([v_c, ones_b], axis=1), extras)):
            ob_half[p][half] = o
    ob = jnp.concatenate([jnp.where(low, o[0], o[1]) for o in ob_half], axis=-1)
    o_ref[:, :WIDTH_A] = _rms(oa, goa_ref[...]).astype(BF16)
    o_ref[:, WIDTH_A:] = _rms(ob, gob_ref[...]).astype(BF16)


def _attn_sample(sinks, qa, ka_cache, ka, va_cache, va, qb, kb_cache, kb, vb_cache, vb,
                 rbv, goa, gob, *, t_s, keep_a, keep_b, layer, prev_new, emit):
    n_tok = qa.shape[0]
    n_seq = n_tok // t_s
    la, lb = ka_cache.shape[2], kb_cache.shape[2]
    assert t_s <= keep_a <= la + t_s and t_s <= keep_b <= lb + t_s
    row = lambda i: (i, 0)
    cache = lambda l, rows, width: pl.BlockSpec((None, 1, rows, width // HEAD_DIM, HEAD_DIM),
                                                lambda i: (l, i, 0, 0, 0))
    new = lambda width: pl.BlockSpec((t_s, width), row)

    def kv_specs(l):
        return [cache(l, la, WIDTH_A), new(WIDTH_A), cache(l, la, WIDTH_A), new(WIDTH_A),
                cache(l, lb, KV_WIDTH_B), new(KV_WIDTH_B), cache(l, lb, KV_WIDTH_B), new(KV_WIDTH_B)]

    own = kv_specs(layer)
    in_specs = ([pl.BlockSpec(memory_space=pltpu.SMEM), new(WIDTH_A)] + own[:4] + [new(WIDTH_B)] + own[4:]
                + [_resident((N_HEADS_A, BIAS_PERIOD)), _resident((1, WIDTH_A)), _resident((1, WIDTH_B))])
    operands = [sinks, qa, ka_cache, ka, va_cache, va, qb, kb_cache, kb, vb_cache, vb, rbv, goa, gob]
    out_specs = [pl.BlockSpec((t_s, MIX_WIDTH), row)]
    out_shape = [jax.ShapeDtypeStruct((n_tok, MIX_WIDTH), BF16)]
    n_prev = 0
    if emit:
        n_prev = len(prev_new)
        for l, (ka_l, va_l, kb_l, vb_l) in enumerate(prev_new):
            in_specs += kv_specs(l)
            operands += [ka_cache, ka_l, va_cache, va_l, kb_cache, kb_l, vb_cache, vb_l]
        n_layers = n_prev + 1
        for keep, width in ((keep_a, WIDTH_A), (keep_a, WIDTH_A), (keep_b, KV_WIDTH_B), (keep_b, KV_WIDTH_B)):
            shape = (keep, width // HEAD_DIM, HEAD_DIM)
            out_specs.append(pl.BlockSpec((n_layers, 1) + shape, lambda i: (0, i, 0, 0, 0)))
            out_shape.append(jax.ShapeDtypeStruct((n_layers, n_seq) + shape, F32))
    return pl.pallas_call(
        functools.partial(_attn_sample_kernel, n_prev=n_prev, emit=emit),
        grid=(n_seq,),
        in_specs=in_specs,
        out_specs=out_specs,
        out_shape=out_shape,
        scratch_shapes=[pltpu.VMEM((N_HEADS_A, t_s, BIAS_PERIOD), F32)],
        compiler_params=pltpu.CompilerParams(
            dimension_semantics=("arbitrary",), vmem_limit_bytes=V7X_VMEM_LIMIT_BYTES),
        name="attn_sample",
    )(*operands)


TAIL_SUB_ROWS = 256


def _tail_kernel(h_ref, o_ref, p_ref, wout_ref, gffn_ref, wgu_ref, wdown_ref, wgate_ref,
                 wproj_ref, gfin_ref, out_ref, *, final, n_sub):
    rows = h_ref.shape[0] // n_sub
    for first in range(0, n_sub, 2):
        subs = [pl.ds(i * rows, rows) for i in range(first, first + 2)]
        h = [h_ref[sl, :] + _dot(o_ref[sl, :], wout_ref[...]) for sl in subs]
        pp = [_dot(p_ref[sl, :].astype(BF16), wproj_ref[...]) for sl in subs]
        gu = [_dot(_rms(x, gffn_ref[...]).astype(BF16), wgu_ref[...]) for x in h]
        act = [(jax.nn.silu(x[:, :D_FF]) * x[:, D_FF:]).astype(BF16) for x in gu]
        h = [x + _dot(a, wdown_ref[...]) for x, a in zip(h, act)]
        gate = [jax.nn.sigmoid(_dot(x.astype(BF16), wgate_ref[...])) for x in h]
        for sl, x, g, e in zip(subs, h, gate, pp):
            x = x + g * e
            if final:
                x = _rms(x, gfin_ref[...])
            out_ref[sl, :] = x


def _tail(h, o, p, wout, gffn, wgu, wdown, wgate, wproj, gfin, *, tm, final, layer):
    n_tok = h.shape[0]
    row = lambda i: (i, 0)
    return pl.pallas_call(
        functools.partial(_tail_kernel, final=final, n_sub=tm // TAIL_SUB_ROWS),
        grid=(n_tok // tm,),
        in_specs=[
            pl.BlockSpec((tm, D_MODEL), row),
            pl.BlockSpec((tm, MIX_WIDTH), row),
            pl.BlockSpec((None, tm, D_PLE), lambda i: (layer, i, 0)),
            _resident((MIX_WIDTH, D_MODEL), layer),
            _resident((1, D_MODEL)),
            _resident((D_MODEL, 2 * D_FF), layer),
            _resident((D_FF, D_MODEL), layer),
            _resident((D_MODEL, D_MODEL), layer),
            _resident((D_PLE, D_MODEL), layer),
            _resident((1, D_MODEL)),
        ],
        out_specs=pl.BlockSpec((tm, D_MODEL), row),
        out_shape=jax.ShapeDtypeStruct((n_tok, D_MODEL), F32),
        compiler_params=pltpu.CompilerParams(
            dimension_semantics=("arbitrary",), vmem_limit_bytes=V7X_VMEM_LIMIT_BYTES),
        name="layer_tail",
    )(h, o, p, wout, gffn, wgu, wdown, wgate, wproj, gfin)


CAST_ROWS = 256


def _cast_kernel(w_ref, o_ref):
    o_ref[...] = w_ref[...].astype(o_ref.dtype)


def _to_bf16(w):
    depth, rows, cols = w.shape
    blk = pl.BlockSpec((1, CAST_ROWS, cols), lambda d, r: (d, r, 0))
    return pl.pallas_call(
        _cast_kernel,
        grid=(depth, rows // CAST_ROWS),
        in_specs=[blk],
        out_specs=blk,
        out_shape=jax.ShapeDtypeStruct(w.shape, BF16),
        compiler_params=pltpu.CompilerParams(dimension_semantics=("arbitrary", "arbitrary")),
        name="cast_bf16",
    )(w)


def _rope_lane_tables(pos):
    half = ROT_DIM // 2
    d = np.arange(PAIR) % HEAD_DIM
    inv = ROPE_THETA ** (-(2.0 * (d % half)) / ROT_DIM)
    ang = np.where(d < ROT_DIM, np.asarray(pos, np.float64)[:, None] * inv[None, :], 0.0)
    lower = (d < half).astype(np.float64)[None, :]
    upper = ((d >= half) & (d < ROT_DIM)).astype(np.float64)[None, :]
    return np.cos(ang), np.sin(ang), lower, upper


def _rope_tables_direct(pos):
    cos, sin, lower, upper = _rope_lane_tables(pos)
    return jnp.asarray(np.stack([cos, -sin * lower, sin * upper]), F32)


def _rope_tables_split(n_tiles, tile):
    cr, sr, lower, upper = _rope_lane_tables(np.arange(tile))
    ct, st, _, _ = _rope_lane_tables(np.arange(n_tiles) * tile)
    rows = np.stack([cr, sr, -cr * lower, -sr * lower, cr * upper, sr * upper])
    return jnp.asarray(rows, F32), jnp.asarray(ct, F32), jnp.asarray(st, F32)


def _rel_bias_row(rel_bias):
    u = np.arange(BIAS_PERIOD)
    diff = np.where(u < BIAS_PERIOD // 2 + A_ROWS // 2, u, u - BIAS_PERIOD)
    idx = np.clip(A_ROWS - diff, -REL_CLIP, REL_CLIP) + REL_CLIP
    return rel_bias.astype(F32)[:, idx]


def kernel(x_prompt, x_sample, p_prompt, p_sample, cache_a_k, cache_a_v, cache_b_k, cache_b_v,
           g_mix_norm, w_in, rel_bias_a, sinks_b, g_out_a, g_out_b, w_out, g_ffn_norm,
           w_gate_up, w_down, w_ple_proj, w_ple_gate, g_final):
    b_p, s_p, _ = x_prompt.shape
    b_s, t_s, _ = x_sample.shape
    depth = w_in.shape[0]
    la_c, lb_c = cache_a_k.shape[2], cache_b_k.shape[2]
    keep_a_s = min(A_ROWS, la_c + t_s)
    keep_b_s = min(WINDOW_B, lb_c + t_s)
    tm_s = 512
    tm_tail = 512
    assert s_p % TQ == 0 and s_p >= A_ROWS and (b_s * t_s) % tm_s == 0 and tm_s % t_s == 0
    assert la_c == A_ROWS and lb_c == WINDOW_B and t_s == CHUNK

    rope_rows, rope_ct, rope_st = _rope_tables_split(s_p // TQ, TQ)
    rope_s = _rope_tables_direct(PAST_LEN + np.arange(tm_s) % t_s)
    row2 = lambda a: a.reshape(1, -1).astype(F32)
    g_fin = row2(g_final)

    hp = x_prompt
    hs = x_sample.reshape(b_s * t_s, D_MODEL)
    outs = [[] for _ in range(4)]
    new_rows = []
    w_in_b, w_out_b, w_gu_b, w_down_b, w_gate_b, w_proj_b = (
        _to_bf16(w) for w in (w_in, w_out, w_gate_up, w_down, w_ple_gate, w_ple_proj))
    p_prompt_f = p_prompt.reshape(depth, b_p * s_p, D_PLE)
    p_sample_f = p_sample.reshape(depth, b_s * t_s, D_PLE)
    for i in range(depth):
        tail_w = (w_out_b, row2(g_ffn_norm[i]), w_gu_b, w_down_b, w_gate_b, w_proj_b, g_fin)
        g_mix = row2(g_mix_norm[i])
        goa, gob = row2(g_out_a[i]), row2(g_out_b[i])
        sinks = sinks_b[i].astype(F32)
        rbv = _rel_bias_row(rel_bias_a[i])
        final = i == depth - 1

        o, cak, cav, cbk, cbv = _mix_prompt(sinks, hp, g_mix, w_in_b, rope_rows, rope_ct, rope_st,
                                            rbv, goa, gob, layer=i)
        hp = _tail(hp.reshape(b_p * s_p, D_MODEL), o.reshape(b_p * s_p, MIX_WIDTH),
                   p_prompt_f, *tail_w, tm=tm_tail, final=final, layer=i)
        hp = hp.reshape(b_p, s_p, D_MODEL)
        outs[0].append(cak.reshape(b_p, A_ROWS, N_HEADS_A, HEAD_DIM))
        outs[1].append(cav.reshape(b_p, A_ROWS, N_HEADS_A, HEAD_DIM))
        outs[2].append(cbk.reshape(b_p, WINDOW_B, N_KV_B, HEAD_DIM))
        outs[3].append(cbv.reshape(b_p, WINDOW_B, N_KV_B, HEAD_DIM))

        qa, ka, va, qb, kb, vb = _in_proj(hs, g_mix, w_in_b, rope_s, tm=tm_s, layer=i)
        o, *rolled = _attn_sample(sinks, qa, cache_a_k, ka, cache_a_v, va, qb, cache_b_k, kb, cache_b_v, vb,
                                  rbv, goa, gob, t_s=t_s, keep_a=keep_a_s, keep_b=keep_b_s,
                                  layer=i, prev_new=new_rows, emit=final)
        new_rows.append((ka, va, kb, vb))
        hs = _tail(hs, o, p_sample_f, *tail_w, tm=tm_tail, final=final, layer=i)

    y_sample = hs.reshape(b_s, t_s, D_MODEL)
    return (hp, y_sample) + tuple(jnp.stack(o) for o in outs) + tuple(rolled)
```

```python
import functools
import math

import numpy as np
import jax
import jax.numpy as jnp
from jax import lax
from jax.experimental import pallas as pl
from jax.experimental.pallas import tpu as pltpu

D_MODEL = 1024
CHUNK = 64
HEAD_DIM = 64
N_HEADS_A = 8
N_HEADS_B = 8
N_KV_B = 2
GROUP_B = N_HEADS_B // N_KV_B
WIDTH_A = N_HEADS_A * HEAD_DIM
WIDTH_B = N_HEADS_B * HEAD_DIM
KV_WIDTH_B = N_KV_B * HEAD_DIM
MIX_WIDTH = WIDTH_A + WIDTH_B
IN_WIDTH = 3 * WIDTH_A + WIDTH_B + 2 * KV_WIDTH_B
PREV_CHUNKS_A = 8
A_ROWS = PREV_CHUNKS_A * CHUNK
REL_CLIP = 256
WINDOW_B = 128
PREV_CHUNKS_B = WINDOW_B // CHUNK
ROT_DIM = HEAD_DIM // 4
ROPE_THETA = 500000.0
D_FF = 2816
D_PLE = 256
PAST_LEN = 2048
RMS_EPS = 1e-6
LOG2E = math.log2(math.e)
QSCALE = HEAD_DIM ** -0.5 * LOG2E
PAIR = 2 * HEAD_DIM
N_PAIRS_A = WIDTH_A // PAIR
N_PAIRS_B = WIDTH_B // PAIR
PAIRS_PER_KV = GROUP_B // 2
NEG_INF = float("-inf")
BIAS_PERIOD = 1024

V7X_VMEM_LIMIT_BYTES = 56 * 1024 * 1024

BF16 = jnp.bfloat16
F32 = jnp.float32

O_KA, O_VA, O_QB = WIDTH_A, 2 * WIDTH_A, 3 * WIDTH_A
O_KB = O_QB + WIDTH_B
O_VB = O_KB + KV_WIDTH_B


def _rms(x, g):
    return x * lax.rsqrt(jnp.mean(x * x, axis=-1, keepdims=True) + RMS_EPS) * g


def _dot(a, b):
    return jnp.dot(a, b, preferred_element_type=F32)


def _dot_nt(a, b):
    return lax.dot_general(a, b, (((1,), (1,)), ((), ())), preferred_element_type=F32)


def _resident(shape, layer=None):
    nd = len(shape)
    if layer is None:
        return pl.BlockSpec(shape, lambda *_: (0,) * nd, pipeline_mode=pl.Buffered(1))
    return pl.BlockSpec((None,) + tuple(shape), lambda *_: (layer,) + (0,) * nd,
                        pipeline_mode=pl.Buffered(1))


def _rope(x, cos, sdn, sup):
    half = ROT_DIM // 2
    return (x * cos + pltpu.roll(x, PAIR - half, axis=1) * sdn
            + pltpu.roll(x, half, axis=1) * sup)


def _low_lanes(shape):
    return lax.broadcasted_iota(jnp.int32, shape, 1) < HEAD_DIM


def _keep_half(x, half, fill):
    keep = _low_lanes(x.shape) if half == 0 else ~_low_lanes(x.shape)
    return jnp.where(keep, x, jnp.full_like(x, fill))


def _exp_scores(s_blocks, sink=None):
    m = functools.reduce(jnp.maximum, [jnp.max(s, axis=-1, keepdims=True) for s in s_blocks])
    if sink is not None:
        m = jnp.maximum(m, sink)
    p = jnp.concatenate([jnp.exp2(s - m).astype(BF16) for s in s_blocks], axis=-1)
    return p, (None if sink is None else jnp.exp2(sink - m))


def _pv_stacked(probs, v_ext, extras=None):
    m = probs[0].shape[0]
    x = _dot(jnp.concatenate(probs, axis=0), v_ext)
    outs = []
    for i in range(len(probs)):
        den = x[i * m:(i + 1) * m, PAIR:]
        if extras is not None:
            den = den + extras[i]
        outs.append(x[i * m:(i + 1) * m, :PAIR] / den)
    return outs


def _rolled_bias_rows(rbv_ref, h, rows):
    x = jnp.broadcast_to(rbv_ref[h:h + 1, :] * LOG2E, (rows, BIAS_PERIOD))
    return pltpu.roll(x, 0, axis=1, stride=1, stride_axis=0)


def _band_valid(rows, col0, cols, prev_chunks):
    r = lax.broadcasted_iota(jnp.int32, (rows, cols), 0) // CHUNK
    c = (lax.broadcasted_iota(jnp.int32, (rows, cols), 1) + col0) // CHUNK
    return (c >= r) & (c <= r + prev_chunks)


TQ = 256
NBLK_A = A_ROWS // TQ + 1
WIN_A = NBLK_A * TQ
WIN_B = WINDOW_B + TQ
NEG_KIND = NBLK_A
HALF_B = WINDOW_B
KEYS_B = HALF_B + WINDOW_B


def _mix_prompt_kernel(sinks_ref, x_ref, g_ref, w_ref, rope_ref, ct_ref, st_ref, rbv_ref,
                       goa_ref, gob_ref, wout_ref,
                       o_ref, cak_ref, cav_ref, cbk_ref, cbv_ref,
                       ka_scr, va_scr, kbp_scr, vbp_scr, bias_scr, maskb_scr):
    b, t = pl.program_id(0), pl.program_id(1)

    @pl.when((b == 0) & (t == 0))
    def _init():
        ka_scr[...] = jnp.zeros_like(ka_scr)
        kbp_scr[...] = jnp.zeros_like(kbp_scr)
        for v_scr in (va_scr, vbp_scr):
            v_scr[..., :PAIR] = jnp.zeros(v_scr.shape[:-1] + (PAIR,), BF16)
            v_scr[..., PAIR:] = jnp.ones(v_scr.shape[:-1] + (PAIR,), BF16)
        for h in range(N_HEADS_A):
            rows = _rolled_bias_rows(rbv_ref, h, TQ)
            for j in range(NBLK_A):
                valid = _band_valid(TQ, j * TQ, TQ, PREV_CHUNKS_A)
                bias_scr[j, h] = jnp.where(valid, rows[:, j * TQ:(j + 1) * TQ], NEG_INF)
            bias_scr[NEG_KIND, h] = jnp.full((TQ, TQ), NEG_INF, F32)
        band = jnp.where(_band_valid(HALF_B, 0, KEYS_B, PREV_CHUNKS_B),
                         jnp.zeros((HALF_B, KEYS_B), F32), NEG_INF)
        col = lax.broadcasted_iota(jnp.int32, (HALF_B, KEYS_B), 1)
        maskb_scr[0] = band
        maskb_scr[1] = jnp.where(col >= WINDOW_B, band, NEG_INF)

    n = _rms(x_ref[0], g_ref[...]).astype(BF16)
    slot = lax.rem(t, NBLK_A)
    row0 = pl.multiple_of(slot * TQ, TQ)
    ka = _dot(n, w_ref[:, O_KA:O_VA])
    for p in range(N_PAIRS_A):
        ka_scr[p, pl.ds(row0, TQ), :] = ka[:, p * PAIR:(p + 1) * PAIR].astype(BF16)
    za = _dot(n, w_ref[:, :O_KA])
    qa = [[_keep_half(za[:, j * PAIR:(j + 1) * PAIR] * QSCALE, half, 0.0).astype(BF16)
           for half in range(2)] for j in range(N_PAIRS_A)]
    va = _dot(n, w_ref[:, O_VA:O_QB])
    for p in range(N_PAIRS_A):
        va_scr[p, pl.ds(row0, TQ), :PAIR] = va[:, p * PAIR:(p + 1) * PAIR].astype(BF16)
    zb = _dot(n, w_ref[:, O_QB:])
    ct, st = ct_ref[pl.ds(t, 1), :], st_ref[pl.ds(t, 1), :]
    cos = ct * rope_ref[0] - st * rope_ref[1]
    sdn = st * rope_ref[2] + ct * rope_ref[3]
    sup = st * rope_ref[4] + ct * rope_ref[5]
    kb = _rope(zb[:, WIDTH_B:WIDTH_B + KV_WIDTH_B], cos, sdn, sup)
    vb = zb[:, WIDTH_B + KV_WIDTH_B:]
    qb = []
    for j in range(N_PAIRS_B):
        q = _rope(zb[:, j * PAIR:(j + 1) * PAIR], cos, sdn, sup) * QSCALE
        qb.append([_keep_half(q, half, 0.0).astype(BF16) for half in range(2)])

    cak_ref[0] = ka
    cav_ref[0] = va
    cbk_ref[0] = kb[TQ - WINDOW_B:, :]
    cbv_ref[0] = vb[TQ - WINDOW_B:, :]

    kinds = []
    for ps in range(NBLK_A):
        d = lax.rem(t - ps + NBLK_A, NBLK_A)
        kinds.append(jnp.where(t >= d, NBLK_A - 1 - d, NEG_KIND))
    low = _low_lanes((TQ, PAIR))
    oa = []
    for p in range(N_PAIRS_A):
        s = _dot_nt(jnp.concatenate(qa[p], axis=0), ka_scr[p])
        probs = []
        for half in range(2):
            s_blocks = [s[half * TQ:(half + 1) * TQ, ps * TQ:(ps + 1) * TQ]
                        + bias_scr[kinds[ps], 2 * p + half] for ps in range(NBLK_A)]
            probs.append(_exp_scores(s_blocks)[0])
        o = _pv_stacked(probs, va_scr[p])
        oa.append(jnp.where(low, o[0], o[1]))
    oa = jnp.concatenate(oa, axis=-1)

    kb_sw, vb_sw = pltpu.roll(kb, HEAD_DIM, axis=1), pltpu.roll(vb, HEAD_DIM, axis=1)
    ones = jnp.ones((TQ, PAIR), BF16)
    slot_b = lax.rem(t, 2)
    mask_first = maskb_scr[jnp.where(t >= 1, 0, 1)]
    ob_rows = [[[None] * (TQ // HALF_B) for _ in range(2)] for _ in range(N_PAIRS_B)]
    for c, (k_c, v_c) in enumerate(((kb, vb), (kb_sw, vb_sw))):
        k_c, v_c = k_c.astype(BF16), v_c.astype(BF16)
        kbp_scr[c, slot_b] = k_c[TQ - WINDOW_B:, :]
        vbp_scr[c, slot_b, :, :PAIR] = v_c[TQ - WINDOW_B:, :]
        k_win = jnp.concatenate([kbp_scr[c, 1 - slot_b], k_c], axis=0)
        v_win = jnp.concatenate([vbp_scr[c, 1 - slot_b], jnp.concatenate([v_c, ones], axis=1)], axis=0)
        heads = [(p, half) for p in range(N_PAIRS_B) for half in range(2)
                 if (half == p // PAIRS_PER_KV) == (c == 0)]
        for r in range(TQ // HALF_B):
            rows, keys = slice(r * HALF_B, (r + 1) * HALF_B), slice(r * HALF_B, r * HALF_B + KEYS_B)
            mask = mask_first if r == 0 else maskb_scr[0]
            s = _dot_nt(jnp.concatenate([qb[p][half][rows] for p, half in heads], axis=0), k_win[keys])
            probs, extras = [], []
            for i, (p, half) in enumerate(heads):
                pr, ex = _exp_scores([s[i * HALF_B:(i + 1) * HALF_B, :] + mask],
                                     sinks_ref[2 * p + half] * LOG2E)
                probs.append(pr)
                extras.append(ex)
            for (p, half), o in zip(heads, _pv_stacked(probs, v_win[keys], extras)):
                ob_rows[p][half][r] = o
    ob = jnp.concatenate([jnp.where(low, jnp.concatenate(o[0], axis=0), jnp.concatenate(o[1], axis=0))
                          for o in ob_rows], axis=-1)

    mixed = jnp.concatenate([_rms(oa, goa_ref[...]).astype(BF16), _rms(ob, gob_ref[...]).astype(BF16)],
                            axis=-1)
    o_ref[0] = x_ref[0] + _dot(mixed, wout_ref[...])


def _mix_prompt(sinks, x, g, w_bf16, rope_rows, ct, st, rbv, goa, gob, wout, *, layer):
    b, s, _ = x.shape
    nt = s // TQ
    cur = lambda bi, t: (bi, t, 0)
    keep_a = lambda bi, t: (bi, jnp.maximum(t - (nt - A_ROWS // TQ), 0), 0)
    seq = lambda bi, t: (bi, 0, 0)
    return pl.pallas_call(
        _mix_prompt_kernel,
        grid=(b, nt),
        in_specs=[
            pl.BlockSpec(memory_space=pltpu.SMEM),
            pl.BlockSpec((1, TQ, D_MODEL), cur),
            _resident((1, D_MODEL)),
            _resident((D_MODEL, IN_WIDTH), layer),
            _resident(rope_rows.shape),
            _resident(ct.shape),
            _resident(st.shape),
            _resident((N_HEADS_A, BIAS_PERIOD)),
            _resident((1, WIDTH_A)),
            _resident((1, WIDTH_B)),
            _resident((MIX_WIDTH, D_MODEL), layer),
        ],
        out_specs=[
            pl.BlockSpec((1, TQ, D_MODEL), cur),
            pl.BlockSpec((1, TQ, WIDTH_A), keep_a),
            pl.BlockSpec((1, TQ, WIDTH_A), keep_a),
            pl.BlockSpec((1, WINDOW_B, KV_WIDTH_B), seq),
            pl.BlockSpec((1, WINDOW_B, KV_WIDTH_B), seq),
        ],
        out_shape=[
            jax.ShapeDtypeStruct((b, s, D_MODEL), F32),
            jax.ShapeDtypeStruct((b, A_ROWS, WIDTH_A), F32),
            jax.ShapeDtypeStruct((b, A_ROWS, WIDTH_A), F32),
            jax.ShapeDtypeStruct((b, WINDOW_B, KV_WIDTH_B), F32),
            jax.ShapeDtypeStruct((b, WINDOW_B, KV_WIDTH_B), F32),
        ],
        scratch_shapes=[
            pltpu.VMEM((N_PAIRS_A, WIN_A, PAIR), BF16),
            pltpu.VMEM((N_PAIRS_A, WIN_A, 2 * PAIR), BF16),
            pltpu.VMEM((2, 2, WINDOW_B, PAIR), BF16),
            pltpu.VMEM((2, 2, WINDOW_B, 2 * PAIR), BF16),
            pltpu.VMEM((NBLK_A + 1, N_HEADS_A, TQ, TQ), F32),
            pltpu.VMEM((2, HALF_B, KEYS_B), F32),
        ],
        compiler_params=pltpu.CompilerParams(
            dimension_semantics=("arbitrary", "arbitrary"),
            vmem_limit_bytes=V7X_VMEM_LIMIT_BYTES),
        name="mix_prompt",
    )(sinks, x, g, w_bf16, rope_rows, ct, st, rbv, goa, gob, wout)


def _in_proj_kernel(x_ref, g_ref, w_ref, rope_ref, qa_ref, ka_ref, va_ref, qb_ref, kb_ref, vb_ref):
    n = _rms(x_ref[...], g_ref[...]).astype(BF16)
    z = _dot(n, w_ref[...])
    cos, sdn, sup = rope_ref[0], rope_ref[1], rope_ref[2]
    qa_ref[...] = (z[:, :O_KA] * QSCALE).astype(BF16)
    ka_ref[...] = z[:, O_KA:O_VA]
    va_ref[...] = z[:, O_VA:O_QB]
    for j in range(N_PAIRS_B):
        sl = slice(O_QB + j * PAIR, O_QB + (j + 1) * PAIR)
        qb_ref[:, j * PAIR:(j + 1) * PAIR] = (_rope(z[:, sl], cos, sdn, sup) * QSCALE).astype(BF16)
    kb_ref[...] = _rope(z[:, O_KB:O_VB], cos, sdn, sup)
    vb_ref[...] = z[:, O_VB:]


def _in_proj(x, g, w_bf16, rope_tabs, *, tm, layer):
    n_tok = x.shape[0]
    row = lambda i: (i, 0)
    widths = (WIDTH_A, WIDTH_A, WIDTH_A, WIDTH_B, KV_WIDTH_B, KV_WIDTH_B)
    dtypes = (BF16, F32, F32, BF16, F32, F32)
    return pl.pallas_call(
        _in_proj_kernel,
        grid=(n_tok // tm,),
        in_specs=[
            pl.BlockSpec((tm, D_MODEL), row),
            _resident((1, D_MODEL)),
            _resident((D_MODEL, IN_WIDTH), layer),
            _resident(rope_tabs.shape),
        ],
        out_specs=[pl.BlockSpec((tm, w), row) for w in widths],
        out_shape=[jax.ShapeDtypeStruct((n_tok, w), d) for w, d in zip(widths, dtypes)],
        compiler_params=pltpu.CompilerParams(
            dimension_semantics=("arbitrary",), vmem_limit_bytes=V7X_VMEM_LIMIT_BYTES),
        name="in_proj",
    )(x, g, w_bf16, rope_tabs)


N_SAMPLE_IN = 14
N_ROLL_IN = 8


def _attn_sample_kernel(*refs, n_prev, emit):
    (sinks_ref, qa_ref, kac_ref, kan_ref, vac_ref, van_ref, qb_ref, kbc_ref, kbn_ref, vbc_ref,
     vbn_ref, rbv_ref, goa_ref, gob_ref) = refs[:N_SAMPLE_IN]
    n_in = N_SAMPLE_IN + N_ROLL_IN * n_prev
    o_ref, bias_scr = refs[n_in], refs[-1]
    t_s = qa_ref.shape[0]
    la = kac_ref.shape[1]

    @pl.when(pl.program_id(0) == 0)
    def _init():
        for h in range(N_HEADS_A):
            bias_scr[h] = _rolled_bias_rows(rbv_ref, h, t_s)

    if emit:
        own = (kac_ref, kan_ref, vac_ref, van_ref, kbc_ref, kbn_ref, vbc_ref, vbn_ref)
        layers = [refs[N_SAMPLE_IN + N_ROLL_IN * l:N_SAMPLE_IN + N_ROLL_IN * (l + 1)]
                  for l in range(n_prev)] + [own]
        for l, lr in enumerate(layers):
            for j, new_ref in enumerate(refs[n_in + 1:n_in + 5]):
                cache_ref, fresh_ref = lr[2 * j], lr[2 * j + 1]
                kept = new_ref.shape[2] - t_s
                new_ref[l, 0, :kept, :] = cache_ref[0, cache_ref.shape[1] - kept:, :]
                new_ref[l, 0, kept:, :] = fresh_ref[...]

    low = _low_lanes((t_s, PAIR))
    ones_a = jnp.ones((la + t_s, PAIR), BF16)
    oa = []
    for p in range(N_PAIRS_A):
        sl = slice(p * PAIR, (p + 1) * PAIR)
        qp = qa_ref[:, sl]
        k = jnp.concatenate([kac_ref[0, :, sl], kan_ref[:, sl]], axis=0).astype(BF16)
        v = jnp.concatenate([vac_ref[0, :, sl], van_ref[:, sl]], axis=0).astype(BF16)
        s = _dot_nt(jnp.concatenate([_keep_half(qp, half, 0.0) for half in range(2)], axis=0), k)
        probs = [_exp_scores([s[half * t_s:(half + 1) * t_s, :] + bias_scr[2 * p + half, :, :la + t_s]])[0]
                 for half in range(2)]
        o = _pv_stacked(probs, jnp.concatenate([v, ones_a], axis=1))
        oa.append(jnp.where(low, o[0], o[1]))
    oa = jnp.concatenate(oa, axis=-1)

    kb = jnp.concatenate([kbc_ref[0], kbn_ref[...]], axis=0)
    vb = jnp.concatenate([vbc_ref[0], vbn_ref[...]], axis=0)
    ones_b = jnp.ones((kb.shape[0], PAIR), BF16)
    ob_half = [[None, None] for _ in range(N_PAIRS_B)]
    for c in range(2):
        k_c = (kb if c == 0 else pltpu.roll(kb, HEAD_DIM, axis=1)).astype(BF16)
        v_c = (vb if c == 0 else pltpu.roll(vb, HEAD_DIM, axis=1)).astype(BF16)
        heads = [(p, half) for p in range(N_PAIRS_B) for half in range(2)
                 if (half == p // PAIRS_PER_KV) == (c == 0)]
        s = _dot_nt(jnp.concatenate([_keep_half(qb_ref[:, p * PAIR:(p + 1) * PAIR], half, 0.0)
                                     for p, half in heads], axis=0), k_c)
        probs, extras = [], []
        for i, (p, half) in enumerate(heads):
            pr, ex = _exp_scores([s[i * t_s:(i + 1) * t_s, :]], sinks_ref[2 * p + half] * LOG2E)
            probs.append(pr)
            extras.append(ex)
        for (p, half), o in zip(heads, _pv_stacked(probs, jnp.concatenate([v_c, ones_b], axis=1), extras)):
            ob_half[p][half] = o
    ob = jnp.concatenate([jnp.where(low, o[0], o[1]) for o in ob_half], axis=-1)
    o_ref[:, :WIDTH_A] = _rms(oa, goa_ref[...]).astype(BF16)
    o_ref[:, WIDTH_A:] = _rms(ob, gob_ref[...]).astype(BF16)


def _attn_sample(sinks, qa, ka_cache, ka, va_cache, va, qb, kb_cache, kb, vb_cache, vb,
                 rbv, goa, gob, *, t_s, keep_a, keep_b, layer, prev_new, emit):
    n_tok = qa.shape[0]
    n_seq = n_tok // t_s
    la, lb = ka_cache.shape[2], kb_cache.shape[2]
    assert t_s <= keep_a <= la + t_s and t_s <= keep_b <= lb + t_s
    row = lambda i: (i, 0)
    cache = lambda l, rows, width: pl.BlockSpec((None, 1, rows, width), lambda i: (l, i, 0, 0))
    new = lambda width: pl.BlockSpec((t_s, width), row)

    def kv_specs(l):
        return [cache(l, la, WIDTH_A), new(WIDTH_A), cache(l, la, WIDTH_A), new(WIDTH_A),
                cache(l, lb, KV_WIDTH_B), new(KV_WIDTH_B), cache(l, lb, KV_WIDTH_B), new(KV_WIDTH_B)]

    own = kv_specs(layer)
    in_specs = ([pl.BlockSpec(memory_space=pltpu.SMEM), new(WIDTH_A)] + own[:4] + [new(WIDTH_B)] + own[4:]
                + [_resident((N_HEADS_A, BIAS_PERIOD)), _resident((1, WIDTH_A)), _resident((1, WIDTH_B))])
    operands = [sinks, qa, ka_cache, ka, va_cache, va, qb, kb_cache, kb, vb_cache, vb, rbv, goa, gob]
    out_specs = [pl.BlockSpec((t_s, MIX_WIDTH), row)]
    out_shape = [jax.ShapeDtypeStruct((n_tok, MIX_WIDTH), BF16)]
    n_prev = 0
    if emit:
        n_prev = len(prev_new)
        for l, (ka_l, va_l, kb_l, vb_l) in enumerate(prev_new):
            in_specs += kv_specs(l)
            operands += [ka_cache, ka_l, va_cache, va_l, kb_cache, kb_l, vb_cache, vb_l]
        n_layers = n_prev + 1
        for keep, width in ((keep_a, WIDTH_A), (keep_a, WIDTH_A), (keep_b, KV_WIDTH_B), (keep_b, KV_WIDTH_B)):
            out_specs.append(pl.BlockSpec((n_layers, 1, keep, width), lambda i: (0, i, 0, 0)))
            out_shape.append(jax.ShapeDtypeStruct((n_layers, n_seq, keep, width), F32))
    return pl.pallas_call(
        functools.partial(_attn_sample_kernel, n_prev=n_prev, emit=emit),
        grid=(n_seq,),
        in_specs=in_specs,
        out_specs=out_specs,
        out_shape=out_shape,
        scratch_shapes=[pltpu.VMEM((N_HEADS_A, t_s, BIAS_PERIOD), F32)],
        compiler_params=pltpu.CompilerParams(
            dimension_semantics=("arbitrary",), vmem_limit_bytes=V7X_VMEM_LIMIT_BYTES),
        name="attn_sample",
    )(*operands)


TAIL_SUB_ROWS = 256


def _tail_kernel(*refs, final, n_sub, out_proj):
    if out_proj:
        (h_ref, o_ref, wout_ref, p_ref, gffn_ref, wgu_ref, wdown_ref, wgate_ref, wproj_ref, gfin_ref,
         out_ref) = refs
    else:
        h_ref, p_ref, gffn_ref, wgu_ref, wdown_ref, wgate_ref, wproj_ref, gfin_ref, out_ref = refs
    rows = h_ref.shape[0] // n_sub
    for first in range(0, n_sub, 2):
        subs = [pl.ds(i * rows, rows) for i in range(first, first + 2)]
        if out_proj:
            h = [h_ref[sl, :] + _dot(o_ref[sl, :], wout_ref[...]) for sl in subs]
        else:
            h = [h_ref[sl, :] for sl in subs]
        pp = [_dot(p_ref[sl, :].astype(BF16), wproj_ref[...]) for sl in subs]
        gu = [_dot(_rms(x, gffn_ref[...]).astype(BF16), wgu_ref[...]) for x in h]
        act = [(jax.nn.silu(x[:, :D_FF]) * x[:, D_FF:]).astype(BF16) for x in gu]
        h = [x + _dot(a, wdown_ref[...]) for x, a in zip(h, act)]
        gate = [jax.nn.sigmoid(_dot(x.astype(BF16), wgate_ref[...])) for x in h]
        for sl, x, g, e in zip(subs, h, gate, pp):
            x = x + g * e
            if final:
                x = _rms(x, gfin_ref[...])
            out_ref[sl, :] = x


def _tail(h, o, p, wout, gffn, wgu, wdown, wgate, wproj, gfin, *, tm, final, layer):
    n_tok = h.shape[0]
    row = lambda i: (i, 0)
    out_proj = o is not None
    mix_specs = [pl.BlockSpec((tm, MIX_WIDTH), row), _resident((MIX_WIDTH, D_MODEL), layer)]
    return pl.pallas_call(
        functools.partial(_tail_kernel, final=final, n_sub=tm // TAIL_SUB_ROWS, out_proj=out_proj),
        grid=(n_tok // tm,),
        in_specs=[pl.BlockSpec((tm, D_MODEL), row)] + (mix_specs if out_proj else []) + [
            pl.BlockSpec((None, tm, D_PLE), lambda i: (layer, i, 0)),
            _resident((1, D_MODEL)),
            _resident((D_MODEL, 2 * D_FF), layer),
            _resident((D_FF, D_MODEL), layer),
            _resident((D_MODEL, D_MODEL), layer),
            _resident((D_PLE, D_MODEL), layer),
            _resident((1, D_MODEL)),
        ],
        out_specs=pl.BlockSpec((tm, D_MODEL), row),
        out_shape=jax.ShapeDtypeStruct((n_tok, D_MODEL), F32),
        compiler_params=pltpu.CompilerParams(
            dimension_semantics=("arbitrary",), vmem_limit_bytes=V7X_VMEM_LIMIT_BYTES),
        name="layer_tail",
    )(h, *((o, wout) if out_proj else ()), p, gffn, wgu, wdown, wgate, wproj, gfin)


CAST_ROWS = 256


def _cast_kernel(w_ref, o_ref):
    o_ref[...] = w_ref[...].astype(o_ref.dtype)


def _to_bf16(w):
    depth, rows, cols = w.shape
    blk = pl.BlockSpec((1, CAST_ROWS, cols), lambda d, r: (d, r, 0))
    return pl.pallas_call(
        _cast_kernel,
        grid=(depth, rows // CAST_ROWS),
        in_specs=[blk],
        out_specs=blk,
        out_shape=jax.ShapeDtypeStruct(w.shape, BF16),
        compiler_params=pltpu.CompilerParams(dimension_semantics=("arbitrary", "arbitrary")),
        name="cast_bf16",
    )(w)


def _rope_lane_tables(pos):
    half = ROT_DIM // 2
    d = np.arange(PAIR) % HEAD_DIM
    inv = ROPE_THETA ** (-(2.0 * (d % half)) / ROT_DIM)
    ang = np.where(d < ROT_DIM, np.asarray(pos, np.float64)[:, None] * inv[None, :], 0.0)
    lower = (d < half).astype(np.float64)[None, :]
    upper = ((d >= half) & (d < ROT_DIM)).astype(np.float64)[None, :]
    return np.cos(ang), np.sin(ang), lower, upper


def _rope_tables_direct(pos):
    cos, sin, lower, upper = _rope_lane_tables(pos)
    return jnp.asarray(np.stack([cos, -sin * lower, sin * upper]), F32)


def _rope_tables_split(n_tiles, tile):
    cr, sr, lower, upper = _rope_lane_tables(np.arange(tile))
    ct, st, _, _ = _rope_lane_tables(np.arange(n_tiles) * tile)
    rows = np.stack([cr, sr, -cr * lower, -sr * lower, cr * upper, sr * upper])
    return jnp.asarray(rows, F32), jnp.asarray(ct, F32), jnp.asarray(st, F32)


def _rel_bias_row(rel_bias):
    u = np.arange(BIAS_PERIOD)
    diff = np.where(u < BIAS_PERIOD // 2 + A_ROWS // 2, u, u - BIAS_PERIOD)
    idx = np.clip(A_ROWS - diff, -REL_CLIP, REL_CLIP) + REL_CLIP
    return rel_bias.astype(F32)[:, idx]


def kernel(x_prompt, x_sample, p_prompt, p_sample, cache_a_k, cache_a_v, cache_b_k, cache_b_v,
           g_mix_norm, w_in, rel_bias_a, sinks_b, g_out_a, g_out_b, w_out, g_ffn_norm,
           w_gate_up, w_down, w_ple_proj, w_ple_gate, g_final):
    b_p, s_p, _ = x_prompt.shape
    b_s, t_s, _ = x_sample.shape
    depth = w_in.shape[0]
    la_c, lb_c = cache_a_k.shape[2], cache_b_k.shape[2]
    keep_a_s = min(A_ROWS, la_c + t_s)
    keep_b_s = min(WINDOW_B, lb_c + t_s)
    tm_s = 512
    tm_tail = 512
    assert s_p % TQ == 0 and s_p >= A_ROWS and (b_s * t_s) % tm_s == 0 and tm_s % t_s == 0
    assert la_c == A_ROWS and lb_c == WINDOW_B and t_s == CHUNK

    rope_rows, rope_ct, rope_st = _rope_tables_split(s_p // TQ, TQ)
    rope_s = _rope_tables_direct(PAST_LEN + np.arange(tm_s) % t_s)
    row2 = lambda a: a.reshape(1, -1).astype(F32)
    g_fin = row2(g_final)

    hp = x_prompt
    hs = x_sample.reshape(b_s * t_s, D_MODEL)
    outs = [[] for _ in range(4)]
    new_rows = []
    w_in_b, w_out_b, w_gu_b, w_down_b, w_gate_b, w_proj_b = (
        _to_bf16(w) for w in (w_in, w_out, w_gate_up, w_down, w_ple_gate, w_ple_proj))
    p_prompt_f = p_prompt.reshape(depth, b_p * s_p, D_PLE)
    p_sample_f = p_sample.reshape(depth, b_s * t_s, D_PLE)
    cak_s = cache_a_k.reshape(depth, b_s, la_c, WIDTH_A)
    cav_s = cache_a_v.reshape(depth, b_s, la_c, WIDTH_A)
    cbk_s = cache_b_k.reshape(depth, b_s, lb_c, KV_WIDTH_B)
    cbv_s = cache_b_v.reshape(depth, b_s, lb_c, KV_WIDTH_B)
    for i in range(depth):
        tail_w = (w_out_b, row2(g_ffn_norm[i]), w_gu_b, w_down_b, w_gate_b, w_proj_b, g_fin)
        g_mix = row2(g_mix_norm[i])
        goa, gob = row2(g_out_a[i]), row2(g_out_b[i])
        sinks = sinks_b[i].astype(F32)
        rbv = _rel_bias_row(rel_bias_a[i])
        final = i == depth - 1

        hp, cak, cav, cbk, cbv = _mix_prompt(sinks, hp, g_mix, w_in_b, rope_rows, rope_ct, rope_st,
                                             rbv, goa, gob, w_out_b, layer=i)
        hp = _tail(hp.reshape(b_p * s_p, D_MODEL), None,
                   p_prompt_f, *tail_w, tm=tm_tail, final=final, layer=i)
        hp = hp.reshape(b_p, s_p, D_MODEL)
        outs[0].append(cak.reshape(b_p, A_ROWS, N_HEADS_A, HEAD_DIM))
        outs[1].append(cav.reshape(b_p, A_ROWS, N_HEADS_A, HEAD_DIM))
        outs[2].append(cbk.reshape(b_p, WINDOW_B, N_KV_B, HEAD_DIM))
        outs[3].append(cbv.reshape(b_p, WINDOW_B, N_KV_B, HEAD_DIM))

        qa, ka, va, qb, kb, vb = _in_proj(hs, g_mix, w_in_b, rope_s, tm=tm_s, layer=i)
        o, *rolled = _attn_sample(sinks, qa, cak_s, ka, cav_s, va, qb, cbk_s, kb, cbv_s, vb,
                                  rbv, goa, gob, t_s=t_s, keep_a=keep_a_s, keep_b=keep_b_s,
                                  layer=i, prev_new=new_rows, emit=final)
        new_rows.append((ka, va, kb, vb))
        hs = _tail(hs, o, p_sample_f, *tail_w, tm=tm_tail, final=final, layer=i)

    nak, nav, nbk, nbv = rolled
    y_sample = hs.reshape(b_s, t_s, D_MODEL)
    return ((hp, y_sample) + tuple(jnp.stack(o) for o in outs)
            + (nak.reshape(depth, b_s, keep_a_s, N_HEADS_A, HEAD_DIM),
               nav.reshape(depth, b_s, keep_a_s, N_HEADS_A, HEAD_DIM),
               nbk.reshape(depth, b_s, keep_b_s, N_KV_B, HEAD_DIM),
               nbv.reshape(depth, b_s, keep_b_s, N_KV_B, HEAD_DIM)))
```
